```python
import math
import jax, jax.numpy as jnp
from jax import lax
import numpy as np

D_MODEL = 1024
BATCH = 2
SEQ = 8192
DEPTH = 1
DEC_BATCH = 128
DEC_SEQ = 8
PAST_LEN = 8192
PAGE_SIZE = 128

GLA_HEADS = 4
GLA_DK = D_MODEL // 2 // GLA_HEADS
GLA_DV = D_MODEL // GLA_HEADS
GLA_QK = GLA_HEADS * GLA_DK
GLA_V = GLA_HEADS * GLA_DV
GLA_RANK = 16
GLA_NORMALIZER = 16.0
GLA_CHUNK = 16
DIL_GROUPS = ((128, 1), (512, 4), (2048, 16))
N_GROUPS = len(DIL_GROUPS)
DIL_HEADS = 8
DIL_HD = 64
DIL_WIDTH = DIL_HEADS * DIL_HD
DIL_QKV = N_GROUPS * DIL_WIDTH
Q_BLOCK = 128
ROPE_THETA = 10000.0
D_FF = (8 * D_MODEL // 3 + 127) // 128 * 128
CONV_W = 3
PLE_DIM = 256
ALPHA = (2.0 * DEPTH) ** 0.25
BETA = (8.0 * DEPTH) ** -0.25
NORM_EPS = 1e-5
IN_SPLITS = (GLA_QK, GLA_QK, GLA_V, GLA_V, GLA_RANK, DIL_QKV, DIL_QKV, DIL_QKV, D_MODEL, D_MODEL)
IN_COLS = sum(IN_SPLITS)
SPLIT_AT = tuple(int(s) for s in np.cumsum(IN_SPLITS)[:-1])

kernel_name = 'hybrid_gla_dilated_convffn_decoder'


def layer_norm(x, g, b):
    xf = x.astype(jnp.float32)
    mu = jnp.mean(xf, -1, keepdims=True)
    var = jnp.mean(jnp.square(xf - mu), -1, keepdims=True)
    y = (xf - mu) * lax.rsqrt(var + NORM_EPS) * g.astype(jnp.float32) + b.astype(jnp.float32)
    return y.astype(x.dtype)


def rope(x, pos):
    half = x.shape[-1] // 2
    inv = ROPE_THETA ** (-jnp.arange(half, dtype=jnp.float32) / half)
    ang = pos.astype(jnp.float32)[:, None] * inv[None, :]
    cos = jnp.cos(ang)[None, :, None, :]
    sin = jnp.sin(ang)[None, :, None, :]
    xf = x.astype(jnp.float32)
    x1, x2 = xf[..., :half], xf[..., half:]
    return jnp.concatenate([x1 * cos - x2 * sin, x2 * cos + x1 * sin], -1).astype(x.dtype)


def gla_recurrence(q, k, v, logd, s0):
    B, T, H, _ = q.shape
    DV = v.shape[-1]
    C = GLA_CHUNK
    pad = (-T) % C
    n = (T + pad) // C

    def blocks(a):
        a = jnp.pad(a.astype(jnp.float32), ((0, 0), (0, pad), (0, 0), (0, 0)))
        return a.reshape(B, n, C, H, a.shape[-1]).transpose(1, 0, 3, 2, 4)

    qc, kc, vc, gc = blocks(q), blocks(k), blocks(v), blocks(logd)
    causal = jnp.tril(jnp.ones((C, C), bool))

    def step(S, inp):
        qb, kb, vb, gb = inp
        b = jnp.cumsum(gb, axis=2)
        rel = jnp.where(causal[None, None, :, :, None],
                        b[:, :, :, None, :] - b[:, :, None, :, :], -jnp.inf)
        A = jnp.einsum('bhtk,bhsk,bhtsk->bhts', qb, kb, jnp.exp(rel))
        o = jnp.einsum('bhtk,bhkv->bhtv', qb * jnp.exp(b), S) + jnp.einsum('bhts,bhsv->bhtv', A, vb)
        b_last = b[:, :, -1, :]
        S = jnp.exp(b_last)[..., None] * S + jnp.einsum(
            'bhsk,bhsv->bhkv', kb * jnp.exp(b_last[:, :, None, :] - b), vb)
        return S, o

    S, o = lax.scan(step, s0.astype(jnp.float32), (qc, kc, vc, gc))
    o = o.transpose(1, 0, 3, 2, 4).reshape(B, n * C, H, DV)[:, :T]
    return o, S


def dilated_attention(qs, ks, vs, offsets, block):
    B, T, H, hd = qs[0].shape
    nb = T // block
    scale = DIL_HD ** -0.5

    def one_block(bi):
        t0 = bi * block
        outs, lses = [], []
        for (w, r), q, k, v, off in zip(DIL_GROUPS, qs, ks, vs, offsets):
            n_keys = w // r + 1
            qb = lax.dynamic_slice_in_dim(q, t0, block, axis=1).astype(jnp.float32)
            idx = off + t0 + jnp.arange(block)[:, None] - r * jnp.arange(n_keys)[None, :]
            valid = idx >= 0
            idx = jnp.maximum(idx, 0)
            kg = jnp.take(k, idx, axis=1).astype(jnp.float32)
            vg = jnp.take(v, idx, axis=1).astype(jnp.float32)
            s = jnp.einsum('bthd,btnhd->bthn', qb, kg) * scale
            s = jnp.where(valid[None, :, None, :], s, -1e30)
            m = jnp.max(s, -1, keepdims=True)
            pe = jnp.exp(s - m)
            den = jnp.sum(pe, -1)
            outs.append(jnp.einsum('bthn,btnhd->bthd', pe, vg) / den[..., None])
            lses.append(m[..., 0] + jnp.log(den))
        wts = jax.nn.softmax(jnp.stack(lses, 0), axis=0)
        return jnp.sum(wts[..., None] * jnp.stack(outs, 0), axis=0)

    out = lax.map(one_block, jnp.arange(nb))
    return out.transpose(1, 0, 2, 3, 4).reshape(B, T, H, hd)


def conv_ffn(x, conv_prev, w_up, conv_w, conv_b, w_down):
    a, u = jnp.split(x @ w_up, 2, axis=-1)
    T = a.shape[1]
    ext = jnp.concatenate([conv_prev.astype(a.dtype), a], axis=1)
    c = conv_b
    for j in range(CONV_W):
        c = c + conv_w[j] * ext[:, j:j + T]
    y = (jax.nn.gelu(c, approximate=False) * u) @ w_down
    return y, ext[:, T:]


def trunk_layer(x, pe, pos, s0, conv_prev, kv_cache, q_block,
                w_in, w_gk_b, b_gk, gla_norm, w_br_gla, w_br_dil, w_out, ln1_g, ln1_b,
                w_up, conv_w, conv_b, w_down, ln2_g, ln2_b, w_ple_gate, w_ple_proj, ln3_g, ln3_b):
    B, T, _ = x.shape
    h = x @ w_in
    gq, gk, gv, gg, g_lr, dq, dk, dv, gate_a, gate_b = jnp.split(h, SPLIT_AT, axis=-1)

    logd = jax.nn.log_sigmoid((g_lr @ w_gk_b + b_gk).astype(jnp.float32)) / GLA_NORMALIZER
    o, S = gla_recurrence(gq.reshape(B, T, GLA_HEADS, GLA_DK) * (GLA_DK ** -0.5),
                          gk.reshape(B, T, GLA_HEADS, GLA_DK),
                          gv.reshape(B, T, GLA_HEADS, GLA_DV),
                          logd.reshape(B, T, GLA_HEADS, GLA_DK), s0)
    o = o * lax.rsqrt(jnp.mean(jnp.square(o), -1, keepdims=True) + NORM_EPS) * gla_norm.astype(jnp.float32)
    o = o * jax.nn.silu(gg.reshape(B, T, GLA_HEADS, GLA_DV).astype(jnp.float32))
    ya = o.reshape(B, T, GLA_V).astype(x.dtype) @ w_br_gla

    dq = dq.reshape(B, T, N_GROUPS, DIL_HEADS, DIL_HD)
    dk = dk.reshape(B, T, N_GROUPS, DIL_HEADS, DIL_HD)
    dv = dv.reshape(B, T, N_GROUPS, DIL_HEADS, DIL_HD)
    qs = [rope(dq[:, :, g], pos) for g in range(N_GROUPS)]
    ks = [rope(dk[:, :, g], pos) for g in range(N_GROUPS)]
    vs = [dv[:, :, g] for g in range(N_GROUPS)]
    if kv_cache is None:
        keys, vals, offs = ks, vs, [0] * N_GROUPS
        new_kv = [jnp.stack([k, v], 2)[:, -min(w, T):] for (w, _), k, v in zip(DIL_GROUPS, ks, vs)]
    else:
        keys = [jnp.concatenate([c[:, :, 0].astype(k.dtype), k], 1) for c, k in zip(kv_cache, ks)]
        vals = [jnp.concatenate([c[:, :, 1].astype(v.dtype), v], 1) for c, v in zip(kv_cache, vs)]
        offs = [c.shape[1] for c in kv_cache]
        new_kv = [jnp.stack([k, v], 2) for k, v in zip(ks, vs)]
    yb = dilated_attention(qs, keys, vals, offs, q_block)
    yb = yb.reshape(B, T, DIL_WIDTH).astype(x.dtype) @ w_br_dil

    m = jax.nn.sigmoid(gate_a) * ya + jax.nn.sigmoid(gate_b) * yb
    x = layer_norm(ALPHA * x + m @ w_out, ln1_g, ln1_b)

    f, conv_new = conv_ffn(x, conv_prev, w_up, conv_w, conv_b, w_down)
    x = layer_norm(ALPHA * x + f, ln2_g, ln2_b)

    x = layer_norm(ALPHA * x + jax.nn.sigmoid(x @ w_ple_gate) * (pe @ w_ple_proj), ln3_g, ln3_b)
    return x, S.astype(s0.dtype), conv_new, new_kv


def setup_inputs(seed: int = 0) -> dict:
    key = jax.random.key(seed)
    ks = jax.random.split(key, 28)
    f32 = jnp.float32

    def nrm(k, shape, scale=1.0):
        return jax.random.normal(k, shape, f32) * scale

    n_kv = [min(w, PAST_LEN) for w, _ in DIL_GROUPS]
    return {
        'x_prompt': nrm(ks[0], (BATCH, SEQ, D_MODEL)),
        'x_sample': nrm(ks[1], (DEC_BATCH, DEC_SEQ, D_MODEL)),
        'p_prompt': nrm(ks[2], (DEPTH, BATCH, SEQ, PLE_DIM)),
        'p_sample': nrm(ks[3], (DEPTH, DEC_BATCH, DEC_SEQ, PLE_DIM)),
        'state_gla': nrm(ks[4], (DEPTH, DEC_BATCH, GLA_HEADS, GLA_DK, GLA_DV), 0.5),
        'cache_conv': nrm(ks[5], (DEPTH, DEC_BATCH, CONV_W - 1, D_FF)),
        'cache_kv_w128': nrm(ks[6], (DEPTH, DEC_BATCH, n_kv[0], 2, DIL_HEADS, DIL_HD)),
        'cache_kv_w512': nrm(ks[7], (DEPTH, DEC_BATCH, n_kv[1], 2, DIL_HEADS, DIL_HD)),
        'cache_kv_w2048': nrm(ks[8], (DEPTH, DEC_BATCH, n_kv[2], 2, DIL_HEADS, DIL_HD)),
        'w_in': nrm(ks[9], (DEPTH, D_MODEL, IN_COLS), D_MODEL ** -0.5),
        'w_gk_b': nrm(ks[10], (DEPTH, GLA_RANK, GLA_QK), GLA_RANK ** -0.5),
        'b_gk': nrm(ks[11], (DEPTH, GLA_QK), 0.1),
        'gla_norm': 1.0 + nrm(ks[12], (DEPTH, GLA_DV), 0.02),
        'w_br_gla': nrm(ks[13], (DEPTH, GLA_V, D_MODEL), GLA_V ** -0.5),
        'w_br_dil': nrm(ks[14], (DEPTH, DIL_WIDTH, D_MODEL), DIL_WIDTH ** -0.5),
        'w_out': nrm(ks[15], (DEPTH, D_MODEL, D_MODEL), BETA * D_MODEL ** -0.5),
        'ln1_g': 1.0 + nrm(ks[16], (DEPTH, D_MODEL), 0.02),
        'ln1_b': nrm(ks[17], (DEPTH, D_MODEL), 0.02),
        'w_up': nrm(ks[18], (DEPTH, D_MODEL, 2 * D_FF), D_MODEL ** -0.5),
        'conv_w': nrm(ks[19], (DEPTH, CONV_W, D_FF), CONV_W ** -0.5),
        'conv_b': nrm(ks[20], (DEPTH, D_FF), 0.02),
        'w_down': nrm(ks[21], (DEPTH, D_FF, D_MODEL), BETA * D_FF ** -0.5),
        'ln2_g': 1.0 + nrm(ks[22], (DEPTH, D_MODEL), 0.02),
        'ln2_b': nrm(ks[23], (DEPTH, D_MODEL), 0.02),
        'w_ple_gate': nrm(ks[24], (DEPTH, D_MODEL, D_MODEL), D_MODEL ** -0.5),
        'w_ple_proj': nrm(ks[25], (DEPTH, PLE_DIM, D_MODEL), BETA * PLE_DIM ** -0.5),
        'ln3_g': 1.0 + nrm(ks[26], (DEPTH, D_MODEL), 0.02),
        'ln3_b': nrm(ks[27], (DEPTH, D_MODEL), 0.02),
    }


def reference(x_prompt, x_sample, p_prompt, p_sample, state_gla, cache_conv,
              cache_kv_w128, cache_kv_w512, cache_kv_w2048,
              w_in, w_gk_b, b_gk, gla_norm, w_br_gla, w_br_dil, w_out, ln1_g, ln1_b,
              w_up, conv_w, conv_b, w_down, ln2_g, ln2_b, w_ple_gate, w_ple_proj, ln3_g, ln3_b):
    Bp, Tp, _ = x_prompt.shape
    Ts = x_sample.shape[1]
    pos_p = jnp.arange(Tp, dtype=jnp.int32)
    pos_s = PAST_LEN + jnp.arange(Ts, dtype=jnp.int32)
    y_prompt, y_sample = x_prompt, x_sample
    gla_p, gla_s, conv_p, conv_s = [], [], [], []
    kv_p = [[] for _ in range(N_GROUPS)]
    kv_s = [[] for _ in range(N_GROUPS)]
    for i in range(DEPTH):
        lw = (w_in[i], w_gk_b[i], b_gk[i], gla_norm[i], w_br_gla[i], w_br_dil[i], w_out[i],
              ln1_g[i], ln1_b[i], w_up[i], conv_w[i], conv_b[i], w_down[i], ln2_g[i], ln2_b[i],
              w_ple_gate[i], w_ple_proj[i], ln3_g[i], ln3_b[i])
        s0 = jnp.zeros((Bp, GLA_HEADS, GLA_DK, GLA_DV), state_gla.dtype)
        c0 = jnp.zeros((Bp, CONV_W - 1, D_FF), x_prompt.dtype)
        y_prompt, sp, cp, kvp = trunk_layer(y_prompt, p_prompt[i], pos_p, s0, c0, None, Q_BLOCK, *lw)
        y_sample, ss, cs, kvs = trunk_layer(
            y_sample, p_sample[i], pos_s, state_gla[i], cache_conv[i],
            (cache_kv_w128[i], cache_kv_w512[i], cache_kv_w2048[i]), 1, *lw)
        gla_p.append(sp)
        gla_s.append(ss)
        conv_p.append(cp)
        conv_s.append(cs)
        for g in range(N_GROUPS):
            kv_p[g].append(kvp[g])
            kv_s[g].append(kvs[g])
    return (y_prompt, y_sample, jnp.stack(gla_p), jnp.stack(gla_s), jnp.stack(conv_p), jnp.stack(conv_s),
            jnp.stack(kv_p[0]), jnp.stack(kv_p[1]), jnp.stack(kv_p[2]),
            jnp.stack(kv_s[0]), jnp.stack(kv_s[1]), jnp.stack(kv_s[2]))
```

```python
import functools

import jax
import jax.numpy as jnp
from jax import lax
from jax.experimental import pallas as pl
from jax.experimental.pallas import tpu as pltpu

F32 = jnp.float32
BF16 = jnp.bfloat16

D_MODEL = 1024
DEPTH = 1
PAST_LEN = 8192
GLA_HEADS = 4
GLA_DK = 128
GLA_DV = 256
GLA_QK = GLA_HEADS * GLA_DK
GLA_V = GLA_HEADS * GLA_DV
GLA_RANK = 16
GLA_NORMALIZER = 16.0
DIL_GROUPS = ((128, 1), (512, 4), (2048, 16))
N_GROUPS = len(DIL_GROUPS)
DIL_HEADS = 8
DIL_HD = 64
DIL_WIDTH = DIL_HEADS * DIL_HD
DIL_QKV = N_GROUPS * DIL_WIDTH
ROPE_THETA = 10000.0
D_FF = 2816
CONV_W = 3
PLE_DIM = 256
ALPHA = (2.0 * DEPTH) ** 0.25
NORM_EPS = 1e-5
COL_GLA = 2 * GLA_QK + 2 * GLA_V
COL_LR = COL_GLA + GLA_RANK
COL_DIL = COL_LR + 3 * DIL_QKV
NEG_FILL = -1e30

LANES = 128
SUBLANES = 8
VMEM_BYTES_V7X = 64 * 1024 * 1024
VMEM_LIMIT_CAP = 60000 * 1024

ATT_BLK = 128
GLA_CHUNK = 128
GLA_SUB = 16
FF_CHUNK = D_FF // 2


def _vmem_limit(nbytes):
    return int(min(max(2 * nbytes, 16 * 1024 * 1024), VMEM_LIMIT_CAP))


def _nbytes(shape, dtype):
    n = 1
    for s in shape:
        n *= s
    return n * jnp.dtype(dtype).itemsize


def _sigmoid(x):
    return 1.0 / (1.0 + jnp.exp(-x))


def _log_sigmoid(x):
    return jnp.minimum(x, 0.0) - jnp.log1p(jnp.exp(-jnp.abs(x)))


def _layer_norm(z, g, b):
    mu = jnp.mean(z, axis=-1, keepdims=True)
    d = z - mu
    var = jnp.mean(d * d, axis=-1, keepdims=True)
    return d * lax.rsqrt(var + NORM_EPS) * g + b


def _log2(n):
    assert n > 0 and n & (n - 1) == 0, n
    return n.bit_length() - 1


def _div(x, n):
    return x >> _log2(n)


def _mod(x, n):
    _log2(n)
    return x & (n - 1)


def _resident(shape):
    return pl.BlockSpec(shape, lambda *_: (0,) * len(shape), pipeline_mode=pl.Buffered(1))


def _dot(a, b):
    return jnp.dot(a, b, preferred_element_type=F32)


def _dot_nt(a, b):
    return lax.dot_general(a, b, (((1,), (1,)), ((), ())), preferred_element_type=F32)


def _gla_proj_body(x_ref, w_ref, wlr_ref, wgkb_ref, bgk_ref, q_ref, k_ref, v_ref, g_ref, ld_ref):
    xb = x_ref[...].astype(BF16)
    cw = GLA_QK

    def proj(lo):
        return _dot(xb, w_ref[:, lo:lo + cw])

    q_ref[...] = (proj(0) * GLA_DK ** -0.5).astype(q_ref.dtype)
    k_ref[...] = proj(GLA_QK).astype(k_ref.dtype)
    for c in range(GLA_V // cw):
        v_ref[:, c * cw:(c + 1) * cw] = proj(2 * GLA_QK + c * cw).astype(v_ref.dtype)
        gg = proj(2 * GLA_QK + GLA_V + c * cw)
        g_ref[:, c * cw:(c + 1) * cw] = (gg * _sigmoid(gg)).astype(g_ref.dtype)
    glr = _dot(xb, wlr_ref[...])
    z = _dot(glr.astype(BF16), wgkb_ref[...]) + bgk_ref[...]
    ld_ref[...] = _log_sigmoid(z) * (1.0 / GLA_NORMALIZER)


def _gla_proj(x2d, w_gla, w_lr, w_gkb, b_gk, *, tm, act_dtype):
    m = x2d.shape[0]
    row = lambda i: (i, 0)
    out_shape = [
        jax.ShapeDtypeStruct((m, GLA_QK), act_dtype), jax.ShapeDtypeStruct((m, GLA_QK), act_dtype),
        jax.ShapeDtypeStruct((m, GLA_V), act_dtype), jax.ShapeDtypeStruct((m, GLA_V), act_dtype),
        jax.ShapeDtypeStruct((m, GLA_QK), F32),
    ]
    weights = [w_gla, w_lr, w_gkb, b_gk]
    est = (2 * _nbytes((tm, D_MODEL), F32) + sum(_nbytes(w.shape, w.dtype) for w in weights)
           + 2 * sum(_nbytes((tm, s.shape[1]), s.dtype) for s in out_shape) + 2 * _nbytes((tm, GLA_QK), F32))
    return pl.pallas_call(
        _gla_proj_body,
        grid=(m // tm,),
        in_specs=[pl.BlockSpec((tm, D_MODEL), row)] + [_resident(w.shape) for w in weights],
        out_specs=[pl.BlockSpec((tm, s.shape[1]), row) for s in out_shape],
        out_shape=out_shape,
        compiler_params=pltpu.CompilerParams(dimension_semantics=("arbitrary",), vmem_limit_bytes=_vmem_limit(est)),
        name="gla_proj",
    )(x2d, w_gla, w_lr, w_gkb, b_gk)


def _dil_proj_body(x_ref, w_ref, inv_ref, *out_refs, tm, seq_len, pos_offset, keep_rows):
    qkv_refs, kv_refs = out_refs[:N_GROUPS], out_refs[N_GROUPS:]
    i = pl.program_id(0)
    xb = x_ref[...].astype(BF16)
    half = DIL_HD // 2
    row = lax.broadcasted_iota(jnp.int32, (tm, LANES), 0)
    lane = lax.broadcasted_iota(jnp.int32, (tm, LANES), 1)
    pos = pos_offset + _mod(i * tm + row, seq_len)
    ang = pos.astype(F32) * inv_ref[...]
    cos, sin = jnp.cos(ang), jnp.sin(ang)
    first_half = _mod(lane, DIL_HD) < half
    sin_signed = jnp.where(first_half, -sin, sin)

    def rope(xc):
        partner = jnp.where(first_half, pltpu.roll(xc, LANES - half, 1), pltpu.roll(xc, half, 1))
        return xc * cos + partner * sin_signed

    for g in range(N_GROUPS):
        keep = keep_rows[g]
        hq = _dot(xb, w_ref[:, g * DIL_WIDTH:(g + 1) * DIL_WIDTH])
        hk = _dot(xb, w_ref[:, DIL_QKV + g * DIL_WIDTH:DIL_QKV + (g + 1) * DIL_WIDTH])
        hv = _dot(xb, w_ref[:, 2 * DIL_QKV + g * DIL_WIDTH:2 * DIL_QKV + (g + 1) * DIL_WIDTH])
        for c in range(DIL_WIDTH // LANES):
            sl = slice(c * LANES, (c + 1) * LANES)
            qr = rope(hq[:, sl])
            kr = rope(hk[:, sl])
            qkv_refs[g][:, c * LANES:(c + 1) * LANES] = (qr * DIL_HD ** -0.5).astype(qkv_refs[g].dtype)
            qkv_refs[g][:, DIL_WIDTH + c * LANES:DIL_WIDTH + (c + 1) * LANES] = kr.astype(qkv_refs[g].dtype)
            kv_refs[g][:, c * LANES:(c + 1) * LANES] = kr[tm - keep:, :]
        qkv_refs[g][:, 2 * DIL_WIDTH:] = hv.astype(qkv_refs[g].dtype)
        kv_refs[g][:, DIL_WIDTH:] = hv[tm - keep:, :]


def _dil_proj(x2d, w_dil, inv_freq, *, tm, seq_len, pos_offset, act_dtype):
    m = x2d.shape[0]
    n_seq = m // seq_len
    tiles_per_seq = max(seq_len // tm, 1)
    row = lambda i: (i, 0)
    keep_rows, kv_shapes, kv_specs = [], [], []
    for w, _ in DIL_GROUPS:
        if seq_len <= tm:
            assert w >= seq_len
            keep, shape, imap = tm, (m, 2 * DIL_WIDTH), row
        else:
            keep = min(w, tm)
            nblk = w // keep
            shape = (n_seq * w, 2 * DIL_WIDTH)
            imap = functools.partial(
                lambda i, nblk: ((i // tiles_per_seq) * nblk
                                 + jnp.maximum(i % tiles_per_seq - (tiles_per_seq - nblk), 0), 0), nblk=nblk)
        keep_rows.append(keep)
        kv_shapes.append(jax.ShapeDtypeStruct(shape, F32))
        kv_specs.append(pl.BlockSpec((keep, 2 * DIL_WIDTH), imap))
    qkv_shapes = [jax.ShapeDtypeStruct((m, 3 * DIL_WIDTH), act_dtype)] * N_GROUPS
    est = (2 * _nbytes((tm, D_MODEL), F32) + _nbytes(w_dil.shape, BF16)
           + 2 * N_GROUPS * _nbytes((tm, 3 * DIL_WIDTH), act_dtype)
           + 2 * sum(_nbytes((k, 2 * DIL_WIDTH), F32) for k in keep_rows) + 3 * _nbytes((tm, DIL_WIDTH), F32))
    body = functools.partial(_dil_proj_body, tm=tm, seq_len=seq_len, pos_offset=pos_offset, keep_rows=tuple(keep_rows))
    outs = pl.pallas_call(
        body,
        grid=(m // tm,),
        in_specs=[pl.BlockSpec((tm, D_MODEL), row), _resident(w_dil.shape), _resident(inv_freq.shape)],
        out_specs=[pl.BlockSpec((tm, 3 * DIL_WIDTH), row)] * N_GROUPS + kv_specs,
        out_shape=qkv_shapes + kv_shapes,
        compiler_params=pltpu.CompilerParams(dimension_semantics=("arbitrary",), vmem_limit_bytes=_vmem_limit(est)),
        name="dil_proj",
    )(x2d, w_dil, inv_freq)
    return outs[:N_GROUPS], outs[N_GROUPS:]


def _cumsum_rows(x):
    n = x.shape[0]
    row = lax.broadcasted_iota(jnp.int32, x.shape, 0)
    s = 1
    while s < n:
        x = x + jnp.where(row >= s, pltpu.roll(x, s, 0), 0.0)
        s *= 2
    return x


def _pad_rows(x, n):
    if x.shape[0] == n:
        return x
    return jnp.concatenate([x, jnp.zeros((n - x.shape[0], x.shape[1]), x.dtype)], axis=0)


def _gla_scan_body(q_ref, k_ref, v_ref, g_ref, ld_ref, s0_ref, nw_ref, o_ref, sout_ref, st_ref, *, nseq, cc):
    chunk = pl.program_id(1)
    n_chunks = pl.num_programs(1)
    cp, sub = GLA_CHUNK, GLA_SUB
    nsub = cp // sub

    @pl.when(chunk == 0)
    def _():
        for s in range(nseq):
            for h in range(GLA_HEADS):
                st_ref[s * GLA_HEADS + h] = s0_ref[s, h].T

    row = lax.broadcasted_iota(jnp.int32, (cp, GLA_QK), 0)
    arow = lax.broadcasted_iota(jnp.int32, (cp, cp), 0)
    acol = lax.broadcasted_iota(jnp.int32, (cp, cp), 1)
    for s in range(nseq):
        q = _pad_rows(q_ref[s].astype(F32), cp)
        k = _pad_rows(k_ref[s].astype(F32), cp)
        b = _cumsum_rows(_pad_rows(ld_ref[s], cp))
        starts = [jnp.zeros((1, GLA_QK), F32)] + [b[i * sub - 1:i * sub, :] for i in range(1, nsub)]
        r = jnp.concatenate([jnp.broadcast_to(st, (sub, GLA_QK)) for st in starts], axis=0)
        qh = q * jnp.exp(b - r)
        qt = (qh * jnp.exp(r)).astype(BF16)
        b_last = b[cp - 1:cp, :]
        kl = (k * jnp.exp(b_last - b)).astype(BF16)
        a_rows = [[] for _ in range(GLA_HEADS)]
        for i in range(nsub):
            e = jnp.where(row < (i + 1) * sub, starts[i] - b, NEG_FILL)
            ki = (k * jnp.exp(e)).astype(BF16)
            qi = qh[i * sub:(i + 1) * sub, :].astype(BF16)
            for h in range(GLA_HEADS):
                hs = slice(h * GLA_DK, (h + 1) * GLA_DK)
                a_rows[h].append(_dot_nt(qi[:, hs], ki[:, hs]))
        d_last = jnp.exp(b_last)
        for h in range(GLA_HEADS):
            hs = slice(h * GLA_DK, (h + 1) * GLA_DK)
            vs = slice(h * GLA_DV, (h + 1) * GLA_DV)
            a = jnp.where(acol <= arow, jnp.concatenate(a_rows[h], axis=0), 0.0).astype(BF16)
            v = _pad_rows(v_ref[s][:, vs].astype(F32), cp)
            st = st_ref[s * GLA_HEADS + h]
            o = (_dot_nt(qt[:, hs], st.astype(BF16)) + _dot(a, v.astype(BF16)))[:cc]
            o = o * lax.rsqrt(jnp.mean(o * o, axis=-1, keepdims=True) + NORM_EPS) * nw_ref[...]
            o_ref[s, :, vs] = (o * g_ref[s][:, vs].astype(F32)).astype(o_ref.dtype)
            st_ref[s * GLA_HEADS + h] = d_last[:, hs] * st + _dot(v.T.astype(BF16), kl[:, hs])

    @pl.when(chunk == n_chunks - 1)
    def _():
        for s in range(nseq):
            for h in range(GLA_HEADS):
                sout_ref[s, h] = st_ref[s * GLA_HEADS + h].T


def _gla_scan(q, k, v, g, ld, s0, norm_w, *, nseq, cc):
    n_seq, t, _ = q.shape
    assert n_seq % nseq == 0 and t % cc == 0 and cc <= GLA_CHUNK
    blk = lambda sg, c: (sg, c, 0)
    sblk = lambda sg, c: (sg, 0, 0, 0)
    body = functools.partial(_gla_scan_body, nseq=nseq, cc=cc)
    est = (2 * nseq * cc * (2 * GLA_QK + 2 * GLA_V) * jnp.dtype(q.dtype).itemsize + 2 * nseq * cc * GLA_QK * 4
           + 2 * nseq * cc * GLA_V * jnp.dtype(q.dtype).itemsize + 5 * nseq * GLA_HEADS * GLA_DK * GLA_DV * 4)
    return pl.pallas_call(
        body,
        grid=(n_seq // nseq, t // cc),
        in_specs=[pl.BlockSpec((nseq, cc, GLA_QK), blk), pl.BlockSpec((nseq, cc, GLA_QK), blk),
                  pl.BlockSpec((nseq, cc, GLA_V), blk), pl.BlockSpec((nseq, cc, GLA_V), blk),
                  pl.BlockSpec((nseq, cc, GLA_QK), blk),
                  pl.BlockSpec((nseq, GLA_HEADS, GLA_DK, GLA_DV), sblk),
                  pl.BlockSpec(norm_w.shape, lambda sg, c: (0, 0))],
        out_specs=[pl.BlockSpec((nseq, cc, GLA_V), blk), pl.BlockSpec((nseq, GLA_HEADS, GLA_DK, GLA_DV), sblk)],
        out_shape=[jax.ShapeDtypeStruct((n_seq, t, GLA_V), q.dtype),
                   jax.ShapeDtypeStruct((n_seq, GLA_HEADS, GLA_DK, GLA_DV), F32)],
        scratch_shapes=[pltpu.VMEM((nseq * GLA_HEADS, GLA_DV, GLA_DK), F32)],
        compiler_params=pltpu.CompilerParams(dimension_semantics=("arbitrary", "arbitrary"),
                                             vmem_limit_bytes=_vmem_limit(est)),
        name="gla_scan",
    )(q, k, v, g, ld, s0, norm_w)


def _dil_attn_body(*refs, blocks_per_residue):
    ins, outs = refs[:5 * N_GROUPS], refs[5 * N_GROUPS:]
    n = pl.program_id(1)
    row = lax.broadcasted_iota(jnp.int32, (ATT_BLK, 2 * ATT_BLK), 0)
    col = lax.broadcasted_iota(jnp.int32, (ATT_BLK, 2 * ATT_BLK), 1)
    dist = row + ATT_BLK - col
    band = (dist >= 0) & (dist <= ATT_BLK)
    lane = lax.broadcasted_iota(jnp.int32, (ATT_BLK, LANES), 1)
    for g in range(N_GROUPS):
        q_ref, kp_ref, kc_ref, vp_ref, vc_ref = ins[5 * g:5 * g + 5]
        o_ref, lse_ref = outs[2 * g:2 * g + 2]
        jb = _mod(n, blocks_per_residue[g])
        valid = band & ((col >= ATT_BLK) | (jb > 0))
        q = q_ref[0]
        kk = jnp.concatenate([kp_ref[0], kc_ref[0]], axis=0)
        vv = jnp.concatenate([vp_ref[0], vc_ref[0]], axis=0)
        lse_tile = jnp.zeros((ATT_BLK, LANES), F32)
        for p in range(DIL_WIDTH // LANES):
            ls = slice(p * LANES, (p + 1) * LANES)
            q2, k2, v2 = q[:, ls], kk[:, ls], vv[:, ls]
            o_pair = jnp.zeros((ATT_BLK, LANES), F32)
            for e in range(LANES // DIL_HD):
                sel = (lane >= DIL_HD) if e else (lane < DIL_HD)
                s = _dot_nt(jnp.where(sel, q2, jnp.zeros_like(q2)), k2)
                s = jnp.where(valid, s, NEG_FILL)
                m = jnp.max(s, axis=-1, keepdims=True)
                pe = jnp.exp(s - m)
                den = jnp.sum(pe, axis=-1, keepdims=True)
                pv = _dot(pe.astype(BF16), v2)
                o_pair = jnp.where(sel, pv * (1.0 / den), o_pair)
                lse_tile = jnp.where(lane == p * (LANES // DIL_HD) + e, m + jnp.log(den), lse_tile)
            o_ref[0, :, ls] = o_pair.astype(o_ref.dtype)
        lse_ref[0] = lse_tile


def _dil_attn(qkv, *, n_seq, seq_len):
    args, in_specs, out_specs, out_shape, bpr = [], [], [], [], []
    n_blocks = seq_len // ATT_BLK
    for g, (w, r) in enumerate(DIL_GROUPS):
        assert w // r == ATT_BLK and seq_len % (r * ATT_BLK) == 0
        nb = n_blocks // r
        bpr.append(nb)
        view = qkv[g].reshape(n_seq, seq_len // r, r * 3 * DIL_WIDTH)
        cur = lambda b, n, nb, part: (b, n % nb, (n // nb) * 3 + part)
        prev = lambda b, n, nb, part: (b, jnp.maximum(n % nb - 1, 0), (n // nb) * 3 + part)
        blk = (1, ATT_BLK, DIL_WIDTH)
        for fn, part in ((cur, 0), (prev, 1), (cur, 1), (prev, 2), (cur, 2)):
            args.append(view)
            in_specs.append(pl.BlockSpec(blk, functools.partial(fn, nb=nb, part=part)))
        omap = functools.partial(lambda b, n, nb: (b, n % nb, n // nb), nb=nb)
        out_specs += [pl.BlockSpec(blk, omap), pl.BlockSpec((1, ATT_BLK, LANES), omap)]
        out_shape += [jax.ShapeDtypeStruct((n_seq, seq_len // r, r * DIL_WIDTH), qkv[g].dtype),
                      jax.ShapeDtypeStruct((n_seq, seq_len // r, r * LANES), F32)]
    est = 2 * N_GROUPS * (6 * _nbytes((ATT_BLK, DIL_WIDTH), qkv[0].dtype) + _nbytes((ATT_BLK, LANES), F32))
    outs = pl.pallas_call(
        functools.partial(_dil_attn_body, blocks_per_residue=tuple(bpr)),
        grid=(n_seq, n_blocks),
        in_specs=in_specs, out_specs=out_specs, out_shape=out_shape,
        compiler_params=pltpu.CompilerParams(dimension_semantics=("arbitrary", "arbitrary"),
                                             vmem_limit_bytes=_vmem_limit(est)),
        name="dil_attn",
    )(*args)
    m = n_seq * seq_len
    return ([outs[2 * g].reshape(m, DIL_WIDTH) for g in range(N_GROUPS)],
            [outs[2 * g + 1].reshape(m, LANES) for g in range(N_GROUPS)])


def _dec_attn_body(qkv0_ref, qkv1_ref, qkv2_ref, c0_ref, c1_ref, c2_ref, o_ref, *, t_new):
    qkv_refs = (qkv0_ref, qkv1_ref, qkv2_ref)
    c_refs = (c0_ref, c1_ref, c2_ref)
    nq = DIL_HEADS * t_new
    lane_w = lax.broadcasted_iota(jnp.int32, (nq, DIL_WIDTH), 1)
    row_w = lax.broadcasted_iota(jnp.int32, (nq, DIL_WIDTH), 0)
    own_head = _div(lane_w, DIL_HD) == _div(row_w, t_new)
    scores, values = [], []
    for g, (w, r) in enumerate(DIL_GROUPS):
        qkv = qkv_refs[g][0].astype(F32)
        q, k_new, v_new = qkv[:, :DIL_WIDTH], qkv[:, DIL_WIDTH:2 * DIL_WIDTH], qkv[:, 2 * DIL_WIDTH:]
        q_rep = jnp.broadcast_to(q[None], (DIL_HEADS, t_new, DIL_WIDTH)).reshape(nq, DIL_WIDTH)
        q_bd = jnp.where(own_head, q_rep, 0.0).astype(BF16)
        cache = c_refs[g][0]
        if cache.ndim == 3:
            cache = cache.reshape(cache.shape[0] * cache.shape[1], cache.shape[2])
        n_rows = cache.shape[0]
        k_c, v_c = cache[:, :DIL_WIDTH].astype(BF16), cache[:, DIL_WIDTH:].astype(BF16)
        t_q = _mod(lax.broadcasted_iota(jnp.int32, (nq, n_rows), 0), t_new)
        cidx = lax.broadcasted_iota(jnp.int32, (nq, n_rows), 1)
        j = cidx if n_rows == w else _div(cidx, t_new) * r + _mod(cidx, t_new)
        ok_c = (j >= t_q) & (_mod(w + t_q - j, r) == 0)
        scores.append(jnp.where(ok_c, _dot_nt(q_bd, k_c), NEG_FILL))
        values.append(v_c)
        t_n = _mod(lax.broadcasted_iota(jnp.int32, (nq, LANES), 0), t_new)
        u = lax.broadcasted_iota(jnp.int32, (nq, LANES), 1)
        ok_n = (u <= t_n) & (_mod(t_n - u, r) == 0)
        scores.append(jnp.where(ok_n, _dot_nt(q_bd, _pad_rows(k_new, LANES).astype(BF16)), NEG_FILL))
        values.append(_pad_rows(v_new, LANES).astype(BF16))
    m = functools.reduce(jnp.maximum, [jnp.max(s, axis=-1, keepdims=True) for s in scores])
    den = jnp.zeros((nq, 1), F32)
    acc = jnp.zeros((nq, DIL_WIDTH), F32)
    for s, v in zip(scores, values):
        pe = jnp.exp(s - m)
        den = den + jnp.sum(pe, axis=-1, keepdims=True)
        acc = acc + _dot(pe.astype(BF16), v)
    acc = jnp.where(own_head, acc * (1.0 / den), 0.0)
    o_ref[0] = jnp.sum(acc.reshape(DIL_HEADS, t_new, DIL_WIDTH), axis=0)


def _dec_attn(qkv, caches, *, n_seq, t_new):
    args, in_specs = [], []
    for g in range(N_GROUPS):
        args.append(qkv[g].reshape(n_seq, t_new, 3 * DIL_WIDTH))
        in_specs.append(pl.BlockSpec((1, t_new, 3 * DIL_WIDTH), lambda b: (b, 0, 0)))
    est = 0
    for g, (w, r) in enumerate(DIL_GROUPS):
        c = caches[g].reshape(n_seq, w, 2 * DIL_WIDTH)
        if r > t_new:
            assert t_new % SUBLANES == 0
            args.append(c.reshape(n_seq, w // r, r, 2 * DIL_WIDTH))
            in_specs.append(pl.BlockSpec((1, w // r, t_new, 2 * DIL_WIDTH), lambda b: (b, 0, 0, 0)))
            est += 2 * _nbytes((w // r, t_new, 2 * DIL_WIDTH), F32)
        else:
            args.append(c)
            in_specs.append(pl.BlockSpec((1, w, 2 * DIL_WIDTH), lambda b: (b, 0, 0)))
            est += 2 * _nbytes((w, 2 * DIL_WIDTH), F32)
    out = pl.pallas_call(
        functools.partial(_dec_attn_body, t_new=t_new),
        grid=(n_seq,),
        in_specs=in_specs,
        out_specs=pl.BlockSpec((1, t_new, DIL_WIDTH), lambda b: (b, 0, 0)),
        out_shape=jax.ShapeDtypeStruct((n_seq, t_new, DIL_WIDTH), F32),
        compiler_params=pltpu.CompilerParams(dimension_semantics=("arbitrary",), vmem_limit_bytes=_vmem_limit(2 * est)),
        name="dec_attn",
    )(*args)
    return out.reshape(n_seq * t_new, DIL_WIDTH)


def _merge_body(*refs, n_dil_inputs):
    x_ref, og_ref = refs[:2]
    dil_refs = refs[2:2 + n_dil_inputs]
    (wg_ref, wa_ref, wb_ref, wo_ref, ex_ref, g_ref, b_ref, y_ref) = refs[2 + n_dil_inputs:]
    x = x_ref[...]
    xb = x.astype(BF16)
    if n_dil_inputs == 1:
        yb_in = dil_refs[0][...].astype(F32)
    else:
        os_, lses = dil_refs[:N_GROUPS], [r[...] for r in dil_refs[N_GROUPS:]]
        top = functools.reduce(jnp.maximum, lses)
        es = [jnp.exp(l - top) for l in lses]
        inv = 1.0 / functools.reduce(jnp.add, es)
        yb_in = jnp.zeros(os_[0].shape, F32)
        for o_ref, e in zip(os_, es):
            wgt = e * inv
            hi = wgt.astype(BF16)
            lo = (wgt - hi.astype(F32)).astype(BF16)
            wide = _dot(hi, ex_ref[...]) + _dot(lo, ex_ref[...])
            yb_in = yb_in + wide * o_ref[...].astype(F32)
    ya = _dot(og_ref[...].astype(BF16), wa_ref[...])
    yb = _dot(yb_in.astype(BF16), wb_ref[...])
    ga = _dot(xb, wg_ref[:, :D_MODEL])
    gb = _dot(xb, wg_ref[:, D_MODEL:])
    mix = _sigmoid(ga) * ya + _sigmoid(gb) * yb
    z = ALPHA * x + _dot(mix.astype(BF16), wo_ref[...])
    y_ref[...] = _layer_norm(z, g_ref[...], b_ref[...])


def _merge(x2d, o_gla, dil_inputs, w_gates, w_br_gla, w_br_dil, w_out, expand, ln_g, ln_b, *, tm):
    m = x2d.shape[0]
    row = lambda i: (i, 0)
    acts = [x2d, o_gla] + list(dil_inputs)
    weights = [w_gates, w_br_gla, w_br_dil, w_out, expand, ln_g, ln_b]
    est = (2 * sum(_nbytes((tm, a.shape[1]), a.dtype) for a in acts) + sum(_nbytes(w.shape, w.dtype) for w in weights)
           + 6 * _nbytes((tm, D_MODEL), F32))
    return pl.pallas_call(
        functools.partial(_merge_body, n_dil_inputs=len(dil_inputs)),
        grid=(m // tm,),
        in_specs=[pl.BlockSpec((tm, a.shape[1]), row) for a in acts] + [_resident(w.shape) for w in weights],
        out_specs=pl.BlockSpec((tm, D_MODEL), row),
        out_shape=jax.ShapeDtypeStruct((m, D_MODEL), F32),
        compiler_params=pltpu.CompilerParams(dimension_semantics=("arbitrary",), vmem_limit_bytes=_vmem_limit(est)),
        name="merge",
    )(*acts, *weights)


def _ffn_body(*refs, tm, tiles_per_seq, seq_len, tail_rows):
    long_seq = seq_len >= tm
    if long_seq:
        x_ref, halo_ref, cp_ref, pe_ref = refs[:4]
    else:
        x_ref, cp0_ref, cp1_ref, pe_ref = refs[:4]
    (wup_ref, cw_ref, cb_ref, wdn_ref, g2_ref, b2_ref, wpg_ref, wpp_ref, g3_ref, b3_ref, y_ref, tail_ref) = refs[4:]
    i = pl.program_id(0)
    x = x_ref[...]
    xb = x.astype(BF16)
    row = lax.broadcasted_iota(jnp.int32, (tm, FF_CHUNK), 0)
    acc = jnp.zeros((tm, D_MODEL), F32)
    for c in range(D_FF // FF_CHUNK):
        cs = slice(c * FF_CHUNK, (c + 1) * FF_CHUNK)
        a = _dot(xb, wup_ref[:, cs])
        u = _dot(xb, wup_ref[:, D_FF + c * FF_CHUNK:D_FF + (c + 1) * FF_CHUNK])
        if long_seq:
            a_halo = _dot(halo_ref[...].astype(BF16), wup_ref[:, cs])
            prev = jnp.where(i % tiles_per_seq == 0, cp_ref[0][:, cs], a_halo)
            p1, p2 = prev[SUBLANES - 1:SUBLANES, :], prev[SUBLANES - 2:SUBLANES - 1, :]
            t = row
        else:
            p1, p2 = cp1_ref[:, cs], cp0_ref[:, cs]
            t = _mod(row, seq_len)
        am1 = jnp.where(t == 0, p1, pltpu.roll(a, 1, 0))
        am2 = jnp.where(t == 0, p2, jnp.where(t == 1, p1, pltpu.roll(a, 2, 0)))
        cw = cw_ref[:, cs]
        conv = cb_ref[:, cs] + cw[0:1, :] * am2 + cw[1:2, :] * am1 + cw[2:3, :] * a
        gelu = 0.5 * conv * (1.0 + lax.erf(conv * (2.0 ** -0.5)))
        acc = acc + _dot((gelu * u).astype(BF16), wdn_ref[cs, :])
        tail_ref[:, cs] = a[tm - tail_rows:, :]
    x2 = _layer_norm(ALPHA * x + acc, g2_ref[...], b2_ref[...])
    gate = _sigmoid(_dot(x2.astype(BF16), wpg_ref[...]))
    emb = _dot(pe_ref[...].astype(BF16), wpp_ref[...])
    y_ref[...] = _layer_norm(ALPHA * x2 + gate * emb, g3_ref[...], b3_ref[...])


def _ffn(x2d, conv_prev, pe2d, w_up, conv_w, conv_b, w_down, ln2_g, ln2_b, w_pg, w_pp, ln3_g, ln3_b, *, tm, seq_len):
    m = x2d.shape[0]
    n_seq = m // seq_len
    row = lambda i: (i, 0)
    n_tiles = m // tm
    if seq_len >= tm:
        tiles_per_seq, tail_rows = seq_len // tm, SUBLANES
        cp = jnp.concatenate([jnp.zeros((n_seq, SUBLANES - (CONV_W - 1), D_FF), F32), conv_prev], axis=1)
        acts = [x2d, x2d, cp, pe2d]
        act_specs = [pl.BlockSpec((tm, D_MODEL), row),
                     pl.BlockSpec((SUBLANES, D_MODEL), lambda i: (jnp.maximum(i * (tm // SUBLANES) - 1, 0), 0)),
                     pl.BlockSpec((1, SUBLANES, D_FF), lambda i: (i // tiles_per_seq, 0, 0)),
                     pl.BlockSpec((tm, PLE_DIM), row)]
        act_bytes = 2 * _nbytes((tm, D_MODEL), F32) + _nbytes((SUBLANES, D_FF), F32)
    else:
        assert tm % seq_len == 0 and seq_len >= CONV_W - 1
        tiles_per_seq, tail_rows = 1, tm
        cp0 = jnp.repeat(conv_prev[:, 0, :], seq_len, axis=0)
        cp1 = jnp.repeat(conv_prev[:, 1, :], seq_len, axis=0)
        acts = [x2d, cp0, cp1, pe2d]
        act_specs = [pl.BlockSpec((tm, D_MODEL), row), pl.BlockSpec((tm, D_FF), row), pl.BlockSpec((tm, D_FF), row),
                     pl.BlockSpec((tm, PLE_DIM), row)]
        act_bytes = _nbytes((tm, D_MODEL), F32) + 2 * _nbytes((tm, D_FF), F32)
    weights = [w_up, conv_w, conv_b, w_down, ln2_g, ln2_b, w_pg, w_pp, ln3_g, ln3_b]
    est = (2 * act_bytes + sum(_nbytes(w.shape, w.dtype) for w in weights) + 4 * _nbytes((tm, D_MODEL), F32)
           + 2 * _nbytes((tail_rows, D_FF), F32) + 6 * _nbytes((tm, FF_CHUNK), F32))
    y, tail = pl.pallas_call(
        functools.partial(_ffn_body, tm=tm, tiles_per_seq=tiles_per_seq, seq_len=seq_len, tail_rows=tail_rows),
        grid=(n_tiles,),
        in_specs=act_specs + [_resident(w.shape) for w in weights],
        out_specs=[pl.BlockSpec((tm, D_MODEL), row), pl.BlockSpec((tail_rows, D_FF), row)],
        out_shape=[jax.ShapeDtypeStruct((m, D_MODEL), F32), jax.ShapeDtypeStruct((n_tiles * tail_rows, D_FF), F32)],
        compiler_params=pltpu.CompilerParams(dimension_semantics=("arbitrary",), vmem_limit_bytes=_vmem_limit(est)),
        name="ffn",
    )(*acts, *weights)
    return y, tail


def _tile_rows(m):
    return min(m, 512)


def _layer(x, pe, s0, conv_prev, kv_cache, pos_offset, wts):
    n_seq, t, _ = x.shape
    m = n_seq * t
    x2d = x.reshape(m, D_MODEL)
    prompt = kv_cache is None
    act_dtype = BF16 if prompt else F32
    tm = _tile_rows(m)

    q, k, v, g, ld = _gla_proj(x2d, wts["w_gla"], wts["w_lr"], wts["w_gkb"], wts["b_gk"], tm=tm, act_dtype=act_dtype)
    qkv, kv_new = _dil_proj(x2d, wts["w_dil"], wts["inv_freq"], tm=tm, seq_len=t, pos_offset=pos_offset, act_dtype=act_dtype)

    r3 = lambda a: a.reshape(n_seq, t, a.shape[-1])
    cc = min(t, GLA_CHUNK)
    nseq = n_seq if prompt else 8
    o_gla, s_new = _gla_scan(r3(q), r3(k), r3(v), r3(g), r3(ld), s0, wts["gla_norm"], nseq=nseq, cc=cc)
    o_gla = o_gla.reshape(m, GLA_V)

    if prompt:
        o_dil, lse = _dil_attn(qkv, n_seq=n_seq, seq_len=t)
        dil_inputs = o_dil + lse
    else:
        dil_inputs = [_dec_attn(qkv, kv_cache, n_seq=n_seq, t_new=t)]

    x1 = _merge(x2d, o_gla, dil_inputs, wts["w_gates"], wts["w_br_gla"], wts["w_br_dil"], wts["w_out"], wts["expand"],
                wts["ln1_g"], wts["ln1_b"], tm=tm)
    tm_ffn = tm if prompt else min(tm, 256)
    y, tail = _ffn(x1, conv_prev, pe.reshape(m, PLE_DIM), wts["w_up"], wts["conv_w"], wts["conv_b"], wts["w_down"],
                   wts["ln2_g"], wts["ln2_b"], wts["w_ple_gate"], wts["w_ple_proj"], wts["ln3_g"], wts["ln3_b"],
                   tm=tm_ffn, seq_len=t)
    if t >= tm_ffn:
        tiles_per_seq = t // tm_ffn
        conv_new = tail.reshape(n_seq, tiles_per_seq, SUBLANES, D_FF)[:, -1, SUBLANES - (CONV_W - 1):, :]
    else:
        conv_new = tail.reshape(n_seq, t, D_FF)[:, t - (CONV_W - 1):, :]
    kv_out = [kvn.reshape(n_seq, -1, 2, DIL_HEADS, DIL_HD) for kvn in kv_new]
    return y.reshape(n_seq, t, D_MODEL), s_new, conv_new, kv_out


def _prep_weights(w_in, w_gk_b, b_gk, gla_norm, w_br_gla, w_br_dil, w_out, ln1_g, ln1_b, w_up, conv_w, conv_b, w_down,
                  ln2_g, ln2_b, w_ple_gate, w_ple_proj, ln3_g, ln3_b):
    w_in_b = w_in[0].astype(BF16)
    vec = lambda a: a[0].reshape(1, -1).astype(F32)
    half = DIL_HD // 2
    inv = ROPE_THETA ** (-jnp.arange(half, dtype=F32) / half)
    head_of_lane = jnp.arange(DIL_WIDTH) // DIL_HD
    expand = (jnp.arange(LANES)[:, None] == head_of_lane[None, :]).astype(BF16)
    return {
        "w_gla": w_in_b[:, :COL_GLA],
        "w_lr": jnp.pad(w_in_b[:, COL_GLA:COL_LR], ((0, 0), (0, LANES - GLA_RANK))),
        "w_gkb": jnp.pad(w_gk_b[0].astype(BF16), ((0, LANES - GLA_RANK), (0, 0))),
        "b_gk": vec(b_gk),
        "w_dil": w_in_b[:, COL_LR:COL_DIL],
        "w_gates": w_in_b[:, COL_DIL:],
        "inv_freq": jnp.tile(inv, LANES // half).reshape(1, LANES),
        "gla_norm": vec(gla_norm),
        "w_br_gla": w_br_gla[0].astype(BF16), "w_br_dil": w_br_dil[0].astype(BF16), "w_out": w_out[0].astype(BF16),
        "expand": expand,
        "ln1_g": vec(ln1_g), "ln1_b": vec(ln1_b),
        "w_up": w_up[0].astype(BF16),
        "conv_w": jnp.pad(conv_w[0].astype(F32), ((0, SUBLANES - CONV_W), (0, 0))),
        "conv_b": vec(conv_b),
        "w_down": w_down[0].astype(BF16),
        "ln2_g": vec(ln2_g), "ln2_b": vec(ln2_b),
        "w_ple_gate": w_ple_gate[0].astype(BF16), "w_ple_proj": w_ple_proj[0].astype(BF16),
        "ln3_g": vec(ln3_g), "ln3_b": vec(ln3_b),
    }


def kernel(x_prompt, x_sample, p_prompt, p_sample, state_gla, cache_conv, cache_kv_w128, cache_kv_w512, cache_kv_w2048, w_in, w_gk_b, b_gk, gla_norm, w_br_gla, w_br_dil, w_out, ln1_g, ln1_b, w_up, conv_w, conv_b, w_down, ln2_g, ln2_b, w_ple_gate, w_ple_proj, ln3_g, ln3_b):
    assert w_in.shape[0] == DEPTH == 1
    wts = _prep_weights(w_in, w_gk_b, b_gk, gla_norm, w_br_gla, w_br_dil, w_out, ln1_g, ln1_b, w_up, conv_w, conv_b,
                        w_down, ln2_g, ln2_b, w_ple_gate, w_ple_proj, ln3_g, ln3_b)
    n_p = x_prompt.shape[0]
    s0_p = jnp.zeros((n_p, GLA_HEADS, GLA_DK, GLA_DV), state_gla.dtype)
    c0_p = jnp.zeros((n_p, CONV_W - 1, D_FF), x_prompt.dtype)
    y_p, s_p, conv_p, kv_p = _layer(x_prompt, p_prompt[0], s0_p, c0_p, None, 0, wts)
    caches = (cache_kv_w128[0], cache_kv_w512[0], cache_kv_w2048[0])
    y_s, s_s, conv_s, kv_s = _layer(x_sample, p_sample[0], state_gla[0], cache_conv[0], caches, PAST_LEN, wts)
    return (y_p, y_s, s_p[None], s_s[None], conv_p[None], conv_s[None],
            kv_p[0][None], kv_p[1][None], kv_p[2][None], kv_s[0][None], kv_s[1][None], kv_s[2][None])
```

```python
import functools

import jax
import jax.numpy as jnp
from jax import lax
from jax.experimental import pallas as pl
from jax.experimental.pallas import tpu as pltpu

F32 = jnp.float32
BF16 = jnp.bfloat16

D_MODEL = 1024
DEPTH = 1
PAST_LEN = 8192
GLA_HEADS = 4
GLA_DK = 128
GLA_DV = 256
GLA_QK = GLA_HEADS * GLA_DK
GLA_V = GLA_HEADS * GLA_DV
GLA_RANK = 16
GLA_NORMALIZER = 16.0
DIL_GROUPS = ((128, 1), (512, 4), (2048, 16))
N_GROUPS = len(DIL_GROUPS)
DIL_HEADS = 8
DIL_HD = 64
DIL_WIDTH = DIL_HEADS * DIL_HD
DIL_QKV = N_GROUPS * DIL_WIDTH
ROPE_THETA = 10000.0
D_FF = 2816
CONV_W = 3
PLE_DIM = 256
ALPHA = (2.0 * DEPTH) ** 0.25
NORM_EPS = 1e-5
COL_GLA = 2 * GLA_QK + 2 * GLA_V
COL_LR = COL_GLA + GLA_RANK
COL_DIL = COL_LR + 3 * DIL_QKV
NEG_FILL = -1e30

LANES = 128
SUBLANES = 8
VMEM_BYTES_V7X = 64 * 1024 * 1024
VMEM_LIMIT_CAP = 60000 * 1024

ATT_BLK = 128
GLA_CHUNK = 128
GLA_SUB = 16
FF_CHUNK = D_FF // 2


def _vmem_limit(nbytes):
    return int(min(max(2 * nbytes, 16 * 1024 * 1024), VMEM_LIMIT_CAP))


def _nbytes(shape, dtype):
    n = 1
    for s in shape:
        n *= s
    return n * jnp.dtype(dtype).itemsize


def _sigmoid(x):
    return 1.0 / (1.0 + jnp.exp(-x))


def _log_sigmoid(x):
    return jnp.minimum(x, 0.0) - jnp.log1p(jnp.exp(-jnp.abs(x)))


def _layer_norm(z, g, b):
    mu = jnp.mean(z, axis=-1, keepdims=True)
    d = z - mu
    var = jnp.mean(d * d, axis=-1, keepdims=True)
    return d * lax.rsqrt(var + NORM_EPS) * g + b


def _log2(n):
    assert n > 0 and n & (n - 1) == 0, n
    return n.bit_length() - 1


def _div(x, n):
    return x >> _log2(n)


def _mod(x, n):
    _log2(n)
    return x & (n - 1)


def _resident(shape):
    return pl.BlockSpec(shape, lambda *_: (0,) * len(shape), pipeline_mode=pl.Buffered(1))


def _dot(a, b):
    return jnp.dot(a, b, preferred_element_type=F32)


def _dot_nt(a, b):
    return lax.dot_general(a, b, (((1,), (1,)), ((), ())), preferred_element_type=F32)


def _gla_proj_body(x_ref, w_ref, wlr_ref, wgkb_ref, bgk_ref, q_ref, k_ref, v_ref, g_ref, ld_ref):
    xb = x_ref[...].astype(BF16)
    cw = GLA_QK

    def proj(lo):
        return _dot(xb, w_ref[:, lo:lo + cw])

    q_ref[...] = (proj(0) * GLA_DK ** -0.5).astype(q_ref.dtype)
    k_ref[...] = proj(GLA_QK).astype(k_ref.dtype)
    for c in range(GLA_V // cw):
        v_ref[:, c * cw:(c + 1) * cw] = proj(2 * GLA_QK + c * cw).astype(v_ref.dtype)
        gg = proj(2 * GLA_QK + GLA_V + c * cw)
        g_ref[:, c * cw:(c + 1) * cw] = (gg * _sigmoid(gg)).astype(g_ref.dtype)
    glr = _dot(xb, wlr_ref[...])
    z = _dot(glr.astype(BF16), wgkb_ref[...]) + bgk_ref[...]
    ld_ref[...] = _log_sigmoid(z) * (1.0 / GLA_NORMALIZER)


def _gla_proj(x2d, w_gla, w_lr, w_gkb, b_gk, *, tm, act_dtype):
    m = x2d.shape[0]
    row = lambda i: (i, 0)
    out_shape = [
        jax.ShapeDtypeStruct((m, GLA_QK), act_dtype), jax.ShapeDtypeStruct((m, GLA_QK), act_dtype),
        jax.ShapeDtypeStruct((m, GLA_V), act_dtype), jax.ShapeDtypeStruct((m, GLA_V), act_dtype),
        jax.ShapeDtypeStruct((m, GLA_QK), F32),
    ]
    weights = [w_gla, w_lr, w_gkb, b_gk]
    est = (2 * _nbytes((tm, D_MODEL), F32) + sum(_nbytes(w.shape, w.dtype) for w in weights)
           + 2 * sum(_nbytes((tm, s.shape[1]), s.dtype) for s in out_shape) + 2 * _nbytes((tm, GLA_QK), F32))
    return pl.pallas_call(
        _gla_proj_body,
        grid=(m // tm,),
        in_specs=[pl.BlockSpec((tm, D_MODEL), row)] + [_resident(w.shape) for w in weights],
        out_specs=[pl.BlockSpec((tm, s.shape[1]), row) for s in out_shape],
        out_shape=out_shape,
        compiler_params=pltpu.CompilerParams(dimension_semantics=("arbitrary",), vmem_limit_bytes=_vmem_limit(est)),
        name="gla_proj",
    )(x2d, w_gla, w_lr, w_gkb, b_gk)


def _dil_proj_body(x_ref, w_ref, inv_ref, *refs, tm, seq_len, pos_offset, regroup):
    qkv_refs, kv_refs = refs[:N_GROUPS], refs[N_GROUPS:2 * N_GROUPS]
    stage_ref = refs[2 * N_GROUPS] if regroup else None
    i = pl.program_id(0)
    xb = x_ref[...].astype(BF16)
    half = DIL_HD // 2
    row = lax.broadcasted_iota(jnp.int32, (tm, LANES), 0)
    lane = lax.broadcasted_iota(jnp.int32, (tm, LANES), 1)
    pos = pos_offset + _mod(i * tm + row, seq_len)
    ang = pos.astype(F32) * inv_ref[...]
    cos, sin = jnp.cos(ang), jnp.sin(ang)
    first_half = _mod(lane, DIL_HD) < half
    sin_signed = jnp.where(first_half, -sin, sin)

    def rope(xc):
        partner = jnp.where(first_half, pltpu.roll(xc, LANES - half, 1), pltpu.roll(xc, half, 1))
        return xc * cos + partner * sin_signed

    tiles_per_seq = max(seq_len // tm, 1)
    it = _mod(i, tiles_per_seq)
    n_slab = DIL_WIDTH // LANES
    for g, (w, r) in enumerate(DIL_GROUPS):
        hq = _dot(xb, w_ref[:, g * DIL_WIDTH:(g + 1) * DIL_WIDTH])
        hk = _dot(xb, w_ref[:, DIL_QKV + g * DIL_WIDTH:DIL_QKV + (g + 1) * DIL_WIDTH])
        hv = _dot(xb, w_ref[:, 2 * DIL_QKV + g * DIL_WIDTH:2 * DIL_QKV + (g + 1) * DIL_WIDTH])
        slabs = []
        for part, h in enumerate((hq, hk, hv)):
            for c in range(n_slab):
                xc = h[:, c * LANES:(c + 1) * LANES]
                slabs.append(xc if part == 2 else rope(xc) * (DIL_HD ** -0.5 if part == 0 else 1.0))
        if not regroup:
            for j, xc in enumerate(slabs):
                qkv_refs[g][:, j * LANES:(j + 1) * LANES] = xc.astype(qkv_refs[g].dtype)
                if j >= n_slab:
                    kv_refs[g][:, (j - n_slab) * LANES:(j - n_slab + 1) * LANES] = xc
            continue
        for j, xc in enumerate(slabs):
            if r == 1:
                qkv_refs[g][0, 0, :, j * LANES:(j + 1) * LANES] = xc.astype(qkv_refs[g].dtype)
            else:
                stage_ref[g - 1, j] = xc
        for c in range(r if r > 1 else 0):
            for j in range(3 * n_slab):
                qkv_refs[g][0, c, :, j * LANES:(j + 1) * LANES] = (
                    stage_ref[g - 1, j, pl.ds(c, tm // r, stride=r), :].astype(qkv_refs[g].dtype))
        keep = min(w, tm)

        @pl.when(it >= tiles_per_seq - w // keep)
        def _(g=g, keep=keep, slabs=slabs):
            for j in range(2 * n_slab):
                kv_refs[g][0, j // n_slab, (j % n_slab) * LANES:(j % n_slab + 1) * LANES, :] = (
                    slabs[n_slab + j][tm - keep:, :].T)


def _dil_proj(x2d, w_dil, inv_freq, *, tm, seq_len, pos_offset, act_dtype):
    m = x2d.shape[0]
    n_seq = m // seq_len
    regroup = seq_len >= tm
    row = lambda i: (i, 0)
    tiles_per_seq = max(seq_len // tm, 1)
    qkv_shapes, qkv_specs, kv_shapes, kv_specs, scratch = [], [], [], [], []
    for w, r in DIL_GROUPS:
        if regroup:
            assert tm % r == 0 and seq_len % tm == 0
            keep = min(w, tm)
            nblk = w // keep
            qkv_shapes.append(jax.ShapeDtypeStruct((n_seq, r, seq_len // r, 3 * DIL_WIDTH), act_dtype))
            qkv_specs.append(pl.BlockSpec((1, r, tm // r, 3 * DIL_WIDTH),
                                          lambda i: (i // tiles_per_seq, 0, i % tiles_per_seq, 0)))
            kv_shapes.append(jax.ShapeDtypeStruct((n_seq, 2, DIL_WIDTH, w), F32))
            kv_specs.append(pl.BlockSpec(
                (1, 2, DIL_WIDTH, keep),
                functools.partial(lambda i, nblk: (i // tiles_per_seq, 0, 0,
                                                   jnp.maximum(i % tiles_per_seq - (tiles_per_seq - nblk), 0)), nblk=nblk)))
        else:
            assert w >= seq_len
            qkv_shapes.append(jax.ShapeDtypeStruct((m, 3 * DIL_WIDTH), act_dtype))
            qkv_specs.append(pl.BlockSpec((tm, 3 * DIL_WIDTH), row))
            kv_shapes.append(jax.ShapeDtypeStruct((m, 2 * DIL_WIDTH), F32))
            kv_specs.append(pl.BlockSpec((tm, 2 * DIL_WIDTH), row))
    est = (2 * _nbytes((tm, D_MODEL), F32) + _nbytes(w_dil.shape, BF16)
           + 2 * N_GROUPS * _nbytes((tm, 3 * DIL_WIDTH), act_dtype)
           + 2 * sum(_nbytes(s.block_shape, F32) for s in kv_specs) + 3 * _nbytes((tm, DIL_WIDTH), F32))
    if regroup:
        assert DIL_GROUPS[0][1] == 1 and all(r > 1 for _, r in DIL_GROUPS[1:])
        scratch.append(pltpu.VMEM((N_GROUPS - 1, 3 * DIL_WIDTH // LANES, tm, LANES), F32))
        est += _nbytes((N_GROUPS - 1, tm, 3 * DIL_WIDTH), F32)
    body = functools.partial(_dil_proj_body, tm=tm, seq_len=seq_len, pos_offset=pos_offset, regroup=regroup)
    outs = pl.pallas_call(
        body,
        grid=(m // tm,),
        in_specs=[pl.BlockSpec((tm, D_MODEL), row), _resident(w_dil.shape), _resident(inv_freq.shape)],
        out_specs=qkv_specs + kv_specs,
        out_shape=qkv_shapes + kv_shapes,
        scratch_shapes=scratch,
        compiler_params=pltpu.CompilerParams(dimension_semantics=("arbitrary",), vmem_limit_bytes=_vmem_limit(est)),
        name="dil_proj",
    )(x2d, w_dil, inv_freq)
    return outs[:N_GROUPS], outs[N_GROUPS:]


def _cumsum_rows(x):
    n = x.shape[0]
    row = lax.broadcasted_iota(jnp.int32, x.shape, 0)
    s = 1
    while s < n:
        x = x + jnp.where(row >= s, pltpu.roll(x, s, 0), 0.0)
        s *= 2
    return x


def _pad_rows(x, n):
    if x.shape[0] == n:
        return x
    return jnp.concatenate([x, jnp.zeros((n - x.shape[0], x.shape[1]), x.dtype)], axis=0)


def _gla_scan_body(q_ref, k_ref, v_ref, g_ref, ld_ref, s0_ref, nw_ref, o_ref, sout_ref, st_ref, *, nseq, cc):
    chunk = pl.program_id(1)
    n_chunks = pl.num_programs(1)
    cp, sub = GLA_CHUNK, GLA_SUB
    nsub = cp // sub

    @pl.when(chunk == 0)
    def _():
        for s in range(nseq):
            for h in range(GLA_HEADS):
                st_ref[s * GLA_HEADS + h] = s0_ref[s, h].T

    row = lax.broadcasted_iota(jnp.int32, (cp, GLA_QK), 0)
    arow = lax.broadcasted_iota(jnp.int32, (cp, cp), 0)
    acol = lax.broadcasted_iota(jnp.int32, (cp, cp), 1)
    for s in range(nseq):
        q = _pad_rows(q_ref[s].astype(F32), cp)
        k = _pad_rows(k_ref[s].astype(F32), cp)
        b = _cumsum_rows(_pad_rows(ld_ref[s], cp))
        starts = [jnp.zeros((1, GLA_QK), F32)] + [b[i * sub - 1:i * sub, :] for i in range(1, nsub)]
        r = jnp.concatenate([jnp.broadcast_to(st, (sub, GLA_QK)) for st in starts], axis=0)
        qh = q * jnp.exp(b - r)
        qt = (qh * jnp.exp(r)).astype(BF16)
        b_last = b[cp - 1:cp, :]
        kl = (k * jnp.exp(b_last - b)).astype(BF16)
        a_rows = [[] for _ in range(GLA_HEADS)]
        for i in range(nsub):
            e = jnp.where(row < (i + 1) * sub, starts[i] - b, NEG_FILL)
            ki = (k * jnp.exp(e)).astype(BF16)
            qi = qh[i * sub:(i + 1) * sub, :].astype(BF16)
            for h in range(GLA_HEADS):
                hs = slice(h * GLA_DK, (h + 1) * GLA_DK)
                a_rows[h].append(_dot_nt(qi[:, hs], ki[:, hs]))
        d_last = jnp.exp(b_last)
        for h in range(GLA_HEADS):
            hs = slice(h * GLA_DK, (h + 1) * GLA_DK)
            vs = slice(h * GLA_DV, (h + 1) * GLA_DV)
            a = jnp.where(acol <= arow, jnp.concatenate(a_rows[h], axis=0), 0.0).astype(BF16)
            v = _pad_rows(v_ref[s][:, vs].astype(F32), cp)
            st = st_ref[s * GLA_HEADS + h]
            o = (_dot_nt(qt[:, hs], st.astype(BF16)) + _dot(a, v.astype(BF16)))[:cc]
            o = o * lax.rsqrt(jnp.mean(o * o, axis=-1, keepdims=True) + NORM_EPS) * nw_ref[...]
            o_ref[s, :, vs] = (o * g_ref[s][:, vs].astype(F32)).astype(o_ref.dtype)
            st_ref[s * GLA_HEADS + h] = d_last[:, hs] * st + _dot(v.T.astype(BF16), kl[:, hs])

    @pl.when(chunk == n_chunks - 1)
    def _():
        for s in range(nseq):
            for h in range(GLA_HEADS):
                sout_ref[s, h] = st_ref[s * GLA_HEADS + h].T


def _gla_scan(q, k, v, g, ld, s0, norm_w, *, nseq, cc):
    n_seq, t, _ = q.shape
    assert n_seq % nseq == 0 and t % cc == 0 and cc <= GLA_CHUNK
    blk = lambda sg, c: (sg, c, 0)
    sblk = lambda sg, c: (sg, 0, 0, 0)
    body = functools.partial(_gla_scan_body, nseq=nseq, cc=cc)
    est = (2 * nseq * cc * (2 * GLA_QK + 2 * GLA_V) * jnp.dtype(q.dtype).itemsize + 2 * nseq * cc * GLA_QK * 4
           + 2 * nseq * cc * GLA_V * jnp.dtype(q.dtype).itemsize + 5 * nseq * GLA_HEADS * GLA_DK * GLA_DV * 4)
    return pl.pallas_call(
        body,
        grid=(n_seq // nseq, t // cc),
        in_specs=[pl.BlockSpec((nseq, cc, GLA_QK), blk), pl.BlockSpec((nseq, cc, GLA_QK), blk),
                  pl.BlockSpec((nseq, cc, GLA_V), blk), pl.BlockSpec((nseq, cc, GLA_V), blk),
                  pl.BlockSpec((nseq, cc, GLA_QK), blk),
                  pl.BlockSpec((nseq, GLA_HEADS, GLA_DK, GLA_DV), sblk),
                  pl.BlockSpec(norm_w.shape, lambda sg, c: (0, 0))],
        out_specs=[pl.BlockSpec((nseq, cc, GLA_V), blk), pl.BlockSpec((nseq, GLA_HEADS, GLA_DK, GLA_DV), sblk)],
        out_shape=[jax.ShapeDtypeStruct((n_seq, t, GLA_V), q.dtype),
                   jax.ShapeDtypeStruct((n_seq, GLA_HEADS, GLA_DK, GLA_DV), F32)],
        scratch_shapes=[pltpu.VMEM((nseq * GLA_HEADS, GLA_DV, GLA_DK), F32)],
        compiler_params=pltpu.CompilerParams(dimension_semantics=("arbitrary", "arbitrary"),
                                             vmem_limit_bytes=_vmem_limit(est)),
        name="gla_scan",
    )(q, k, v, g, ld, s0, norm_w)


def _dil_attn_body(*refs, blocks_per_residue):
    ins, outs = refs[:5 * N_GROUPS], refs[5 * N_GROUPS:]
    n = pl.program_id(1)
    row = lax.broadcasted_iota(jnp.int32, (ATT_BLK, 2 * ATT_BLK), 0)
    col = lax.broadcasted_iota(jnp.int32, (ATT_BLK, 2 * ATT_BLK), 1)
    dist = row + ATT_BLK - col
    band = (dist >= 0) & (dist <= ATT_BLK)
    lane = lax.broadcasted_iota(jnp.int32, (ATT_BLK, LANES), 1)
    for g in range(N_GROUPS):
        q_ref, kp_ref, kc_ref, vp_ref, vc_ref = ins[5 * g:5 * g + 5]
        o_ref, lse_ref = outs[2 * g:2 * g + 2]
        jb = _mod(n, blocks_per_residue[g])
        valid = band & ((col >= ATT_BLK) | (jb > 0))
        q = q_ref[0, 0]
        kk = jnp.concatenate([kp_ref[0, 0], kc_ref[0, 0]], axis=0)
        vv = jnp.concatenate([vp_ref[0, 0], vc_ref[0, 0]], axis=0)
        lse_tile = jnp.zeros((ATT_BLK, LANES), F32)
        for p in range(DIL_WIDTH // LANES):
            ls = slice(p * LANES, (p + 1) * LANES)
            q2, k2, v2 = q[:, ls], kk[:, ls], vv[:, ls]
            o_pair = jnp.zeros((ATT_BLK, LANES), F32)
            for e in range(LANES // DIL_HD):
                sel = (lane >= DIL_HD) if e else (lane < DIL_HD)
                s = _dot_nt(jnp.where(sel, q2, jnp.zeros_like(q2)), k2)
                s = jnp.where(valid, s, NEG_FILL)
                m = jnp.max(s, axis=-1, keepdims=True)
                pe = jnp.exp(s - m)
                den = jnp.sum(pe, axis=-1, keepdims=True)
                pv = _dot(pe.astype(BF16), v2)
                o_pair = jnp.where(sel, pv * (1.0 / den), o_pair)
                lse_tile = jnp.where(lane == p * (LANES // DIL_HD) + e, m + jnp.log(den), lse_tile)
            o_ref[0, 0, :, ls] = o_pair.astype(o_ref.dtype)
        lse_ref[0, 0] = lse_tile


def _dil_attn(qkv, *, n_seq, seq_len):
    args, in_specs, out_specs, out_shape, bpr = [], [], [], [], []
    n_blocks = seq_len // ATT_BLK
    for g, (w, r) in enumerate(DIL_GROUPS):
        assert w // r == ATT_BLK and seq_len % (r * ATT_BLK) == 0
        nb = n_blocks // r
        bpr.append(nb)
        cur = lambda b, n, nb, part: (b, n // nb, n % nb, part)
        prev = lambda b, n, nb, part: (b, n // nb, jnp.maximum(n % nb - 1, 0), part)
        blk = (1, 1, ATT_BLK, DIL_WIDTH)
        for fn, part in ((cur, 0), (prev, 1), (cur, 1), (prev, 2), (cur, 2)):
            args.append(qkv[g])
            in_specs.append(pl.BlockSpec(blk, functools.partial(fn, nb=nb, part=part)))
        omap = functools.partial(cur, nb=nb, part=0)
        out_specs += [pl.BlockSpec(blk, omap), pl.BlockSpec((1, 1, ATT_BLK, LANES), omap)]
        out_shape += [jax.ShapeDtypeStruct((n_seq, r, seq_len // r, DIL_WIDTH), qkv[g].dtype),
                      jax.ShapeDtypeStruct((n_seq, r, seq_len // r, LANES), F32)]
    est = 2 * N_GROUPS * (6 * _nbytes((ATT_BLK, DIL_WIDTH), qkv[0].dtype) + _nbytes((ATT_BLK, LANES), F32))
    outs = pl.pallas_call(
        functools.partial(_dil_attn_body, blocks_per_residue=tuple(bpr)),
        grid=(n_seq, n_blocks),
        in_specs=in_specs, out_specs=out_specs, out_shape=out_shape,
        compiler_params=pltpu.CompilerParams(dimension_semantics=("arbitrary", "arbitrary"),
                                             vmem_limit_bytes=_vmem_limit(est)),
        name="dil_attn",
    )(*args)
    return [outs[2 * g] for g in range(N_GROUPS)], [outs[2 * g + 1] for g in range(N_GROUPS)]


def _dec_attn_body(qkv0_ref, qkv1_ref, qkv2_ref, c0_ref, c1_ref, c2_ref, o_ref, *, t_new):
    qkv_refs = (qkv0_ref, qkv1_ref, qkv2_ref)
    c_refs = (c0_ref, c1_ref, c2_ref)
    nq = DIL_HEADS * t_new
    lane_w = lax.broadcasted_iota(jnp.int32, (nq, DIL_WIDTH), 1)
    row_w = lax.broadcasted_iota(jnp.int32, (nq, DIL_WIDTH), 0)
    own_head = _div(lane_w, DIL_HD) == _div(row_w, t_new)
    scores, values = [], []
    for g, (w, r) in enumerate(DIL_GROUPS):
        qkv = qkv_refs[g][0].astype(F32)
        q, k_new, v_new = qkv[:, :DIL_WIDTH], qkv[:, DIL_WIDTH:2 * DIL_WIDTH], qkv[:, 2 * DIL_WIDTH:]
        q_rep = jnp.broadcast_to(q[None], (DIL_HEADS, t_new, DIL_WIDTH)).reshape(nq, DIL_WIDTH)
        q_bd = jnp.where(own_head, q_rep, 0.0).astype(BF16)
        kt_c, vt_c = c_refs[g][0, 0].astype(BF16), c_refs[g][0, 1].astype(BF16)
        t_q = _mod(lax.broadcasted_iota(jnp.int32, (nq, w), 0), t_new)
        j = lax.broadcasted_iota(jnp.int32, (nq, w), 1)
        ok_c = (j >= t_q) & (_mod(w + t_q - j, r) == 0)
        scores.append(jnp.where(ok_c, _dot(q_bd, kt_c), NEG_FILL))
        values.append(vt_c)
        t_n = _mod(lax.broadcasted_iota(jnp.int32, (nq, LANES), 0), t_new)
        u = lax.broadcasted_iota(jnp.int32, (nq, LANES), 1)
        ok_n = (u <= t_n) & (_mod(t_n - u, r) == 0)
        scores.append(jnp.where(ok_n, _dot_nt(q_bd, _pad_rows(k_new, LANES).astype(BF16)), NEG_FILL))
        values.append(_pad_rows(v_new, LANES).astype(BF16))
    m = functools.reduce(jnp.maximum, [jnp.max(s, axis=-1, keepdims=True) for s in scores])
    den = jnp.zeros((nq, 1), F32)
    acc = jnp.zeros((nq, DIL_WIDTH), F32)
    for idx, (s, v) in enumerate(zip(scores, values)):
        pe = jnp.exp(s - m)
        den = den + jnp.sum(pe, axis=-1, keepdims=True)
        pe = pe.astype(BF16)
        acc = acc + (_dot(pe, v) if idx % 2 else _dot_nt(pe, v))
    acc = jnp.where(own_head, acc * (1.0 / den), 0.0)
    o_ref[0] = jnp.sum(acc.reshape(DIL_HEADS, t_new, DIL_WIDTH), axis=0)


def _dec_attn(qkv, caches, *, n_seq, t_new):
    args, in_specs = [], []
    for g in range(N_GROUPS):
        args.append(qkv[g].reshape(n_seq, t_new, 3 * DIL_WIDTH))
        in_specs.append(pl.BlockSpec((1, t_new, 3 * DIL_WIDTH), lambda b: (b, 0, 0)))
    est = 0
    for g, (w, r) in enumerate(DIL_GROUPS):
        c = jnp.transpose(caches[g], (0, 2, 3, 4, 1)).reshape(n_seq, 2, DIL_WIDTH, w)
        args.append(c)
        in_specs.append(pl.BlockSpec((1, 2, DIL_WIDTH, w), lambda b: (b, 0, 0, 0)))
        est += 2 * _nbytes((2, DIL_WIDTH, w), F32) + 3 * _nbytes((DIL_HEADS * t_new + DIL_WIDTH, w), F32)
    out = pl.pallas_call(
        functools.partial(_dec_attn_body, t_new=t_new),
        grid=(n_seq,),
        in_specs=in_specs,
        out_specs=pl.BlockSpec((1, t_new, DIL_WIDTH), lambda b: (b, 0, 0)),
        out_shape=jax.ShapeDtypeStruct((n_seq, t_new, DIL_WIDTH), F32),
        compiler_params=pltpu.CompilerParams(dimension_semantics=("arbitrary",), vmem_limit_bytes=_vmem_limit(est)),
        name="dec_attn",
    )(*args)
    return out.reshape(n_seq * t_new, DIL_WIDTH)


def _row_order(src_ref, stage_ref, r):
    if r == 1:
        return src_ref[0, 0].astype(F32)
    n = src_ref.shape[2]
    n_slab = src_ref.shape[3] // LANES
    for c in range(r):
        for j in range(n_slab):
            stage_ref[j, pl.ds(c, n, stride=r), :] = src_ref[0, c, :, j * LANES:(j + 1) * LANES].astype(F32)
    slabs = [stage_ref[j] for j in range(n_slab)]
    return slabs[0] if n_slab == 1 else jnp.concatenate(slabs, axis=1)


def _merge_body(*refs, n_dil_inputs):
    x_ref, og_ref = refs[:2]
    dil_refs = refs[2:2 + n_dil_inputs]
    (wg_ref, wa_ref, wb_ref, wo_ref, ex_ref, g_ref, b_ref, y_ref) = refs[2 + n_dil_inputs:10 + n_dil_inputs]
    x = x_ref[...]
    xb = x.astype(BF16)
    if n_dil_inputs == 1:
        yb_in = dil_refs[0][...].astype(F32)
    else:
        ostage_ref, lstage_ref = refs[10 + n_dil_inputs:]
        stages = [(None, None)] + [(ostage_ref.at[g - 1], lstage_ref.at[g - 1]) for g in range(1, N_GROUPS)]
        os_ = [_row_order(dil_refs[g], stages[g][0], DIL_GROUPS[g][1]) for g in range(N_GROUPS)]
        lses = [_row_order(dil_refs[N_GROUPS + g], stages[g][1], DIL_GROUPS[g][1]) for g in range(N_GROUPS)]
        top = functools.reduce(jnp.maximum, lses)
        es = [jnp.exp(l - top) for l in lses]
        inv = 1.0 / functools.reduce(jnp.add, es)
        yb_in = jnp.zeros(os_[0].shape, F32)
        for o, e in zip(os_, es):
            wgt = e * inv
            hi = wgt.astype(BF16)
            lo = (wgt - hi.astype(F32)).astype(BF16)
            wide = _dot(hi, ex_ref[...]) + _dot(lo, ex_ref[...])
            yb_in = yb_in + wide * o
    ya = _dot(og_ref[...].astype(BF16), wa_ref[...])
    yb = _dot(yb_in.astype(BF16), wb_ref[...])
    ga = _dot(xb, wg_ref[:, :D_MODEL])
    gb = _dot(xb, wg_ref[:, D_MODEL:])
    mix = _sigmoid(ga) * ya + _sigmoid(gb) * yb
    z = ALPHA * x + _dot(mix.astype(BF16), wo_ref[...])
    y_ref[...] = _layer_norm(z, g_ref[...], b_ref[...])


def _merge(x2d, o_gla, dil_inputs, w_gates, w_br_gla, w_br_dil, w_out, expand, ln_g, ln_b, *, tm, seq_len):
    m = x2d.shape[0]
    row = lambda i: (i, 0)
    tiles_per_seq = max(seq_len // tm, 1)
    acts, act_specs, scratch = [x2d, o_gla], [pl.BlockSpec((tm, D_MODEL), row), pl.BlockSpec((tm, GLA_V), row)], []
    est = 2 * _nbytes((tm, D_MODEL), F32) + 2 * _nbytes((tm, GLA_V), o_gla.dtype)
    for a in dil_inputs:
        acts.append(a)
        if a.ndim == 2:
            act_specs.append(pl.BlockSpec((tm, a.shape[1]), row))
        else:
            r = a.shape[1]
            assert seq_len % tm == 0 and tm % r == 0
            act_specs.append(pl.BlockSpec((1, r, tm // r, a.shape[3]), lambda i: (i // tiles_per_seq, 0, i % tiles_per_seq, 0)))
        est += 2 * _nbytes((tm, a.shape[-1]), a.dtype)
    if len(dil_inputs) > 1:
        scratch = [pltpu.VMEM((N_GROUPS - 1, DIL_WIDTH // LANES, tm, LANES), F32),
                   pltpu.VMEM((N_GROUPS - 1, 1, tm, LANES), F32)]
        est += (N_GROUPS - 1) * _nbytes((tm, DIL_WIDTH + LANES), F32)
    weights = [w_gates, w_br_gla, w_br_dil, w_out, expand, ln_g, ln_b]
    est += sum(_nbytes(w.shape, w.dtype) for w in weights) + 6 * _nbytes((tm, D_MODEL), F32)
    return pl.pallas_call(
        functools.partial(_merge_body, n_dil_inputs=len(dil_inputs)),
        grid=(m // tm,),
        in_specs=act_specs + [_resident(w.shape) for w in weights],
        out_specs=pl.BlockSpec((tm, D_MODEL), row),
        out_shape=jax.ShapeDtypeStruct((m, D_MODEL), F32),
        scratch_shapes=scratch,
        compiler_params=pltpu.CompilerParams(dimension_semantics=("arbitrary",), vmem_limit_bytes=_vmem_limit(est)),
        name="merge",
    )(*acts, *weights)


def _ffn_body(*refs, tm, tiles_per_seq, seq_len, tail_rows):
    long_seq = seq_len >= tm
    if long_seq:
        x_ref, halo_ref, cp_ref, pe_ref = refs[:4]
    else:
        x_ref, cp0_ref, cp1_ref, pe_ref = refs[:4]
    (wup_ref, cw_ref, cb_ref, wdn_ref, g2_ref, b2_ref, wpg_ref, wpp_ref, g3_ref, b3_ref, y_ref, tail_ref) = refs[4:]
    i = pl.program_id(0)
    x = x_ref[...]
    xb = x.astype(BF16)
    row = lax.broadcasted_iota(jnp.int32, (tm, FF_CHUNK), 0)
    acc = jnp.zeros((tm, D_MODEL), F32)
    for c in range(D_FF // FF_CHUNK):
        cs = slice(c * FF_CHUNK, (c + 1) * FF_CHUNK)
        a = _dot(xb, wup_ref[:, cs])
        u = _dot(xb, wup_ref[:, D_FF + c * FF_CHUNK:D_FF + (c + 1) * FF_CHUNK])
        if long_seq:
            a_halo = _dot(halo_ref[...].astype(BF16), wup_ref[:, cs])
            prev = jnp.where(i % tiles_per_seq == 0, cp_ref[0][:, cs], a_halo)
            p1, p2 = prev[SUBLANES - 1:SUBLANES, :], prev[SUBLANES - 2:SUBLANES - 1, :]
            t = row
        else:
            p1, p2 = cp1_ref[:, cs], cp0_ref[:, cs]
            t = _mod(row, seq_len)
        am1 = jnp.where(t == 0, p1, pltpu.roll(a, 1, 0))
        am2 = jnp.where(t == 0, p2, jnp.where(t == 1, p1, pltpu.roll(a, 2, 0)))
        cw = cw_ref[:, cs]
        conv = cb_ref[:, cs] + cw[0:1, :] * am2 + cw[1:2, :] * am1 + cw[2:3, :] * a
        gelu = 0.5 * conv * (1.0 + lax.erf(conv * (2.0 ** -0.5)))
        acc = acc + _dot((gelu * u).astype(BF16), wdn_ref[cs, :])
        tail_ref[:, cs] = a[tm - tail_rows:, :]
    x2 = _layer_norm(ALPHA * x + acc, g2_ref[...], b2_ref[...])
    gate = _sigmoid(_dot(x2.astype(BF16), wpg_ref[...]))
    emb = _dot(pe_ref[...].astype(BF16), wpp_ref[...])
    y_ref[...] = _layer_norm(ALPHA * x2 + gate * emb, g3_ref[...], b3_ref[...])


def _ffn(x2d, conv_prev, pe2d, w_up, conv_w, conv_b, w_down, ln2_g, ln2_b, w_pg, w_pp, ln3_g, ln3_b, *, tm, seq_len):
    m = x2d.shape[0]
    n_seq = m // seq_len
    row = lambda i: (i, 0)
    n_tiles = m // tm
    if seq_len >= tm:
        tiles_per_seq, tail_rows = seq_len // tm, SUBLANES
        cp = jnp.concatenate([jnp.zeros((n_seq, SUBLANES - (CONV_W - 1), D_FF), F32), conv_prev], axis=1)
        acts = [x2d, x2d, cp, pe2d]
        act_specs = [pl.BlockSpec((tm, D_MODEL), row),
                     pl.BlockSpec((SUBLANES, D_MODEL), lambda i: (jnp.maximum(i * (tm // SUBLANES) - 1, 0), 0)),
                     pl.BlockSpec((1, SUBLANES, D_FF), lambda i: (i // tiles_per_seq, 0, 0)),
                     pl.BlockSpec((tm, PLE_DIM), row)]
        act_bytes = 2 * _nbytes((tm, D_MODEL), F32) + _nbytes((SUBLANES, D_FF), F32)
    else:
        assert tm % seq_len == 0 and seq_len >= CONV_W - 1
        tiles_per_seq, tail_rows = 1, tm
        cp0 = jnp.repeat(conv_prev[:, 0, :], seq_len, axis=0)
        cp1 = jnp.repeat(conv_prev[:, 1, :], seq_len, axis=0)
        acts = [x2d, cp0, cp1, pe2d]
        act_specs = [pl.BlockSpec((tm, D_MODEL), row), pl.BlockSpec((tm, D_FF), row), pl.BlockSpec((tm, D_FF), row),
                     pl.BlockSpec((tm, PLE_DIM), row)]
        act_bytes = _nbytes((tm, D_MODEL), F32) + 2 * _nbytes((tm, D_FF), F32)
    weights = [w_up, conv_w, conv_b, w_down, ln2_g, ln2_b, w_pg, w_pp, ln3_g, ln3_b]
    est = (2 * act_bytes + sum(_nbytes(w.shape, w.dtype) for w in weights) + 4 * _nbytes((tm, D_MODEL), F32)
           + 2 * _nbytes((tail_rows, D_FF), F32) + 6 * _nbytes((tm, FF_CHUNK), F32))
    y, tail = pl.pallas_call(
        functools.partial(_ffn_body, tm=tm, tiles_per_seq=tiles_per_seq, seq_len=seq_len, tail_rows=tail_rows),
        grid=(n_tiles,),
        in_specs=act_specs + [_resident(w.shape) for w in weights],
        out_specs=[pl.BlockSpec((tm, D_MODEL), row), pl.BlockSpec((tail_rows, D_FF), row)],
        out_shape=[jax.ShapeDtypeStruct((m, D_MODEL), F32), jax.ShapeDtypeStruct((n_tiles * tail_rows, D_FF), F32)],
        compiler_params=pltpu.CompilerParams(dimension_semantics=("arbitrary",), vmem_limit_bytes=_vmem_limit(est)),
        name="ffn",
    )(*acts, *weights)
    return y, tail


def _tile_rows(m):
    return min(m, 512)


def _layer(x, pe, s0, conv_prev, kv_cache, pos_offset, wts):
    n_seq, t, _ = x.shape
    m = n_seq * t
    x2d = x.reshape(m, D_MODEL)
    prompt = kv_cache is None
    act_dtype = BF16 if prompt else F32
    tm = _tile_rows(m)

    q, k, v, g, ld = _gla_proj(x2d, wts["w_gla"], wts["w_lr"], wts["w_gkb"], wts["b_gk"], tm=tm, act_dtype=act_dtype)
    qkv, kv_new = _dil_proj(x2d, wts["w_dil"], wts["inv_freq"], tm=tm, seq_len=t, pos_offset=pos_offset, act_dtype=act_dtype)

    r3 = lambda a: a.reshape(n_seq, t, a.shape[-1])
    cc = min(t, GLA_CHUNK)
    nseq = n_seq if prompt else 8
    o_gla, s_new = _gla_scan(r3(q), r3(k), r3(v), r3(g), r3(ld), s0, wts["gla_norm"], nseq=nseq, cc=cc)
    o_gla = o_gla.reshape(m, GLA_V)

    if prompt:
        o_dil, lse = _dil_attn(qkv, n_seq=n_seq, seq_len=t)
        dil_inputs = o_dil + lse
    else:
        dil_inputs = [_dec_attn(qkv, kv_cache, n_seq=n_seq, t_new=t)]

    x1 = _merge(x2d, o_gla, dil_inputs, wts["w_gates"], wts["w_br_gla"], wts["w_br_dil"], wts["w_out"], wts["expand"],
                wts["ln1_g"], wts["ln1_b"], tm=tm, seq_len=t)
    tm_ffn = tm if prompt else min(tm, 256)
    y, tail = _ffn(x1, conv_prev, pe.reshape(m, PLE_DIM), wts["w_up"], wts["conv_w"], wts["conv_b"], wts["w_down"],
                   wts["ln2_g"], wts["ln2_b"], wts["w_ple_gate"], wts["w_ple_proj"], wts["ln3_g"], wts["ln3_b"],
                   tm=tm_ffn, seq_len=t)
    if t >= tm_ffn:
        tiles_per_seq = t // tm_ffn
        conv_new = tail.reshape(n_seq, tiles_per_seq, SUBLANES, D_FF)[:, -1, SUBLANES - (CONV_W - 1):, :]
    else:
        conv_new = tail.reshape(n_seq, t, D_FF)[:, t - (CONV_W - 1):, :]
    if prompt:
        kv_out = [jnp.transpose(kvn.reshape(n_seq, 2, DIL_HEADS, DIL_HD, -1), (0, 4, 1, 2, 3)) for kvn in kv_new]
    else:
        kv_out = [kvn.reshape(n_seq, -1, 2, DIL_HEADS, DIL_HD) for kvn in kv_new]
    return y.reshape(n_seq, t, D_MODEL), s_new, conv_new, kv_out


def _prep_weights(w_in, w_gk_b, b_gk, gla_norm, w_br_gla, w_br_dil, w_out, ln1_g, ln1_b, w_up, conv_w, conv_b, w_down,
                  ln2_g, ln2_b, w_ple_gate, w_ple_proj, ln3_g, ln3_b):
    w_in_b = w_in[0].astype(BF16)
    vec = lambda a: a[0].reshape(1, -1).astype(F32)
    half = DIL_HD // 2
    inv = ROPE_THETA ** (-jnp.arange(half, dtype=F32) / half)
    head_of_lane = jnp.arange(DIL_WIDTH) // DIL_HD
    expand = (jnp.arange(LANES)[:, None] == head_of_lane[None, :]).astype(BF16)
    return {
        "w_gla": w_in_b[:, :COL_GLA],
        "w_lr": jnp.pad(w_in_b[:, COL_GLA:COL_LR], ((0, 0), (0, LANES - GLA_RANK))),
        "w_gkb": jnp.pad(w_gk_b[0].astype(BF16), ((0, LANES - GLA_RANK), (0, 0))),
        "b_gk": vec(b_gk),
        "w_dil": w_in_b[:, COL_LR:COL_DIL],
        "w_gates": w_in_b[:, COL_DIL:],
        "inv_freq": jnp.tile(inv, LANES // half).reshape(1, LANES),
        "gla_norm": vec(gla_norm),
        "w_br_gla": w_br_gla[0].astype(BF16), "w_br_dil": w_br_dil[0].astype(BF16), "w_out": w_out[0].astype(BF16),
        "expand": expand,
        "ln1_g": vec(ln1_g), "ln1_b": vec(ln1_b),
        "w_up": w_up[0].astype(BF16),
        "conv_w": jnp.pad(conv_w[0].astype(F32), ((0, SUBLANES - CONV_W), (0, 0))),
        "conv_b": vec(conv_b),
        "w_down": w_down[0].astype(BF16),
        "ln2_g": vec(ln2_g), "ln2_b": vec(ln2_b),
        "w_ple_gate": w_ple_gate[0].astype(BF16), "w_ple_proj": w_ple_proj[0].astype(BF16),
        "ln3_g": vec(ln3_g), "ln3_b": vec(ln3_b),
    }


def kernel(x_prompt, x_sample, p_prompt, p_sample, state_gla, cache_conv, cache_kv_w128, cache_kv_w512, cache_kv_w2048, w_in, w_gk_b, b_gk, gla_norm, w_br_gla, w_br_dil, w_out, ln1_g, ln1_b, w_up, conv_w, conv_b, w_down, ln2_g, ln2_b, w_ple_gate, w_ple_proj, ln3_g, ln3_b):
    assert w_in.shape[0] == DEPTH == 1
    wts = _prep_weights(w_in, w_gk_b, b_gk, gla_norm, w_br_gla, w_br_dil, w_out, ln1_g, ln1_b, w_up, conv_w, conv_b,
                        w_down, ln2_g, ln2_b, w_ple_gate, w_ple_proj, ln3_g, ln3_b)
    n_p = x_prompt.shape[0]
    s0_p = jnp.zeros((n_p, GLA_HEADS, GLA_DK, GLA_DV), state_gla.dtype)
    c0_p = jnp.zeros((n_p, CONV_W - 1, D_FF), x_prompt.dtype)
    y_p, s_p, conv_p, kv_p = _layer(x_prompt, p_prompt[0], s0_p, c0_p, None, 0, wts)
    caches = (cache_kv_w128[0], cache_kv_w512[0], cache_kv_w2048[0])
    y_s, s_s, conv_s, kv_s = _layer(x_sample, p_sample[0], state_gla[0], cache_conv[0], caches, PAST_LEN, wts)
    return (y_p, y_s, s_p[None], s_s[None], conv_p[None], conv_s[None],
            kv_p[0][None], kv_p[1][None], kv_p[2][None], kv_s[0][None], kv_s[1][None], kv_s[2][None])
```

```python
import functools

import jax
import jax.numpy as jnp
from jax import lax
from jax.experimental import pallas as pl
from jax.experimental.pallas import tpu as pltpu

F32 = jnp.float32
BF16 = jnp.bfloat16

D_MODEL = 1024
DEPTH = 1
PAST_LEN = 8192
GLA_HEADS = 4
GLA_DK = 128
GLA_DV = 256
GLA_QK = GLA_HEADS * GLA_DK
GLA_V = GLA_HEADS * GLA_DV
GLA_RANK = 16
GLA_NORMALIZER = 16.0
DIL_GROUPS = ((128, 1), (512, 4), (2048, 16))
N_GROUPS = len(DIL_GROUPS)
DIL_HEADS = 8
DIL_HD = 64
DIL_WIDTH = DIL_HEADS * DIL_HD
DIL_QKV = N_GROUPS * DIL_WIDTH
ROPE_THETA = 10000.0
D_FF = 2816
CONV_W = 3
PLE_DIM = 256
ALPHA = (2.0 * DEPTH) ** 0.25
NORM_EPS = 1e-5
COL_GLA = 2 * GLA_QK + 2 * GLA_V
COL_LR = COL_GLA + GLA_RANK
COL_DIL = COL_LR + 3 * DIL_QKV
NEG_FILL = -1e30

LANES = 128
SUBLANES = 8
VMEM_BYTES_V7X = 64 * 1024 * 1024
VMEM_LIMIT_CAP = 60000 * 1024

ATT_BLK = 128
GLA_CHUNK = 128
GLA_SUB = 16
FF_CHUNK = D_FF // 2


def _vmem_limit(nbytes):
    return int(min(max(2 * nbytes, 16 * 1024 * 1024), VMEM_LIMIT_CAP))


def _nbytes(shape, dtype):
    n = 1
    for s in shape:
        n *= s
    return n * jnp.dtype(dtype).itemsize


def _sigmoid(x):
    return 1.0 / (1.0 + jnp.exp(-x))


def _log_sigmoid(x):
    return jnp.minimum(x, 0.0) - jnp.log1p(jnp.exp(-jnp.abs(x)))


def _layer_norm(z, g, b):
    mu = jnp.mean(z, axis=-1, keepdims=True)
    d = z - mu
    var = jnp.mean(d * d, axis=-1, keepdims=True)
    return d * lax.rsqrt(var + NORM_EPS) * g + b


def _log2(n):
    assert n > 0 and n & (n - 1) == 0, n
    return n.bit_length() - 1


def _div(x, n):
    return x >> _log2(n)


def _mod(x, n):
    _log2(n)
    return x & (n - 1)


def _resident(shape):
    return pl.BlockSpec(shape, lambda *_: (0,) * len(shape), pipeline_mode=pl.Buffered(1))


def _dot(a, b):
    return jnp.dot(a, b, preferred_element_type=F32)


def _dot_nt(a, b):
    return lax.dot_general(a, b, (((1,), (1,)), ((), ())), preferred_element_type=F32)


def _gla_proj_body(x_ref, w_ref, wlr_ref, wgkb_ref, bgk_ref, q_ref, k_ref, v_ref, g_ref, ld_ref):
    xb = x_ref[...].astype(BF16)
    cw = GLA_QK

    def proj(lo):
        return _dot(xb, w_ref[:, lo:lo + cw])

    q_ref[...] = (proj(0) * GLA_DK ** -0.5).astype(q_ref.dtype)
    k_ref[...] = proj(GLA_QK).astype(k_ref.dtype)
    for c in range(GLA_V // cw):
        v_ref[:, c * cw:(c + 1) * cw] = proj(2 * GLA_QK + c * cw).astype(v_ref.dtype)
        gg = proj(2 * GLA_QK + GLA_V + c * cw)
        g_ref[:, c * cw:(c + 1) * cw] = (gg * _sigmoid(gg)).astype(g_ref.dtype)
    glr = _dot(xb, wlr_ref[...])
    z = _dot(glr.astype(BF16), wgkb_ref[...]) + bgk_ref[...]
    ld_ref[...] = _log_sigmoid(z) * (1.0 / GLA_NORMALIZER)


def _gla_proj(x2d, w_gla, w_lr, w_gkb, b_gk, *, tm, act_dtype):
    m = x2d.shape[0]
    row = lambda i: (i, 0)
    out_shape = [
        jax.ShapeDtypeStruct((m, GLA_QK), act_dtype), jax.ShapeDtypeStruct((m, GLA_QK), act_dtype),
        jax.ShapeDtypeStruct((m, GLA_V), act_dtype), jax.ShapeDtypeStruct((m, GLA_V), act_dtype),
        jax.ShapeDtypeStruct((m, GLA_QK), F32),
    ]
    weights = [w_gla, w_lr, w_gkb, b_gk]
    est = (2 * _nbytes((tm, D_MODEL), F32) + sum(_nbytes(w.shape, w.dtype) for w in weights)
           + 2 * sum(_nbytes((tm, s.shape[1]), s.dtype) for s in out_shape) + 2 * _nbytes((tm, GLA_QK), F32))
    return pl.pallas_call(
        _gla_proj_body,
        grid=(m // tm,),
        in_specs=[pl.BlockSpec((tm, D_MODEL), row)] + [_resident(w.shape) for w in weights],
        out_specs=[pl.BlockSpec((tm, s.shape[1]), row) for s in out_shape],
        out_shape=out_shape,
        compiler_params=pltpu.CompilerParams(dimension_semantics=("arbitrary",), vmem_limit_bytes=_vmem_limit(est)),
        name="gla_proj",
    )(x2d, w_gla, w_lr, w_gkb, b_gk)


def _dil_proj_body(x_ref, w_ref, inv_ref, *refs, tm, seq_len, pos_offset, regroup):
    qkv_refs, kv_refs = refs[:N_GROUPS], refs[N_GROUPS:2 * N_GROUPS]
    trig_ref = refs[2 * N_GROUPS]
    stage_ref = refs[2 * N_GROUPS + 1] if regroup else None
    i = pl.program_id(0)
    xb = x_ref[...].astype(BF16)
    half = DIL_HD // 2
    tiles_per_seq = max(seq_len // tm, 1)
    it = _mod(i, tiles_per_seq)
    lane = lax.broadcasted_iota(jnp.int32, (tm, LANES), 1)

    @pl.when(i == 0)
    def _():
        rel = _mod(lax.broadcasted_iota(jnp.int32, (tm, LANES), 0), seq_len).astype(F32) * inv_ref[...]
        trig_ref[0] = jnp.cos(rel)
        trig_ref[1] = jnp.sin(rel)

    base = (pos_offset + it * tm).astype(F32) * inv_ref[...]
    cos_b, sin_b = jnp.cos(base), jnp.sin(base)
    cos = cos_b * trig_ref[0] - sin_b * trig_ref[1]
    sin = sin_b * trig_ref[0] + cos_b * trig_ref[1]
    first_half = _mod(lane, DIL_HD) < half
    sin_signed = jnp.where(first_half, -sin, sin)

    def rope(xc):
        partner = jnp.where(first_half, pltpu.roll(xc, LANES - half, 1), pltpu.roll(xc, half, 1))
        return xc * cos + partner * sin_signed

    n_slab = DIL_WIDTH // LANES
    for g, (w, r) in enumerate(DIL_GROUPS):
        hq = _dot(xb, w_ref[:, g * DIL_WIDTH:(g + 1) * DIL_WIDTH])
        hk = _dot(xb, w_ref[:, DIL_QKV + g * DIL_WIDTH:DIL_QKV + (g + 1) * DIL_WIDTH])
        hv = _dot(xb, w_ref[:, 2 * DIL_QKV + g * DIL_WIDTH:2 * DIL_QKV + (g + 1) * DIL_WIDTH])
        slabs = []
        for part, h in enumerate((hq, hk, hv)):
            for c in range(n_slab):
                xc = h[:, c * LANES:(c + 1) * LANES]
                slabs.append(xc if part == 2 else rope(xc) * (DIL_HD ** -0.5 if part == 0 else 1.0))
        if not regroup:
            for j, xc in enumerate(slabs):
                qkv_refs[g][:, j * LANES:(j + 1) * LANES] = xc.astype(qkv_refs[g].dtype)
                if j >= n_slab:
                    kv_refs[g][:, (j - n_slab) * LANES:(j - n_slab + 1) * LANES] = xc
            continue
        for j, xc in enumerate(slabs):
            if r == 1:
                qkv_refs[g][0, 0, :, j * LANES:(j + 1) * LANES] = xc.astype(qkv_refs[g].dtype)
            else:
                stage_ref[g - 1, j] = xc
        for c in range(r if r > 1 else 0):
            for j in range(3 * n_slab):
                qkv_refs[g][0, c, :, j * LANES:(j + 1) * LANES] = (
                    stage_ref[g - 1, j, pl.ds(c, tm // r, stride=r), :].astype(qkv_refs[g].dtype))
        keep = min(w, tm)

        @pl.when(it >= tiles_per_seq - w // keep)
        def _(g=g, keep=keep, slabs=slabs):
            for j in range(2 * n_slab):
                kv_refs[g][0, j // n_slab, (j % n_slab) * LANES:(j % n_slab + 1) * LANES, :] = (
                    slabs[n_slab + j][tm - keep:, :].T)


def _dil_proj(x2d, w_dil, inv_freq, *, tm, seq_len, pos_offset, act_dtype):
    m = x2d.shape[0]
    n_seq = m // seq_len
    regroup = seq_len >= tm
    row = lambda i: (i, 0)
    tiles_per_seq = max(seq_len // tm, 1)
    qkv_shapes, qkv_specs, kv_shapes, kv_specs, scratch = [], [], [], [], []
    for w, r in DIL_GROUPS:
        if regroup:
            assert tm % r == 0 and seq_len % tm == 0
            keep = min(w, tm)
            nblk = w // keep
            qkv_shapes.append(jax.ShapeDtypeStruct((n_seq, r, seq_len // r, 3 * DIL_WIDTH), act_dtype))
            qkv_specs.append(pl.BlockSpec((1, r, tm // r, 3 * DIL_WIDTH),
                                          lambda i: (i // tiles_per_seq, 0, i % tiles_per_seq, 0)))
            kv_shapes.append(jax.ShapeDtypeStruct((n_seq, 2, DIL_WIDTH, w), F32))
            kv_specs.append(pl.BlockSpec(
                (1, 2, DIL_WIDTH, keep),
                functools.partial(lambda i, nblk: (i // tiles_per_seq, 0, 0,
                                                   jnp.maximum(i % tiles_per_seq - (tiles_per_seq - nblk), 0)), nblk=nblk)))
        else:
            assert w >= seq_len
            qkv_shapes.append(jax.ShapeDtypeStruct((m, 3 * DIL_WIDTH), act_dtype))
            qkv_specs.append(pl.BlockSpec((tm, 3 * DIL_WIDTH), row))
            kv_shapes.append(jax.ShapeDtypeStruct((m, 2 * DIL_WIDTH), F32))
            kv_specs.append(pl.BlockSpec((tm, 2 * DIL_WIDTH), row))
    est = (2 * _nbytes((tm, D_MODEL), F32) + _nbytes(w_dil.shape, BF16)
           + 2 * N_GROUPS * _nbytes((tm, 3 * DIL_WIDTH), act_dtype)
           + 2 * sum(_nbytes(s.block_shape, F32) for s in kv_specs) + 3 * _nbytes((tm, DIL_WIDTH), F32))
    scratch.append(pltpu.VMEM((2, tm, LANES), F32))
    if regroup:
        assert DIL_GROUPS[0][1] == 1 and all(r > 1 for _, r in DIL_GROUPS[1:])
        scratch.append(pltpu.VMEM((N_GROUPS - 1, 3 * DIL_WIDTH // LANES, tm, LANES), F32))
        est += _nbytes((N_GROUPS - 1, tm, 3 * DIL_WIDTH), F32)
    body = functools.partial(_dil_proj_body, tm=tm, seq_len=seq_len, pos_offset=pos_offset, regroup=regroup)
    outs = pl.pallas_call(
        body,
        grid=(m // tm,),
        in_specs=[pl.BlockSpec((tm, D_MODEL), row), _resident(w_dil.shape), _resident(inv_freq.shape)],
        out_specs=qkv_specs + kv_specs,
        out_shape=qkv_shapes + kv_shapes,
        scratch_shapes=scratch,
        compiler_params=pltpu.CompilerParams(dimension_semantics=("arbitrary",), vmem_limit_bytes=_vmem_limit(est)),
        name="dil_proj",
    )(x2d, w_dil, inv_freq)
    return outs[:N_GROUPS], outs[N_GROUPS:]


def _cumsum_rows(x):
    n = x.shape[0]
    row = lax.broadcasted_iota(jnp.int32, x.shape, 0)
    s = 1
    while s < n:
        x = x + jnp.where(row >= s, pltpu.roll(x, s, 0), 0.0)
        s *= 2
    return x


def _pad_rows(x, n):
    if x.shape[0] == n:
        return x
    return jnp.concatenate([x, jnp.zeros((n - x.shape[0], x.shape[1]), x.dtype)], axis=0)


def _gla_scan_body(q_ref, k_ref, v_ref, g_ref, ld_ref, s0_ref, nw_ref, o_ref, sout_ref, st_ref, *, nseq, cc):
    chunk = pl.program_id(1)
    n_chunks = pl.num_programs(1)
    sub = GLA_SUB
    cp = cc if cc % sub == 0 else sub * (cc // sub + 1)
    nsub = cp // sub

    @pl.when(chunk == 0)
    def _():
        for s in range(nseq):
            for h in range(GLA_HEADS):
                st_ref[s * GLA_HEADS + h] = s0_ref[s, h].T

    row = lax.broadcasted_iota(jnp.int32, (cp, GLA_QK), 0)
    arow = lax.broadcasted_iota(jnp.int32, (cp, cp), 0)
    acol = lax.broadcasted_iota(jnp.int32, (cp, cp), 1)
    for s in range(nseq):
        q = _pad_rows(q_ref[s].astype(F32), cp)
        k = _pad_rows(k_ref[s].astype(F32), cp)
        b = _cumsum_rows(_pad_rows(ld_ref[s], cp))
        starts = [jnp.zeros((1, GLA_QK), F32)] + [b[i * sub - 1:i * sub, :] for i in range(1, nsub)]
        r = jnp.concatenate([jnp.broadcast_to(st, (sub, GLA_QK)) for st in starts], axis=0)
        qh = q * jnp.exp(b - r)
        qt = (qh * jnp.exp(r)).astype(BF16)
        b_last = b[cp - 1:cp, :]
        kl = (k * jnp.exp(b_last - b)).astype(BF16)
        a_rows = [[] for _ in range(GLA_HEADS)]
        for i in range(nsub):
            e = jnp.where(row < (i + 1) * sub, starts[i] - b, NEG_FILL)
            ki = (k * jnp.exp(e)).astype(BF16)
            qi = qh[i * sub:(i + 1) * sub, :].astype(BF16)
            for h in range(GLA_HEADS):
                hs = slice(h * GLA_DK, (h + 1) * GLA_DK)
                a_rows[h].append(_dot_nt(qi[:, hs], ki[:, hs]))
        d_last = jnp.exp(b_last)
        for h in range(GLA_HEADS):
            hs = slice(h * GLA_DK, (h + 1) * GLA_DK)
            vs = slice(h * GLA_DV, (h + 1) * GLA_DV)
            a = jnp.where(acol <= arow, jnp.concatenate(a_rows[h], axis=0), 0.0).astype(BF16)
            v = _pad_rows(v_ref[s][:, vs].astype(F32), cp)
            st = st_ref[s * GLA_HEADS + h]
            o = (_dot_nt(qt[:, hs], st.astype(BF16)) + _dot(a, v.astype(BF16)))[:cc]
            o = o * lax.rsqrt(jnp.mean(o * o, axis=-1, keepdims=True) + NORM_EPS) * nw_ref[...]
            o_ref[s, :, vs] = (o * g_ref[s][:, vs].astype(F32)).astype(o_ref.dtype)
            vt = _pad_rows(v, GLA_CHUNK).T.astype(BF16)
            st_ref[s * GLA_HEADS + h] = d_last[:, hs] * st + _dot(vt, _pad_rows(kl[:, hs], GLA_CHUNK))

    @pl.when(chunk == n_chunks - 1)
    def _():
        for s in range(nseq):
            for h in range(GLA_HEADS):
                sout_ref[s, h] = st_ref[s * GLA_HEADS + h].T


def _gla_scan(q, k, v, g, ld, s0, norm_w, *, nseq, cc):
    n_seq, t, _ = q.shape
    assert n_seq % nseq == 0 and t % cc == 0 and cc <= GLA_CHUNK
    blk = lambda sg, c: (sg, c, 0)
    sblk = lambda sg, c: (sg, 0, 0, 0)
    body = functools.partial(_gla_scan_body, nseq=nseq, cc=cc)
    est = (2 * nseq * cc * (2 * GLA_QK + 2 * GLA_V) * jnp.dtype(q.dtype).itemsize + 2 * nseq * cc * GLA_QK * 4
           + 2 * nseq * cc * GLA_V * jnp.dtype(q.dtype).itemsize + 5 * nseq * GLA_HEADS * GLA_DK * GLA_DV * 4)
    return pl.pallas_call(
        body,
        grid=(n_seq // nseq, t // cc),
        in_specs=[pl.BlockSpec((nseq, cc, GLA_QK), blk), pl.BlockSpec((nseq, cc, GLA_QK), blk),
                  pl.BlockSpec((nseq, cc, GLA_V), blk), pl.BlockSpec((nseq, cc, GLA_V), blk),
                  pl.BlockSpec((nseq, cc, GLA_QK), blk),
                  pl.BlockSpec((nseq, GLA_HEADS, GLA_DK, GLA_DV), sblk),
                  pl.BlockSpec(norm_w.shape, lambda sg, c: (0, 0))],
        out_specs=[pl.BlockSpec((nseq, cc, GLA_V), blk), pl.BlockSpec((nseq, GLA_HEADS, GLA_DK, GLA_DV), sblk)],
        out_shape=[jax.ShapeDtypeStruct((n_seq, t, GLA_V), q.dtype),
                   jax.ShapeDtypeStruct((n_seq, GLA_HEADS, GLA_DK, GLA_DV), F32)],
        scratch_shapes=[pltpu.VMEM((nseq * GLA_HEADS, GLA_DV, GLA_DK), F32)],
        compiler_params=pltpu.CompilerParams(dimension_semantics=("arbitrary", "arbitrary"),
                                             vmem_limit_bytes=_vmem_limit(est)),
        name="gla_scan",
    )(q, k, v, g, ld, s0, norm_w)


def _dil_attn_block(ins, outs, n, blocks_per_residue):
    row = lax.broadcasted_iota(jnp.int32, (ATT_BLK, 2 * ATT_BLK), 0)
    col = lax.broadcasted_iota(jnp.int32, (ATT_BLK, 2 * ATT_BLK), 1)
    dist = row + ATT_BLK - col
    band = (dist >= 0) & (dist <= ATT_BLK)
    lane = lax.broadcasted_iota(jnp.int32, (ATT_BLK, LANES), 1)
    for g in range(N_GROUPS):
        q_ref, kp_ref, kc_ref, vp_ref, vc_ref = ins[5 * g:5 * g + 5]
        o_ref, lse_ref = outs[2 * g:2 * g + 2]
        jb = _mod(n, blocks_per_residue[g])
        valid = band & ((col >= ATT_BLK) | (jb > 0))
        q = q_ref[0, 0]
        kk = jnp.concatenate([kp_ref[0, 0], kc_ref[0, 0]], axis=0)
        vv = jnp.concatenate([vp_ref[0, 0], vc_ref[0, 0]], axis=0)
        lse_tile = jnp.zeros((ATT_BLK, LANES), F32)
        for p in range(DIL_WIDTH // LANES):
            ls = slice(p * LANES, (p + 1) * LANES)
            q2, k2, v2 = q[:, ls], kk[:, ls], vv[:, ls]
            o_pair = jnp.zeros((ATT_BLK, LANES), F32)
            for e in range(LANES // DIL_HD):
                sel = (lane >= DIL_HD) if e else (lane < DIL_HD)
                s = _dot_nt(jnp.where(sel, q2, jnp.zeros_like(q2)), k2)
                s = jnp.where(valid, s, NEG_FILL)
                m = jnp.max(s, axis=-1, keepdims=True)
                pe = jnp.exp(s - m)
                den = jnp.sum(pe, axis=-1, keepdims=True)
                pv = _dot(pe.astype(BF16), v2)
                o_pair = jnp.where(sel, pv * (1.0 / den), o_pair)
                lse_tile = jnp.where(lane == p * (LANES // DIL_HD) + e, m + jnp.log(den), lse_tile)
            o_ref[0, 0, :, ls] = o_pair.astype(o_ref.dtype)
        lse_ref[0, 0] = lse_tile


def _dec_attn_seq(qkv_refs, c_refs, o_ref, t_new):
    nq = DIL_HEADS * t_new
    lane_w = lax.broadcasted_iota(jnp.int32, (nq, DIL_WIDTH), 1)
    row_w = lax.broadcasted_iota(jnp.int32, (nq, DIL_WIDTH), 0)
    own_head = _div(lane_w, DIL_HD) == _div(row_w, t_new)
    scores, values = [], []
    for g, (w, r) in enumerate(DIL_GROUPS):
        qkv = qkv_refs[g][0].astype(F32)
        q, k_new, v_new = qkv[:, :DIL_WIDTH], qkv[:, DIL_WIDTH:2 * DIL_WIDTH], qkv[:, 2 * DIL_WIDTH:]
        q_rep = jnp.broadcast_to(q[None], (DIL_HEADS, t_new, DIL_WIDTH)).reshape(nq, DIL_WIDTH)
        q_bd = jnp.where(own_head, q_rep, 0.0).astype(BF16)
        kt_c, vt_c = c_refs[g][0, 0].astype(BF16), c_refs[g][0, 1].astype(BF16)
        t_q = _mod(lax.broadcasted_iota(jnp.int32, (nq, w), 0), t_new)
        j = lax.broadcasted_iota(jnp.int32, (nq, w), 1)
        ok_c = (j >= t_q) & (_mod(w + t_q - j, r) == 0)
        scores.append(jnp.where(ok_c, _dot(q_bd, kt_c), NEG_FILL))
        values.append(vt_c)
        t_n = _mod(lax.broadcasted_iota(jnp.int32, (nq, LANES), 0), t_new)
        u = lax.broadcasted_iota(jnp.int32, (nq, LANES), 1)
        ok_n = (u <= t_n) & (_mod(t_n - u, r) == 0)
        scores.append(jnp.where(ok_n, _dot_nt(q_bd, _pad_rows(k_new, LANES).astype(BF16)), NEG_FILL))
        values.append(_pad_rows(v_new, LANES).astype(BF16))
    m = functools.reduce(jnp.maximum, [jnp.max(s, axis=-1, keepdims=True) for s in scores])
    den = jnp.zeros((nq, 1), F32)
    acc = jnp.zeros((nq, DIL_WIDTH), F32)
    for idx, (s, v) in enumerate(zip(scores, values)):
        pe = jnp.exp(s - m)
        den = den + jnp.sum(pe, axis=-1, keepdims=True)
        pe = pe.astype(BF16)
        acc = acc + (_dot(pe, v) if idx % 2 else _dot_nt(pe, v))
    acc = jnp.where(own_head, acc * (1.0 / den), 0.0)
    o_ref[0] = jnp.sum(acc.reshape(DIL_HEADS, t_new, DIL_WIDTH), axis=0)


def _attention_body(*refs, blocks_per_residue, n_blocks, t_new):
    n_p = 5 * N_GROUPS
    ins_p, qkv_s, c_s = refs[:n_p], refs[n_p:n_p + N_GROUPS], refs[n_p + N_GROUPS:n_p + 2 * N_GROUPS]
    outs_p, o_s = refs[n_p + 2 * N_GROUPS:n_p + 4 * N_GROUPS], refs[n_p + 4 * N_GROUPS]
    _dil_attn_block(ins_p, outs_p, _mod(pl.program_id(0), n_blocks), blocks_per_residue)
    _dec_attn_seq(qkv_s, c_s, o_s, t_new)


def _attention(qkv_p, qkv_s, caches, *, n_seq_p, seq_len, n_seq_s, t_new):
    args, in_specs, out_specs, out_shape, bpr = [], [], [], [], []
    n_blocks = seq_len // ATT_BLK
    assert n_seq_s == n_seq_p * n_blocks
    for g, (w, r) in enumerate(DIL_GROUPS):
        assert w // r == ATT_BLK and seq_len % (r * ATT_BLK) == 0
        nb = n_blocks // r
        bpr.append(nb)
        cur = lambda i, nb, part: (i // n_blocks, (i % n_blocks) // nb, i % nb, part)
        prev = lambda i, nb, part: (i // n_blocks, (i % n_blocks) // nb, jnp.maximum(i % nb - 1, 0), part)
        blk = (1, 1, ATT_BLK, DIL_WIDTH)
        for fn, part in ((cur, 0), (prev, 1), (cur, 1), (prev, 2), (cur, 2)):
            args.append(qkv_p[g])
            in_specs.append(pl.BlockSpec(blk, functools.partial(fn, nb=nb, part=part)))
        omap = functools.partial(cur, nb=nb, part=0)
        out_specs += [pl.BlockSpec(blk, omap), pl.BlockSpec((1, 1, ATT_BLK, LANES), omap)]
        out_shape += [jax.ShapeDtypeStruct((n_seq_p, r, seq_len // r, DIL_WIDTH), qkv_p[g].dtype),
                      jax.ShapeDtypeStruct((n_seq_p, r, seq_len // r, LANES), F32)]
    est = 2 * N_GROUPS * (6 * _nbytes((ATT_BLK, DIL_WIDTH), qkv_p[0].dtype) + _nbytes((ATT_BLK, LANES), F32))
    for g in range(N_GROUPS):
        args.append(qkv_s[g].reshape(n_seq_s, t_new, 3 * DIL_WIDTH))
        in_specs.append(pl.BlockSpec((1, t_new, 3 * DIL_WIDTH), lambda i: (i, 0, 0)))
    for g, (w, r) in enumerate(DIL_GROUPS):
        args.append(jnp.transpose(caches[g], (0, 2, 3, 4, 1)).reshape(n_seq_s, 2, DIL_WIDTH, w))
        in_specs.append(pl.BlockSpec((1, 2, DIL_WIDTH, w), lambda i: (i, 0, 0, 0)))
        est += 2 * _nbytes((2, DIL_WIDTH, w), F32) + 3 * _nbytes((DIL_HEADS * t_new + DIL_WIDTH, w), F32)
    out_specs.append(pl.BlockSpec((1, t_new, DIL_WIDTH), lambda i: (i, 0, 0)))
    out_shape.append(jax.ShapeDtypeStruct((n_seq_s, t_new, DIL_WIDTH), F32))
    outs = pl.pallas_call(
        functools.partial(_attention_body, blocks_per_residue=tuple(bpr), n_blocks=n_blocks, t_new=t_new),
        grid=(n_seq_s,),
        in_specs=in_specs, out_specs=out_specs, out_shape=out_shape,
        compiler_params=pltpu.CompilerParams(dimension_semantics=("arbitrary",), vmem_limit_bytes=_vmem_limit(est)),
        name="attention",
    )(*args)
    return ([outs[2 * g] for g in range(N_GROUPS)], [outs[2 * g + 1] for g in range(N_GROUPS)],
            outs[2 * N_GROUPS].reshape(n_seq_s * t_new, DIL_WIDTH))


def _row_order(src_ref, stage_ref, r):
    if r == 1:
        return src_ref[0, 0].astype(F32)
    n = src_ref.shape[2]
    n_slab = src_ref.shape[3] // LANES
    for c in range(r):
        for j in range(n_slab):
            stage_ref[j, pl.ds(c, n, stride=r), :] = src_ref[0, c, :, j * LANES:(j + 1) * LANES].astype(F32)
    slabs = [stage_ref[j] for j in range(n_slab)]
    return slabs[0] if n_slab == 1 else jnp.concatenate(slabs, axis=1)


def _merge_body(*refs, n_dil_inputs):
    x_ref, og_ref = refs[:2]
    dil_refs = refs[2:2 + n_dil_inputs]
    (wg_ref, wa_ref, wb_ref, wo_ref, ex_ref, g_ref, b_ref, y_ref) = refs[2 + n_dil_inputs:10 + n_dil_inputs]
    x = x_ref[...]
    xb = x.astype(BF16)
    if n_dil_inputs == 1:
        yb_in = dil_refs[0][...].astype(F32)
    else:
        ostage_ref, lstage_ref = refs[10 + n_dil_inputs:]
        stages = [(None, None)] + [(ostage_ref.at[g - 1], lstage_ref.at[g - 1]) for g in range(1, N_GROUPS)]
        os_ = [_row_order(dil_refs[g], stages[g][0], DIL_GROUPS[g][1]) for g in range(N_GROUPS)]
        lses = [_row_order(dil_refs[N_GROUPS + g], stages[g][1], DIL_GROUPS[g][1]) for g in range(N_GROUPS)]
        top = functools.reduce(jnp.maximum, lses)
        es = [jnp.exp(l - top) for l in lses]
        inv = 1.0 / functools.reduce(jnp.add, es)
        yb_in = jnp.zeros(os_[0].shape, F32)
        for o, e in zip(os_, es):
            wgt = e * inv
            hi = wgt.astype(BF16)
            lo = (wgt - hi.astype(F32)).astype(BF16)
            wide = _dot(hi, ex_ref[...]) + _dot(lo, ex_ref[...])
            yb_in = yb_in + wide * o
    ya = _dot(og_ref[...].astype(BF16), wa_ref[...])
    yb = _dot(yb_in.astype(BF16), wb_ref[...])
    ga = _dot(xb, wg_ref[:, :D_MODEL])
    gb = _dot(xb, wg_ref[:, D_MODEL:])
    mix = _sigmoid(ga) * ya + _sigmoid(gb) * yb
    z = ALPHA * x + _dot(mix.astype(BF16), wo_ref[...])
    y_ref[...] = _layer_norm(z, g_ref[...], b_ref[...])


def _merge(x2d, o_gla, dil_inputs, w_gates, w_br_gla, w_br_dil, w_out, expand, ln_g, ln_b, *, tm, seq_len):
    m = x2d.shape[0]
    row = lambda i: (i, 0)
    tiles_per_seq = max(seq_len // tm, 1)
    acts, act_specs, scratch = [x2d, o_gla], [pl.BlockSpec((tm, D_MODEL), row), pl.BlockSpec((tm, GLA_V), row)], []
    est = 2 * _nbytes((tm, D_MODEL), F32) + 2 * _nbytes((tm, GLA_V), o_gla.dtype)
    for a in dil_inputs:
        acts.append(a)
        if a.ndim == 2:
            act_specs.append(pl.BlockSpec((tm, a.shape[1]), row))
        else:
            r = a.shape[1]
            assert seq_len % tm == 0 and tm % r == 0
            act_specs.append(pl.BlockSpec((1, r, tm // r, a.shape[3]), lambda i: (i // tiles_per_seq, 0, i % tiles_per_seq, 0)))
        est += 2 * _nbytes((tm, a.shape[-1]), a.dtype)
    if len(dil_inputs) > 1:
        scratch = [pltpu.VMEM((N_GROUPS - 1, DIL_WIDTH // LANES, tm, LANES), F32),
                   pltpu.VMEM((N_GROUPS - 1, 1, tm, LANES), F32)]
        est += (N_GROUPS - 1) * _nbytes((tm, DIL_WIDTH + LANES), F32)
    weights = [w_gates, w_br_gla, w_br_dil, w_out, expand, ln_g, ln_b]
    est += sum(_nbytes(w.shape, w.dtype) for w in weights) + 6 * _nbytes((tm, D_MODEL), F32)
    return pl.pallas_call(
        functools.partial(_merge_body, n_dil_inputs=len(dil_inputs)),
        grid=(m // tm,),
        in_specs=act_specs + [_resident(w.shape) for w in weights],
        out_specs=pl.BlockSpec((tm, D_MODEL), row),
        out_shape=jax.ShapeDtypeStruct((m, D_MODEL), F32),
        scratch_shapes=scratch,
        compiler_params=pltpu.CompilerParams(dimension_semantics=("arbitrary",), vmem_limit_bytes=_vmem_limit(est)),
        name="merge",
    )(*acts, *weights)


def _ffn_body(*refs, tm, tiles_per_seq, seq_len, tail_rows):
    long_seq = seq_len >= tm
    if long_seq:
        x_ref, halo_ref, cp_ref, pe_ref = refs[:4]
    else:
        x_ref, cp0_ref, cp1_ref, pe_ref = refs[:4]
    (wup_ref, cw_ref, cb_ref, wdn_ref, g2_ref, b2_ref, wpg_ref, wpp_ref, g3_ref, b3_ref, y_ref, tail_ref) = refs[4:]
    i = pl.program_id(0)
    x = x_ref[...]
    xb = x.astype(BF16)
    row = lax.broadcasted_iota(jnp.int32, (tm, FF_CHUNK), 0)
    acc = jnp.zeros((tm, D_MODEL), F32)
    for c in range(D_FF // FF_CHUNK):
        cs = slice(c * FF_CHUNK, (c + 1) * FF_CHUNK)
        a = _dot(xb, wup_ref[:, cs])
        u = _dot(xb, wup_ref[:, D_FF + c * FF_CHUNK:D_FF + (c + 1) * FF_CHUNK])
        if long_seq:
            a_halo = _dot(halo_ref[...].astype(BF16), wup_ref[:, cs])
            prev = jnp.where(i % tiles_per_seq == 0, cp_ref[0][:, cs], a_halo)
            p1, p2 = prev[SUBLANES - 1:SUBLANES, :], prev[SUBLANES - 2:SUBLANES - 1, :]
            t = row
        else:
            p1, p2 = cp1_ref[:, cs], cp0_ref[:, cs]
            t = _mod(row, seq_len)
        am1 = jnp.where(t == 0, p1, pltpu.roll(a, 1, 0))
        am2 = jnp.where(t == 0, p2, jnp.where(t == 1, p1, pltpu.roll(a, 2, 0)))
        cw = cw_ref[:, cs]
        conv = cb_ref[:, cs] + cw[0:1, :] * am2 + cw[1:2, :] * am1 + cw[2:3, :] * a
        gelu = 0.5 * conv * (1.0 + lax.erf(conv * (2.0 ** -0.5)))
        acc = acc + _dot((gelu * u).astype(BF16), wdn_ref[cs, :])
        tail_ref[:, cs] = a[tm - tail_rows:, :]
    x2 = _layer_norm(ALPHA * x + acc, g2_ref[...], b2_ref[...])
    gate = _sigmoid(_dot(x2.astype(BF16), wpg_ref[...]))
    emb = _dot(pe_ref[...].astype(BF16), wpp_ref[...])
    y_ref[...] = _layer_norm(ALPHA * x2 + gate * emb, g3_ref[...], b3_ref[...])


def _ffn(x2d, conv_prev, pe2d, w_up, conv_w, conv_b, w_down, ln2_g, ln2_b, w_pg, w_pp, ln3_g, ln3_b, *, tm, seq_len):
    m = x2d.shape[0]
    n_seq = m // seq_len
    row = lambda i: (i, 0)
    n_tiles = m // tm
    if seq_len >= tm:
        tiles_per_seq, tail_rows = seq_len // tm, SUBLANES
        cp = jnp.concatenate([jnp.zeros((n_seq, SUBLANES - (CONV_W - 1), D_FF), F32), conv_prev], axis=1)
        acts = [x2d, x2d, cp, pe2d]
        act_specs = [pl.BlockSpec((tm, D_MODEL), row),
                     pl.BlockSpec((SUBLANES, D_MODEL), lambda i: (jnp.maximum(i * (tm // SUBLANES) - 1, 0), 0)),
                     pl.BlockSpec((1, SUBLANES, D_FF), lambda i: (i // tiles_per_seq, 0, 0)),
                     pl.BlockSpec((tm, PLE_DIM), row)]
        act_bytes = 2 * _nbytes((tm, D_MODEL), F32) + _nbytes((SUBLANES, D_FF), F32)
    else:
        assert tm % seq_len == 0 and seq_len >= CONV_W - 1
        tiles_per_seq, tail_rows = 1, tm
        cp0 = jnp.repeat(conv_prev[:, 0, :], seq_len, axis=0)
        cp1 = jnp.repeat(conv_prev[:, 1, :], seq_len, axis=0)
        acts = [x2d, cp0, cp1, pe2d]
        act_specs = [pl.BlockSpec((tm, D_MODEL), row), pl.BlockSpec((tm, D_FF), row), pl.BlockSpec((tm, D_FF), row),
                     pl.BlockSpec((tm, PLE_DIM), row)]
        act_bytes = _nbytes((tm, D_MODEL), F32) + 2 * _nbytes((tm, D_FF), F32)
    weights = [w_up, conv_w, conv_b, w_down, ln2_g, ln2_b, w_pg, w_pp, ln3_g, ln3_b]
    est = (2 * act_bytes + sum(_nbytes(w.shape, w.dtype) for w in weights) + 4 * _nbytes((tm, D_MODEL), F32)
           + 2 * _nbytes((tail_rows, D_FF), F32) + 6 * _nbytes((tm, FF_CHUNK), F32))
    y, tail = pl.pallas_call(
        functools.partial(_ffn_body, tm=tm, tiles_per_seq=tiles_per_seq, seq_len=seq_len, tail_rows=tail_rows),
        grid=(n_tiles,),
        in_specs=act_specs + [_resident(w.shape) for w in weights],
        out_specs=[pl.BlockSpec((tm, D_MODEL), row), pl.BlockSpec((tail_rows, D_FF), row)],
        out_shape=[jax.ShapeDtypeStruct((m, D_MODEL), F32), jax.ShapeDtypeStruct((n_tiles * tail_rows, D_FF), F32)],
        compiler_params=pltpu.CompilerParams(dimension_semantics=("arbitrary",), vmem_limit_bytes=_vmem_limit(est)),
        name="ffn",
    )(*acts, *weights)
    return y, tail


def _tile_rows(m):
    return min(m, 512)


def _mixers(x, s0, pos_offset, wts, *, prompt):
    n_seq, t, _ = x.shape
    m = n_seq * t
    x2d = x.reshape(m, D_MODEL)
    act_dtype = BF16 if prompt else F32
    tm = _tile_rows(m)
    q, k, v, g, ld = _gla_proj(x2d, wts["w_gla"], wts["w_lr"], wts["w_gkb"], wts["b_gk"], tm=tm, act_dtype=act_dtype)
    qkv, kv_new = _dil_proj(x2d, wts["w_dil"], wts["inv_freq"], tm=tm, seq_len=t, pos_offset=pos_offset, act_dtype=act_dtype)
    r3 = lambda a: a.reshape(n_seq, t, a.shape[-1])
    o_gla, s_new = _gla_scan(r3(q), r3(k), r3(v), r3(g), r3(ld), s0, wts["gla_norm"],
                             nseq=n_seq if prompt else 8, cc=min(t, GLA_CHUNK))
    if prompt:
        kv_out = [jnp.transpose(kvn.reshape(n_seq, 2, DIL_HEADS, DIL_HD, -1), (0, 4, 1, 2, 3)) for kvn in kv_new]
    else:
        kv_out = [kvn.reshape(n_seq, -1, 2, DIL_HEADS, DIL_HD) for kvn in kv_new]
    return dict(x2d=x2d, tm=tm, n_seq=n_seq, t=t, o_gla=o_gla.reshape(m, GLA_V), s_new=s_new, qkv=qkv, kv_out=kv_out)


def _finish(mix, dil_inputs, pe, conv_prev, wts, *, prompt):
    n_seq, t, tm = mix["n_seq"], mix["t"], mix["tm"]
    m = n_seq * t
    x1 = _merge(mix["x2d"], mix["o_gla"], dil_inputs, wts["w_gates"], wts["w_br_gla"], wts["w_br_dil"], wts["w_out"],
                wts["expand"], wts["ln1_g"], wts["ln1_b"], tm=tm, seq_len=t)
    tm_ffn = tm if prompt else min(tm, 256)
    y, tail = _ffn(x1, conv_prev, pe.reshape(m, PLE_DIM), wts["w_up"], wts["conv_w"], wts["conv_b"], wts["w_down"],
                   wts["ln2_g"], wts["ln2_b"], wts["w_ple_gate"], wts["w_ple_proj"], wts["ln3_g"], wts["ln3_b"],
                   tm=tm_ffn, seq_len=t)
    if t >= tm_ffn:
        conv_new = tail.reshape(n_seq, t // tm_ffn, SUBLANES, D_FF)[:, -1, SUBLANES - (CONV_W - 1):, :]
    else:
        conv_new = tail.reshape(n_seq, t, D_FF)[:, t - (CONV_W - 1):, :]
    return y.reshape(n_seq, t, D_MODEL), conv_new


def _forward(x_p, x_s, pe_p, pe_s, s0_s, conv_s, caches, wts):
    n_p = x_p.shape[0]
    mix_p = _mixers(x_p, jnp.zeros((n_p, GLA_HEADS, GLA_DK, GLA_DV), s0_s.dtype), 0, wts, prompt=True)
    mix_s = _mixers(x_s, s0_s, PAST_LEN, wts, prompt=False)
    o_p, lse_p, o_s = _attention(mix_p["qkv"], mix_s["qkv"], caches, n_seq_p=n_p, seq_len=x_p.shape[1],
                                 n_seq_s=x_s.shape[0], t_new=x_s.shape[1])
    y_p, conv_new_p = _finish(mix_p, o_p + lse_p, pe_p, jnp.zeros((n_p, CONV_W - 1, D_FF), x_p.dtype), wts, prompt=True)
    y_s, conv_new_s = _finish(mix_s, [o_s], pe_s, conv_s, wts, prompt=False)
    return (y_p, mix_p["s_new"], conv_new_p, mix_p["kv_out"]), (y_s, mix_s["s_new"], conv_new_s, mix_s["kv_out"])


def _prep_weights(w_in, w_gk_b, b_gk, gla_norm, w_br_gla, w_br_dil, w_out, ln1_g, ln1_b, w_up, conv_w, conv_b, w_down,
                  ln2_g, ln2_b, w_ple_gate, w_ple_proj, ln3_g, ln3_b):
    w_in_b = w_in[0].astype(BF16)
    vec = lambda a: a[0].reshape(1, -1).astype(F32)
    half = DIL_HD // 2
    inv = ROPE_THETA ** (-jnp.arange(half, dtype=F32) / half)
    head_of_lane = jnp.arange(DIL_WIDTH) // DIL_HD
    expand = (jnp.arange(LANES)[:, None] == head_of_lane[None, :]).astype(BF16)
    return {
        "w_gla": w_in_b[:, :COL_GLA],
        "w_lr": jnp.pad(w_in_b[:, COL_GLA:COL_LR], ((0, 0), (0, LANES - GLA_RANK))),
        "w_gkb": jnp.pad(w_gk_b[0].astype(BF16), ((0, LANES - GLA_RANK), (0, 0))),
        "b_gk": vec(b_gk),
        "w_dil": w_in_b[:, COL_LR:COL_DIL],
        "w_gates": w_in_b[:, COL_DIL:],
        "inv_freq": jnp.tile(inv, LANES // half).reshape(1, LANES),
        "gla_norm": vec(gla_norm),
        "w_br_gla": w_br_gla[0].astype(BF16), "w_br_dil": w_br_dil[0].astype(BF16), "w_out": w_out[0].astype(BF16),
        "expand": expand,
        "ln1_g": vec(ln1_g), "ln1_b": vec(ln1_b),
        "w_up": w_up[0].astype(BF16),
        "conv_w": jnp.pad(conv_w[0].astype(F32), ((0, SUBLANES - CONV_W), (0, 0))),
        "conv_b": vec(conv_b),
        "w_down": w_down[0].astype(BF16),
        "ln2_g": vec(ln2_g), "ln2_b": vec(ln2_b),
        "w_ple_gate": w_ple_gate[0].astype(BF16), "w_ple_proj": w_ple_proj[0].astype(BF16),
        "ln3_g": vec(ln3_g), "ln3_b": vec(ln3_b),
    }


def kernel(x_prompt, x_sample, p_prompt, p_sample, state_gla, cache_conv, cache_kv_w128, cache_kv_w512, cache_kv_w2048, w_in, w_gk_b, b_gk, gla_norm, w_br_gla, w_br_dil, w_out, ln1_g, ln1_b, w_up, conv_w, conv_b, w_down, ln2_g, ln2_b, w_ple_gate, w_ple_proj, ln3_g, ln3_b):
    assert w_in.shape[0] == DEPTH == 1
    wts = _prep_weights(w_in, w_gk_b, b_gk, gla_norm, w_br_gla, w_br_dil, w_out, ln1_g, ln1_b, w_up, conv_w, conv_b,
                        w_down, ln2_g, ln2_b, w_ple_gate, w_ple_proj, ln3_g, ln3_b)
    caches = (cache_kv_w128[0], cache_kv_w512[0], cache_kv_w2048[0])
    (y_p, s_p, conv_p, kv_p), (y_s, s_s, conv_s, kv_s) = _forward(
        x_prompt, x_sample, p_prompt[0], p_sample[0], state_gla[0], cache_conv[0], caches, wts)
    return (y_p, y_s, s_p[None], s_s[None], conv_p[None], conv_s[None],
            kv_p[0][None], kv_p[1][None], kv_p[2][None], kv_s[0][None], kv_s[1][None], kv_s[2][None])
```

```python
import functools

import jax
import jax.numpy as jnp
from jax import lax
from jax.experimental import pallas as pl
from jax.experimental.pallas import tpu as pltpu

F32 = jnp.float32
BF16 = jnp.bfloat16

D_MODEL = 1024
DEPTH = 1
PAST_LEN = 8192
GLA_HEADS = 4
GLA_DK = 128
GLA_DV = 256
GLA_QK = GLA_HEADS * GLA_DK
GLA_V = GLA_HEADS * GLA_DV
GLA_RANK = 16
GLA_NORMALIZER = 16.0
DIL_GROUPS = ((128, 1), (512, 4), (2048, 16))
N_GROUPS = len(DIL_GROUPS)
DIL_HEADS = 8
DIL_HD = 64
DIL_WIDTH = DIL_HEADS * DIL_HD
DIL_QKV = N_GROUPS * DIL_WIDTH
ROPE_THETA = 10000.0
D_FF = 2816
CONV_W = 3
PLE_DIM = 256
ALPHA = (2.0 * DEPTH) ** 0.25
NORM_EPS = 1e-5
COL_GLA = 2 * GLA_QK + 2 * GLA_V
COL_LR = COL_GLA + GLA_RANK
COL_DIL = COL_LR + 3 * DIL_QKV
NEG_FILL = -1e30

LANES = 128
SUBLANES = 8
VMEM_BYTES_V7X = 64 * 1024 * 1024
VMEM_LIMIT_CAP = 60000 * 1024

ATT_BLK = 128
GLA_CHUNK = 128
GLA_SUB = 16
FF_CHUNK = D_FF // 2
CACHE_RING = 3


def _vmem_limit(nbytes):
    return int(min(max(2 * nbytes, 16 * 1024 * 1024), VMEM_LIMIT_CAP))


def _nbytes(shape, dtype):
    n = 1
    for s in shape:
        n *= s
    return n * jnp.dtype(dtype).itemsize


def _sigmoid(x):
    return 1.0 / (1.0 + jnp.exp(-x))


def _log_sigmoid(x):
    return jnp.minimum(x, 0.0) - jnp.log1p(jnp.exp(-jnp.abs(x)))


def _layer_norm(z, g, b):
    mu = jnp.mean(z, axis=-1, keepdims=True)
    d = z - mu
    var = jnp.mean(d * d, axis=-1, keepdims=True)
    return d * lax.rsqrt(var + NORM_EPS) * g + b


def _log2(n):
    assert n > 0 and n & (n - 1) == 0, n
    return n.bit_length() - 1


def _div(x, n):
    return x >> _log2(n)


def _mod(x, n):
    _log2(n)
    return x & (n - 1)


def _resident(shape):
    return pl.BlockSpec(shape, lambda *_: (0,) * len(shape), pipeline_mode=pl.Buffered(1))


def _dot(a, b):
    return jnp.dot(a, b, preferred_element_type=F32)


def _dot_nt(a, b):
    return lax.dot_general(a, b, (((1,), (1,)), ((), ())), preferred_element_type=F32)


def _gla_proj_body(x_ref, w_ref, wlr_ref, wgkb_ref, bgk_ref, q_ref, k_ref, v_ref, g_ref, ld_ref):
    xb = x_ref[...].astype(BF16)
    cw = GLA_QK

    def proj(lo):
        return _dot_nt(xb, w_ref[lo:lo + cw, :])

    q_ref[...] = (proj(0) * GLA_DK ** -0.5).astype(q_ref.dtype)
    k_ref[...] = proj(GLA_QK).astype(k_ref.dtype)
    for c in range(GLA_V // cw):
        v_ref[:, c * cw:(c + 1) * cw] = proj(2 * GLA_QK + c * cw).astype(v_ref.dtype)
        gg = proj(2 * GLA_QK + GLA_V + c * cw)
        g_ref[:, c * cw:(c + 1) * cw] = (gg * _sigmoid(gg)).astype(g_ref.dtype)
    glr = _dot_nt(xb, wlr_ref[...])
    z = _dot(glr.astype(BF16), wgkb_ref[...]) + bgk_ref[...]
    ld_ref[...] = _log_sigmoid(z) * (1.0 / GLA_NORMALIZER)


def _gla_proj(x2d, w_gla, w_lr, w_gkb, b_gk, *, tm, act_dtype):
    m = x2d.shape[0]
    row = lambda i: (i, 0)
    out_shape = [
        jax.ShapeDtypeStruct((m, GLA_QK), act_dtype), jax.ShapeDtypeStruct((m, GLA_QK), act_dtype),
        jax.ShapeDtypeStruct((m, GLA_V), act_dtype), jax.ShapeDtypeStruct((m, GLA_V), act_dtype),
        jax.ShapeDtypeStruct((m, GLA_QK), F32),
    ]
    weights = [w_gla, w_lr, w_gkb, b_gk]
    est = (2 * _nbytes((tm, D_MODEL), F32) + sum(_nbytes(w.shape, w.dtype) for w in weights)
           + 2 * sum(_nbytes((tm, s.shape[1]), s.dtype) for s in out_shape) + 2 * _nbytes((tm, GLA_QK), F32))
    return pl.pallas_call(
        _gla_proj_body,
        grid=(m // tm,),
        in_specs=[pl.BlockSpec((tm, D_MODEL), row)] + [_resident(w.shape) for w in weights],
        out_specs=[pl.BlockSpec((tm, s.shape[1]), row) for s in out_shape],
        out_shape=out_shape,
        compiler_params=pltpu.CompilerParams(dimension_semantics=("arbitrary",), vmem_limit_bytes=_vmem_limit(est)),
        name="gla_proj",
    )(x2d, w_gla, w_lr, w_gkb, b_gk)


def _dil_proj_body(x_ref, w_ref, inv_ref, *refs, tm, seq_len, pos_offset, regroup):
    qkv_refs, kv_refs = refs[:N_GROUPS], refs[N_GROUPS:2 * N_GROUPS]
    trig_ref = refs[2 * N_GROUPS]
    stage_ref = refs[2 * N_GROUPS + 1] if regroup else None
    i = pl.program_id(0)
    xb = x_ref[...].astype(BF16)
    half = DIL_HD // 2
    tiles_per_seq = max(seq_len // tm, 1)
    it = _mod(i, tiles_per_seq)
    lane = lax.broadcasted_iota(jnp.int32, (tm, LANES), 1)

    @pl.when(i == 0)
    def _():
        rel = _mod(lax.broadcasted_iota(jnp.int32, (tm, LANES), 0), seq_len).astype(F32) * inv_ref[...]
        trig_ref[0] = jnp.cos(rel)
        trig_ref[1] = jnp.sin(rel)

    base = (pos_offset + it * tm).astype(F32) * inv_ref[...]
    cos_b, sin_b = jnp.cos(base), jnp.sin(base)
    cos = cos_b * trig_ref[0] - sin_b * trig_ref[1]
    sin = sin_b * trig_ref[0] + cos_b * trig_ref[1]
    first_half = _mod(lane, DIL_HD) < half
    sin_signed = jnp.where(first_half, -sin, sin)

    def rope(xc):
        partner = jnp.where(first_half, pltpu.roll(xc, LANES - half, 1), pltpu.roll(xc, half, 1))
        return xc * cos + partner * sin_signed

    n_slab = DIL_WIDTH // LANES
    for g, (w, r) in enumerate(DIL_GROUPS):
        hq = _dot_nt(xb, w_ref[g * DIL_WIDTH:(g + 1) * DIL_WIDTH, :])
        hk = _dot_nt(xb, w_ref[DIL_QKV + g * DIL_WIDTH:DIL_QKV + (g + 1) * DIL_WIDTH, :])
        hv = _dot_nt(xb, w_ref[2 * DIL_QKV + g * DIL_WIDTH:2 * DIL_QKV + (g + 1) * DIL_WIDTH, :])
        slabs = []
        for part, h in enumerate((hq, hk, hv)):
            for c in range(n_slab):
                xc = h[:, c * LANES:(c + 1) * LANES]
                slabs.append(xc if part == 2 else rope(xc) * (DIL_HD ** -0.5 if part == 0 else 1.0))
        if not regroup:
            for j, xc in enumerate(slabs):
                qkv_refs[g][:, j * LANES:(j + 1) * LANES] = xc.astype(qkv_refs[g].dtype)
                if j >= n_slab:
                    kv_refs[g][:, (j - n_slab) * LANES:(j - n_slab + 1) * LANES] = xc
            continue
        for j, xc in enumerate(slabs):
            if r == 1:
                qkv_refs[g][0, 0, :, j * LANES:(j + 1) * LANES] = xc.astype(qkv_refs[g].dtype)
            else:
                stage_ref[g - 1, j] = xc
        for c in range(r if r > 1 else 0):
            for j in range(3 * n_slab):
                qkv_refs[g][0, c, :, j * LANES:(j + 1) * LANES] = (
                    stage_ref[g - 1, j, pl.ds(c, tm // r, stride=r), :].astype(qkv_refs[g].dtype))
        keep = min(w, tm)

        @pl.when(it >= tiles_per_seq - w // keep)
        def _(g=g, keep=keep, slabs=slabs):
            for j in range(2 * n_slab):
                kv_refs[g][0, j // n_slab, (j % n_slab) * LANES:(j % n_slab + 1) * LANES, :] = (
                    slabs[n_slab + j][tm - keep:, :].T)


def _dil_proj(x2d, w_dil, inv_freq, *, tm, seq_len, pos_offset, act_dtype):
    m = x2d.shape[0]
    n_seq = m // seq_len
    regroup = seq_len >= tm
    row = lambda i: (i, 0)
    tiles_per_seq = max(seq_len // tm, 1)
    qkv_shapes, qkv_specs, kv_shapes, kv_specs, scratch = [], [], [], [], []
    for w, r in DIL_GROUPS:
        if regroup:
            assert tm % r == 0 and seq_len % tm == 0
            keep = min(w, tm)
            nblk = w // keep
            qkv_shapes.append(jax.ShapeDtypeStruct((n_seq, r, seq_len // r, 3 * DIL_WIDTH), act_dtype))
            qkv_specs.append(pl.BlockSpec((1, r, tm // r, 3 * DIL_WIDTH),
                                          lambda i: (i // tiles_per_seq, 0, i % tiles_per_seq, 0)))
            kv_shapes.append(jax.ShapeDtypeStruct((n_seq, 2, DIL_WIDTH, w), F32))
            kv_specs.append(pl.BlockSpec(
                (1, 2, DIL_WIDTH, keep),
                functools.partial(lambda i, nblk: (i // tiles_per_seq, 0, 0,
                                                   jnp.maximum(i % tiles_per_seq - (tiles_per_seq - nblk), 0)), nblk=nblk)))
        else:
            assert w >= seq_len
            qkv_shapes.append(jax.ShapeDtypeStruct((m, 3 * DIL_WIDTH), act_dtype))
            qkv_specs.append(pl.BlockSpec((tm, 3 * DIL_WIDTH), row))
            kv_shapes.append(jax.ShapeDtypeStruct((m, 2 * DIL_WIDTH), F32))
            kv_specs.append(pl.BlockSpec((tm, 2 * DIL_WIDTH), row))
    est = (2 * _nbytes((tm, D_MODEL), F32) + _nbytes(w_dil.shape, BF16)
           + 2 * N_GROUPS * _nbytes((tm, 3 * DIL_WIDTH), act_dtype)
           + 2 * sum(_nbytes(s.block_shape, F32) for s in kv_specs) + 3 * _nbytes((tm, DIL_WIDTH), F32))
    scratch.append(pltpu.VMEM((2, tm, LANES), F32))
    if regroup:
        assert DIL_GROUPS[0][1] == 1 and all(r > 1 for _, r in DIL_GROUPS[1:])
        scratch.append(pltpu.VMEM((N_GROUPS - 1, 3 * DIL_WIDTH // LANES, tm, LANES), F32))
        est += _nbytes((N_GROUPS - 1, tm, 3 * DIL_WIDTH), F32)
    body = functools.partial(_dil_proj_body, tm=tm, seq_len=seq_len, pos_offset=pos_offset, regroup=regroup)
    outs = pl.pallas_call(
        body,
        grid=(m // tm,),
        in_specs=[pl.BlockSpec((tm, D_MODEL), row), _resident(w_dil.shape), _resident(inv_freq.shape)],
        out_specs=qkv_specs + kv_specs,
        out_shape=qkv_shapes + kv_shapes,
        scratch_shapes=scratch,
        compiler_params=pltpu.CompilerParams(dimension_semantics=("arbitrary",), vmem_limit_bytes=_vmem_limit(est)),
        name="dil_proj",
    )(x2d, w_dil, inv_freq)
    return outs[:N_GROUPS], outs[N_GROUPS:]


def _cumsum_rows(x):
    n = x.shape[0]
    row = lax.broadcasted_iota(jnp.int32, x.shape, 0)
    s = 1
    while s < n:
        x = x + jnp.where(row >= s, pltpu.roll(x, s, 0), 0.0)
        s *= 2
    return x


def _pad_rows(x, n):
    if x.shape[0] == n:
        return x
    return jnp.concatenate([x, jnp.zeros((n - x.shape[0], x.shape[1]), x.dtype)], axis=0)


def _gla_scan_body(q_ref, k_ref, v_ref, g_ref, ld_ref, s0_ref, nw_ref, o_ref, sout_ref, *scratch, nseq, cc, carry):
    chunk = pl.program_id(1)
    sub = GLA_SUB
    cp = cc if cc % sub == 0 else sub * (cc // sub + 1)
    nsub = cp // sub
    st_ref = scratch[0] if carry else None

    if carry:
        @pl.when(chunk == 0)
        def _():
            for s in range(nseq):
                for h in range(GLA_HEADS):
                    st_ref[s * GLA_HEADS + h] = s0_ref[s, h].T

    row = lax.broadcasted_iota(jnp.int32, (cp, GLA_QK), 0)
    arow = lax.broadcasted_iota(jnp.int32, (cp, cp), 0)
    acol = lax.broadcasted_iota(jnp.int32, (cp, cp), 1)
    for s in range(nseq):
        q = _pad_rows(q_ref[s].astype(F32), cp)
        k = _pad_rows(k_ref[s].astype(F32), cp)
        b = _cumsum_rows(_pad_rows(ld_ref[s], cp))
        starts = [jnp.zeros((1, GLA_QK), F32)] + [b[i * sub - 1:i * sub, :] for i in range(1, nsub)]
        r = jnp.concatenate([jnp.broadcast_to(st, (sub, GLA_QK)) for st in starts], axis=0)
        qh = q * jnp.exp(b - r)
        qt = (qh * jnp.exp(r)).astype(BF16)
        b_last = b[cp - 1:cp, :]
        kl32 = k * jnp.exp(b_last - b)
        kl = kl32.astype(BF16)
        a_rows = [[] for _ in range(GLA_HEADS)]
        for i in range(nsub):
            e = jnp.where(row < (i + 1) * sub, starts[i] - b, NEG_FILL)
            ki = (k * jnp.exp(e)).astype(BF16)
            qi = qh[i * sub:(i + 1) * sub, :].astype(BF16)
            for h in range(GLA_HEADS):
                hs = slice(h * GLA_DK, (h + 1) * GLA_DK)
                a_rows[h].append(_dot_nt(qi[:, hs], ki[:, hs]))
        d_last = jnp.exp(b_last)
        for h in range(GLA_HEADS):
            hs = slice(h * GLA_DK, (h + 1) * GLA_DK)
            vs = slice(h * GLA_DV, (h + 1) * GLA_DV)
            a = jnp.where(acol <= arow, jnp.concatenate(a_rows[h], axis=0), 0.0).astype(BF16)
            v = _pad_rows(v_ref[s][:, vs].astype(F32), cp)
            if carry:
                st = st_ref[s * GLA_HEADS + h]
                inter = _dot_nt(qt[:, hs], st.astype(BF16))
            else:
                st = s0_ref[s, h]
                inter = _dot(qt[:, hs], st.astype(BF16))
            o = (inter + _dot(a, v.astype(BF16)))[:cc]
            o = o * lax.rsqrt(jnp.mean(o * o, axis=-1, keepdims=True) + NORM_EPS) * nw_ref[...]
            o_ref[s, :, vs] = (o * g_ref[s][:, vs].astype(F32)).astype(o_ref.dtype)
            v_tile = _pad_rows(v, GLA_CHUNK)
            if carry:
                st_ref[s * GLA_HEADS + h] = d_last[:, hs] * st + _dot(v_tile.T.astype(BF16), _pad_rows(kl[:, hs], GLA_CHUNK))
            else:
                aug = jnp.concatenate([kl32[:, hs], jnp.broadcast_to(d_last[:, hs], (SUBLANES, GLA_DK)),
                                       jnp.zeros((GLA_CHUNK - cp - SUBLANES, GLA_DK), F32)], axis=0).T
                sout_ref[s, h] = aug[:, cp:cp + 1] * st + _dot(aug.astype(BF16), v_tile.astype(BF16))

    if carry:
        @pl.when(chunk == pl.num_programs(1) - 1)
        def _():
            for s in range(nseq):
                for h in range(GLA_HEADS):
                    sout_ref[s, h] = st_ref[s * GLA_HEADS + h].T


def _gla_scan(q, k, v, g, ld, s0, norm_w, *, nseq, cc):
    n_seq, t, _ = q.shape
    assert n_seq % nseq == 0 and t % cc == 0 and cc <= GLA_CHUNK
    blk = lambda sg, c: (sg, c, 0)
    sblk = lambda sg, c: (sg, 0, 0, 0)
    carry = not (t == cc and cc + GLA_SUB + SUBLANES <= GLA_CHUNK)
    body = functools.partial(_gla_scan_body, nseq=nseq, cc=cc, carry=carry)
    est = (2 * nseq * cc * (2 * GLA_QK + 2 * GLA_V) * jnp.dtype(q.dtype).itemsize + 2 * nseq * cc * GLA_QK * 4
           + 2 * nseq * cc * GLA_V * jnp.dtype(q.dtype).itemsize + 5 * nseq * GLA_HEADS * GLA_DK * GLA_DV * 4)
    return pl.pallas_call(
        body,
        grid=(n_seq // nseq, t // cc),
        in_specs=[pl.BlockSpec((nseq, cc, GLA_QK), blk), pl.BlockSpec((nseq, cc, GLA_QK), blk),
                  pl.BlockSpec((nseq, cc, GLA_V), blk), pl.BlockSpec((nseq, cc, GLA_V), blk),
                  pl.BlockSpec((nseq, cc, GLA_QK), blk),
                  pl.BlockSpec((nseq, GLA_HEADS, GLA_DK, GLA_DV), sblk),
                  pl.BlockSpec(norm_w.shape, lambda sg, c: (0, 0))],
        out_specs=[pl.BlockSpec((nseq, cc, GLA_V), blk), pl.BlockSpec((nseq, GLA_HEADS, GLA_DK, GLA_DV), sblk)],
        out_shape=[jax.ShapeDtypeStruct((n_seq, t, GLA_V), q.dtype),
                   jax.ShapeDtypeStruct((n_seq, GLA_HEADS, GLA_DK, GLA_DV), F32)],
        scratch_shapes=[pltpu.VMEM((nseq * GLA_HEADS, GLA_DV, GLA_DK), F32)] if carry else [],
        compiler_params=pltpu.CompilerParams(dimension_semantics=("arbitrary", "arbitrary"),
                                             vmem_limit_bytes=_vmem_limit(est)),
        name="gla_scan",
    )(q, k, v, g, ld, s0, norm_w)


def _dil_attn_block(ins, outs, n, blocks_per_residue):
    row = lax.broadcasted_iota(jnp.int32, (ATT_BLK, 2 * ATT_BLK), 0)
    col = lax.broadcasted_iota(jnp.int32, (ATT_BLK, 2 * ATT_BLK), 1)
    dist = row + ATT_BLK - col
    band = (dist >= 0) & (dist <= ATT_BLK)
    lane = lax.broadcasted_iota(jnp.int32, (ATT_BLK, LANES), 1)
    for g in range(N_GROUPS):
        q_ref, kp_ref, kc_ref, vp_ref, vc_ref = ins[5 * g:5 * g + 5]
        o_ref, lse_ref = outs[2 * g:2 * g + 2]
        jb = _mod(n, blocks_per_residue[g])
        valid = band & ((col >= ATT_BLK) | (jb > 0))
        q = q_ref[0, 0]
        kk = jnp.concatenate([kp_ref[0, 0], kc_ref[0, 0]], axis=0)
        vv = jnp.concatenate([vp_ref[0, 0], vc_ref[0, 0]], axis=0)
        lse_tile = jnp.zeros((ATT_BLK, LANES), F32)
        for p in range(DIL_WIDTH // LANES):
            ls = slice(p * LANES, (p + 1) * LANES)
            q2, k2, v2 = q[:, ls], kk[:, ls], vv[:, ls]
            o_pair = jnp.zeros((ATT_BLK, LANES), F32)
            for e in range(LANES // DIL_HD):
                sel = (lane >= DIL_HD) if e else (lane < DIL_HD)
                s = _dot_nt(jnp.where(sel, q2, jnp.zeros_like(q2)), k2)
                s = jnp.where(valid, s, NEG_FILL)
                m = jnp.max(s, axis=-1, keepdims=True)
                pe = jnp.exp(s - m)
                den = jnp.sum(pe, axis=-1, keepdims=True)
                pv = _dot(pe.astype(BF16), v2)
                o_pair = jnp.where(sel, pv * (1.0 / den), o_pair)
                lse_tile = jnp.where(lane == p * (LANES // DIL_HD) + e, m + jnp.log(den), lse_tile)
            o_ref[0, 0, :, ls] = o_pair.astype(o_ref.dtype)
        lse_ref[0, 0] = lse_tile


def _dec_attn_seq(qkv_refs, c_refs, o_ref, t_new):
    nq = DIL_HEADS * t_new
    lane_w = lax.broadcasted_iota(jnp.int32, (nq, DIL_WIDTH), 1)
    row_w = lax.broadcasted_iota(jnp.int32, (nq, DIL_WIDTH), 0)
    own_head = _div(lane_w, DIL_HD) == _div(row_w, t_new)
    scores, values = [], []
    for g, (w, r) in enumerate(DIL_GROUPS):
        qkv = qkv_refs[g][0].astype(F32)
        q, k_new, v_new = qkv[:, :DIL_WIDTH], qkv[:, DIL_WIDTH:2 * DIL_WIDTH], qkv[:, 2 * DIL_WIDTH:]
        q_rep = jnp.broadcast_to(q[None], (DIL_HEADS, t_new, DIL_WIDTH)).reshape(nq, DIL_WIDTH)
        q_bd = jnp.where(own_head, q_rep, 0.0).astype(BF16)
        kt_c, vt_c = c_refs[g][0].astype(BF16), c_refs[g][1].astype(BF16)
        t_q = _mod(lax.broadcasted_iota(jnp.int32, (nq, w), 0), t_new)
        j = lax.broadcasted_iota(jnp.int32, (nq, w), 1)
        ok_c = (j >= t_q) & (_mod(w + t_q - j, r) == 0)
        scores.append(jnp.where(ok_c, _dot(q_bd, kt_c), NEG_FILL))
        values.append(vt_c)
        t_n = _mod(lax.broadcasted_iota(jnp.int32, (nq, LANES), 0), t_new)
        u = lax.broadcasted_iota(jnp.int32, (nq, LANES), 1)
        ok_n = (u <= t_n) & (_mod(t_n - u, r) == 0)
        scores.append(jnp.where(ok_n, _dot_nt(q_bd, _pad_rows(k_new, LANES).astype(BF16)), NEG_FILL))
        values.append(_pad_rows(v_new, LANES).astype(BF16))
    m = functools.reduce(jnp.maximum, [jnp.max(s, axis=-1, keepdims=True) for s in scores])
    den = jnp.zeros((nq, 1), F32)
    acc = jnp.zeros((nq, DIL_WIDTH), F32)
    for idx, (s, v) in enumerate(zip(scores, values)):
        pe = jnp.exp(s - m)
        den = den + jnp.sum(pe, axis=-1, keepdims=True)
        pe = pe.astype(BF16)
        acc = acc + (_dot(pe, v) if idx % 2 else _dot_nt(pe, v))
    acc = jnp.where(own_head, acc * (1.0 / den), 0.0)
    o_ref[0] = jnp.sum(acc.reshape(DIL_HEADS, t_new, DIL_WIDTH), axis=0)


def _cache_copy(c_hbm, buf, sem, g, seq, slot):
    return pltpu.make_async_copy(c_hbm[g].at[seq], buf[g].at[slot], sem.at[g, slot])


def _attention_body(*refs, blocks_per_residue, n_blocks, t_new):
    n_p = 5 * N_GROUPS
    ins_p, qkv_s, c_hbm = refs[:n_p], refs[n_p:n_p + N_GROUPS], refs[n_p + N_GROUPS:n_p + 2 * N_GROUPS]
    outs_p, o_s = refs[n_p + 2 * N_GROUPS:n_p + 4 * N_GROUPS], refs[n_p + 4 * N_GROUPS]
    bufs, sem = refs[n_p + 4 * N_GROUPS + 1:n_p + 5 * N_GROUPS + 1], refs[n_p + 5 * N_GROUPS + 1]
    i, n_steps = pl.program_id(0), pl.num_programs(0)

    @pl.when(i == 0)
    def _():
        for s in range(CACHE_RING - 1):
            for g in range(N_GROUPS):
                _cache_copy(c_hbm, bufs, sem, g, s, s).start()

    ahead = i + (CACHE_RING - 1)

    @pl.when(ahead < n_steps)
    def _():
        for g in range(N_GROUPS):
            _cache_copy(c_hbm, bufs, sem, g, ahead, ahead % CACHE_RING).start()

    _dil_attn_block(ins_p, outs_p, _mod(i, n_blocks), blocks_per_residue)
    slot = i % CACHE_RING
    for g in range(N_GROUPS):
        _cache_copy(c_hbm, bufs, sem, g, i, slot).wait()
    _dec_attn_seq(qkv_s, [bufs[g].at[slot] for g in range(N_GROUPS)], o_s, t_new)


def _attention(qkv_p, qkv_s, caches, *, n_seq_p, seq_len, n_seq_s, t_new):
    args, in_specs, out_specs, out_shape, bpr, scratch = [], [], [], [], [], []
    n_blocks = seq_len // ATT_BLK
    assert n_seq_s == n_seq_p * n_blocks
    assert n_seq_s >= CACHE_RING
    for g, (w, r) in enumerate(DIL_GROUPS):
        assert w // r == ATT_BLK and seq_len % (r * ATT_BLK) == 0
        nb = n_blocks // r
        bpr.append(nb)
        cur = lambda i, nb, part: (i // n_blocks, (i % n_blocks) // nb, i % nb, part)
        prev = lambda i, nb, part: (i // n_blocks, (i % n_blocks) // nb, jnp.maximum(i % nb - 1, 0), part)
        blk = (1, 1, ATT_BLK, DIL_WIDTH)
        for fn, part in ((cur, 0), (prev, 1), (cur, 1), (prev, 2), (cur, 2)):
            args.append(qkv_p[g])
            in_specs.append(pl.BlockSpec(blk, functools.partial(fn, nb=nb, part=part)))
        omap = functools.partial(cur, nb=nb, part=0)
        out_specs += [pl.BlockSpec(blk, omap), pl.BlockSpec((1, 1, ATT_BLK, LANES), omap)]
        out_shape += [jax.ShapeDtypeStruct((n_seq_p, r, seq_len // r, DIL_WIDTH), qkv_p[g].dtype),
                      jax.ShapeDtypeStruct((n_seq_p, r, seq_len // r, LANES), F32)]
    est = 2 * N_GROUPS * (6 * _nbytes((ATT_BLK, DIL_WIDTH), qkv_p[0].dtype) + _nbytes((ATT_BLK, LANES), F32))
    for g in range(N_GROUPS):
        args.append(qkv_s[g].reshape(n_seq_s, t_new, 3 * DIL_WIDTH))
        in_specs.append(pl.BlockSpec((1, t_new, 3 * DIL_WIDTH), lambda i: (i, 0, 0)))
    for g, (w, r) in enumerate(DIL_GROUPS):
        args.append(jnp.transpose(caches[g], (0, 2, 3, 4, 1)).reshape(n_seq_s, 2, DIL_WIDTH, w))
        in_specs.append(pl.BlockSpec(memory_space=pl.ANY))
        scratch.append(pltpu.VMEM((CACHE_RING, 2, DIL_WIDTH, w), F32))
        est += CACHE_RING * _nbytes((2, DIL_WIDTH, w), F32) + 3 * _nbytes((DIL_HEADS * t_new + DIL_WIDTH, w), F32)
    scratch.append(pltpu.SemaphoreType.DMA((N_GROUPS, CACHE_RING)))
    out_specs.append(pl.BlockSpec((1, t_new, DIL_WIDTH), lambda i: (i, 0, 0)))
    out_shape.append(jax.ShapeDtypeStruct((n_seq_s, t_new, DIL_WIDTH), F32))
    outs = pl.pallas_call(
        functools.partial(_attention_body, blocks_per_residue=tuple(bpr), n_blocks=n_blocks, t_new=t_new),
        grid=(n_seq_s,),
        in_specs=in_specs, out_specs=out_specs, out_shape=out_shape, scratch_shapes=scratch,
        compiler_params=pltpu.CompilerParams(dimension_semantics=("arbitrary",), vmem_limit_bytes=_vmem_limit(est)),
        name="attention",
    )(*args)
    return ([outs[2 * g] for g in range(N_GROUPS)], [outs[2 * g + 1] for g in range(N_GROUPS)],
            outs[2 * N_GROUPS].reshape(n_seq_s * t_new, DIL_WIDTH))


def _row_order(src_ref, stage_ref, r):
    if r == 1:
        return src_ref[0, 0].astype(F32)
    n = src_ref.shape[2]
    n_slab = src_ref.shape[3] // LANES
    for c in range(r):
        for j in range(n_slab):
            stage_ref[j, pl.ds(c, n, stride=r), :] = src_ref[0, c, :, j * LANES:(j + 1) * LANES].astype(F32)
    slabs = [stage_ref[j] for j in range(n_slab)]
    return slabs[0] if n_slab == 1 else jnp.concatenate(slabs, axis=1)


def _merge_body(*refs, n_dil_inputs):
    x_ref, og_ref = refs[:2]
    dil_refs = refs[2:2 + n_dil_inputs]
    (wg_ref, wa_ref, wb_ref, wo_ref, ex_ref, g_ref, b_ref, y_ref) = refs[2 + n_dil_inputs:10 + n_dil_inputs]
    x = x_ref[...]
    xb = x.astype(BF16)
    if n_dil_inputs == 1:
        yb_in = dil_refs[0][...].astype(F32)
    else:
        ostage_ref, lstage_ref = refs[10 + n_dil_inputs:]
        stages = [(None, None)] + [(ostage_ref.at[g - 1], lstage_ref.at[g - 1]) for g in range(1, N_GROUPS)]
        os_ = [_row_order(dil_refs[g], stages[g][0], DIL_GROUPS[g][1]) for g in range(N_GROUPS)]
        lses = [_row_order(dil_refs[N_GROUPS + g], stages[g][1], DIL_GROUPS[g][1]) for g in range(N_GROUPS)]
        top = functools.reduce(jnp.maximum, lses)
        es = [jnp.exp(l - top) for l in lses]
        inv = 1.0 / functools.reduce(jnp.add, es)
        yb_in = jnp.zeros(os_[0].shape, F32)
        for o, e in zip(os_, es):
            wgt = e * inv
            hi = wgt.astype(BF16)
            lo = (wgt - hi.astype(F32)).astype(BF16)
            wide = _dot(hi, ex_ref[...]) + _dot(lo, ex_ref[...])
            yb_in = yb_in + wide * o
    ya = _dot(og_ref[...].astype(BF16), wa_ref[...])
    yb = _dot(yb_in.astype(BF16), wb_ref[...])
    ga = _dot_nt(xb, wg_ref[:D_MODEL, :])
    gb = _dot_nt(xb, wg_ref[D_MODEL:, :])
    mix = _sigmoid(ga) * ya + _sigmoid(gb) * yb
    z = ALPHA * x + _dot(mix.astype(BF16), wo_ref[...])
    y_ref[...] = _layer_norm(z, g_ref[...], b_ref[...])


def _merge(x2d, o_gla, dil_inputs, w_gates, w_br_gla, w_br_dil, w_out, expand, ln_g, ln_b, *, tm, seq_len):
    m = x2d.shape[0]
    row = lambda i: (i, 0)
    tiles_per_seq = max(seq_len // tm, 1)
    acts, act_specs, scratch = [x2d, o_gla], [pl.BlockSpec((tm, D_MODEL), row), pl.BlockSpec((tm, GLA_V), row)], []
    est = 2 * _nbytes((tm, D_MODEL), F32) + 2 * _nbytes((tm, GLA_V), o_gla.dtype)
    for a in dil_inputs:
        acts.append(a)
        if a.ndim == 2:
            act_specs.append(pl.BlockSpec((tm, a.shape[1]), row))
        else:
            r = a.shape[1]
            assert seq_len % tm == 0 and tm % r == 0
            act_specs.append(pl.BlockSpec((1, r, tm // r, a.shape[3]), lambda i: (i // tiles_per_seq, 0, i % tiles_per_seq, 0)))
        est += 2 * _nbytes((tm, a.shape[-1]), a.dtype)
    if len(dil_inputs) > 1:
        scratch = [pltpu.VMEM((N_GROUPS - 1, DIL_WIDTH // LANES, tm, LANES), F32),
                   pltpu.VMEM((N_GROUPS - 1, 1, tm, LANES), F32)]
        est += (N_GROUPS - 1) * _nbytes((tm, DIL_WIDTH + LANES), F32)
    weights = [w_gates, w_br_gla, w_br_dil, w_out, expand, ln_g, ln_b]
    est += sum(_nbytes(w.shape, w.dtype) for w in weights) + 6 * _nbytes((tm, D_MODEL), F32)
    return pl.pallas_call(
        functools.partial(_merge_body, n_dil_inputs=len(dil_inputs)),
        grid=(m // tm,),
        in_specs=act_specs + [_resident(w.shape) for w in weights],
        out_specs=pl.BlockSpec((tm, D_MODEL), row),
        out_shape=jax.ShapeDtypeStruct((m, D_MODEL), F32),
        scratch_shapes=scratch,
        compiler_params=pltpu.CompilerParams(dimension_semantics=("arbitrary",), vmem_limit_bytes=_vmem_limit(est)),
        name="merge",
    )(*acts, *weights)


def _ffn_body(*refs, tm, tiles_per_seq, seq_len, tail_rows):
    long_seq = seq_len >= tm
    if long_seq:
        x_ref, halo_ref, cp_ref, pe_ref = refs[:4]
    else:
        x_ref, cp0_ref, cp1_ref, pe_ref = refs[:4]
    (wup_ref, cw_ref, cb_ref, wdn_ref, g2_ref, b2_ref, wpg_ref, wpp_ref, g3_ref, b3_ref, y_ref, tail_ref) = refs[4:]
    i = pl.program_id(0)
    x = x_ref[...]
    xb = x.astype(BF16)
    row = lax.broadcasted_iota(jnp.int32, (tm, FF_CHUNK), 0)
    acc = jnp.zeros((tm, D_MODEL), F32)
    for c in range(D_FF // FF_CHUNK):
        cs = slice(c * FF_CHUNK, (c + 1) * FF_CHUNK)
        a = _dot(xb, wup_ref[:, cs])
        u = _dot(xb, wup_ref[:, D_FF + c * FF_CHUNK:D_FF + (c + 1) * FF_CHUNK])
        if long_seq:
            a_halo = _dot(halo_ref[...].astype(BF16), wup_ref[:, cs])
            prev = jnp.where(i % tiles_per_seq == 0, cp_ref[0][:, cs], a_halo)
            p1, p2 = prev[SUBLANES - 1:SUBLANES, :], prev[SUBLANES - 2:SUBLANES - 1, :]
            t = row
        else:
            p1, p2 = cp1_ref[:, cs], cp0_ref[:, cs]
            t = _mod(row, seq_len)
        am1 = jnp.where(t == 0, p1, pltpu.roll(a, 1, 0))
        am2 = jnp.where(t == 0, p2, jnp.where(t == 1, p1, pltpu.roll(a, 2, 0)))
        cw = cw_ref[:, cs]
        conv = cb_ref[:, cs] + cw[0:1, :] * am2 + cw[1:2, :] * am1 + cw[2:3, :] * a
        gelu = 0.5 * conv * (1.0 + lax.erf(conv * (2.0 ** -0.5)))
        acc = acc + _dot((gelu * u).astype(BF16), wdn_ref[cs, :])
        tail_ref[:, cs] = a[tm - tail_rows:, :]
    x2 = _layer_norm(ALPHA * x + acc, g2_ref[...], b2_ref[...])
    gate = _sigmoid(_dot(x2.astype(BF16), wpg_ref[...]))
    emb = _dot(pe_ref[...].astype(BF16), wpp_ref[...])
    y_ref[...] = _layer_norm(ALPHA * x2 + gate * emb, g3_ref[...], b3_ref[...])


def _ffn(x2d, conv_prev, pe2d, w_up, conv_w, conv_b, w_down, ln2_g, ln2_b, w_pg, w_pp, ln3_g, ln3_b, *, tm, seq_len):
    m = x2d.shape[0]
    n_seq = m // seq_len
    row = lambda i: (i, 0)
    n_tiles = m // tm
    if seq_len >= tm:
        tiles_per_seq, tail_rows = seq_len // tm, SUBLANES
        cp = jnp.concatenate([jnp.zeros((n_seq, SUBLANES - (CONV_W - 1), D_FF), F32), conv_prev], axis=1)
        acts = [x2d, x2d, cp, pe2d]
        act_specs = [pl.BlockSpec((tm, D_MODEL), row),
                     pl.BlockSpec((SUBLANES, D_MODEL), lambda i: (jnp.maximum(i * (tm // SUBLANES) - 1, 0), 0)),
                     pl.BlockSpec((1, SUBLANES, D_FF), lambda i: (i // tiles_per_seq, 0, 0)),
                     pl.BlockSpec((tm, PLE_DIM), row)]
        act_bytes = 2 * _nbytes((tm, D_MODEL), F32) + _nbytes((SUBLANES, D_FF), F32)
    else:
        assert tm % seq_len == 0 and seq_len >= CONV_W - 1
        tiles_per_seq, tail_rows = 1, tm
        cp0 = jnp.repeat(conv_prev[:, 0, :], seq_len, axis=0)
        cp1 = jnp.repeat(conv_prev[:, 1, :], seq_len, axis=0)
        acts = [x2d, cp0, cp1, pe2d]
        act_specs = [pl.BlockSpec((tm, D_MODEL), row), pl.BlockSpec((tm, D_FF), row), pl.BlockSpec((tm, D_FF), row),
                     pl.BlockSpec((tm, PLE_DIM), row)]
        act_bytes = _nbytes((tm, D_MODEL), F32) + 2 * _nbytes((tm, D_FF), F32)
    weights = [w_up, conv_w, conv_b, w_down, ln2_g, ln2_b, w_pg, w_pp, ln3_g, ln3_b]
    est = (2 * act_bytes + sum(_nbytes(w.shape, w.dtype) for w in weights) + 4 * _nbytes((tm, D_MODEL), F32)
           + 2 * _nbytes((tail_rows, D_FF), F32) + 6 * _nbytes((tm, FF_CHUNK), F32))
    y, tail = pl.pallas_call(
        functools.partial(_ffn_body, tm=tm, tiles_per_seq=tiles_per_seq, seq_len=seq_len, tail_rows=tail_rows),
        grid=(n_tiles,),
        in_specs=act_specs + [_resident(w.shape) for w in weights],
        out_specs=[pl.BlockSpec((tm, D_MODEL), row), pl.BlockSpec((tail_rows, D_FF), row)],
        out_shape=[jax.ShapeDtypeStruct((m, D_MODEL), F32), jax.ShapeDtypeStruct((n_tiles * tail_rows, D_FF), F32)],
        compiler_params=pltpu.CompilerParams(dimension_semantics=("arbitrary",), vmem_limit_bytes=_vmem_limit(est)),
        name="ffn",
    )(*acts, *weights)
    return y, tail


def _tile_rows(m):
    return min(m, 512)


def _mixers(x, s0, pos_offset, wts, *, prompt):
    n_seq, t, _ = x.shape
    m = n_seq * t
    x2d = x.reshape(m, D_MODEL)
    act_dtype = BF16 if prompt else F32
    tm = _tile_rows(m)
    q, k, v, g, ld = _gla_proj(x2d, wts["w_gla"], wts["w_lr"], wts["w_gkb"], wts["b_gk"], tm=tm, act_dtype=act_dtype)
    qkv, kv_new = _dil_proj(x2d, wts["w_dil"], wts["inv_freq"], tm=tm, seq_len=t, pos_offset=pos_offset, act_dtype=act_dtype)
    r3 = lambda a: a.reshape(n_seq, t, a.shape[-1])
    o_gla, s_new = _gla_scan(r3(q), r3(k), r3(v), r3(g), r3(ld), s0, wts["gla_norm"],
                             nseq=n_seq if prompt else 8, cc=min(t, GLA_CHUNK))
    if prompt:
        kv_out = [jnp.transpose(kvn.reshape(n_seq, 2, DIL_HEADS, DIL_HD, -1), (0, 4, 1, 2, 3)) for kvn in kv_new]
    else:
        kv_out = [kvn.reshape(n_seq, -1, 2, DIL_HEADS, DIL_HD) for kvn in kv_new]
    return dict(x2d=x2d, tm=tm, n_seq=n_seq, t=t, o_gla=o_gla.reshape(m, GLA_V), s_new=s_new, qkv=qkv, kv_out=kv_out)


def _finish(mix, dil_inputs, pe, conv_prev, wts, *, prompt):
    n_seq, t, tm = mix["n_seq"], mix["t"], mix["tm"]
    m = n_seq * t
    x1 = _merge(mix["x2d"], mix["o_gla"], dil_inputs, wts["w_gates"], wts["w_br_gla"], wts["w_br_dil"], wts["w_out"],
                wts["expand"], wts["ln1_g"], wts["ln1_b"], tm=tm, seq_len=t)
    tm_ffn = tm if prompt else min(tm, 256)
    y, tail = _ffn(x1, conv_prev, pe.reshape(m, PLE_DIM), wts["w_up"], wts["conv_w"], wts["conv_b"], wts["w_down"],
                   wts["ln2_g"], wts["ln2_b"], wts["w_ple_gate"], wts["w_ple_proj"], wts["ln3_g"], wts["ln3_b"],
                   tm=tm_ffn, seq_len=t)
    if t >= tm_ffn:
        conv_new = tail.reshape(n_seq, t // tm_ffn, SUBLANES, D_FF)[:, -1, SUBLANES - (CONV_W - 1):, :]
    else:
        conv_new = tail.reshape(n_seq, t, D_FF)[:, t - (CONV_W - 1):, :]
    return y.reshape(n_seq, t, D_MODEL), conv_new


def _forward(x_p, x_s, pe_p, pe_s, s0_s, conv_s, caches, wts):
    n_p = x_p.shape[0]
    mix_p = _mixers(x_p, jnp.zeros((n_p, GLA_HEADS, GLA_DK, GLA_DV), s0_s.dtype), 0, wts, prompt=True)
    mix_s = _mixers(x_s, s0_s, PAST_LEN, wts, prompt=False)
    o_p, lse_p, o_s = _attention(mix_p["qkv"], mix_s["qkv"], caches, n_seq_p=n_p, seq_len=x_p.shape[1],
                                 n_seq_s=x_s.shape[0], t_new=x_s.shape[1])
    y_p, conv_new_p = _finish(mix_p, o_p + lse_p, pe_p, jnp.zeros((n_p, CONV_W - 1, D_FF), x_p.dtype), wts, prompt=True)
    y_s, conv_new_s = _finish(mix_s, [o_s], pe_s, conv_s, wts, prompt=False)
    return (y_p, mix_p["s_new"], conv_new_p, mix_p["kv_out"]), (y_s, mix_s["s_new"], conv_new_s, mix_s["kv_out"])


def _prep_weights(w_in, w_gk_b, b_gk, gla_norm, w_br_gla, w_br_dil, w_out, ln1_g, ln1_b, w_up, conv_w, conv_b, w_down,
                  ln2_g, ln2_b, w_ple_gate, w_ple_proj, ln3_g, ln3_b):
    w_in_t = jnp.transpose(w_in[0])
    rows_bf16 = lambda lo, hi: w_in_t[lo:hi].astype(BF16)
    vec = lambda a: a[0].reshape(1, -1).astype(F32)
    half = DIL_HD // 2
    inv = ROPE_THETA ** (-jnp.arange(half, dtype=F32) / half)
    head_of_lane = jnp.arange(DIL_WIDTH) // DIL_HD
    expand = (jnp.arange(LANES)[:, None] == head_of_lane[None, :]).astype(BF16)
    return {
        "w_gla": rows_bf16(0, COL_GLA),
        "w_lr": jnp.pad(rows_bf16(COL_GLA, COL_LR), ((0, LANES - GLA_RANK), (0, 0))),
        "w_gkb": jnp.pad(w_gk_b[0].astype(BF16), ((0, LANES - GLA_RANK), (0, 0))),
        "b_gk": vec(b_gk),
        "w_dil": rows_bf16(COL_LR, COL_DIL),
        "w_gates": rows_bf16(COL_DIL, w_in_t.shape[0]),
        "inv_freq": jnp.tile(inv, LANES // half).reshape(1, LANES),
        "gla_norm": vec(gla_norm),
        "w_br_gla": w_br_gla[0].astype(BF16), "w_br_dil": w_br_dil[0].astype(BF16), "w_out": w_out[0].astype(BF16),
        "expand": expand,
        "ln1_g": vec(ln1_g), "ln1_b": vec(ln1_b),
        "w_up": w_up[0].astype(BF16),
        "conv_w": jnp.pad(conv_w[0].astype(F32), ((0, SUBLANES - CONV_W), (0, 0))),
        "conv_b": vec(conv_b),
        "w_down": w_down[0].astype(BF16),
        "ln2_g": vec(ln2_g), "ln2_b": vec(ln2_b),
        "w_ple_gate": w_ple_gate[0].astype(BF16), "w_ple_proj": w_ple_proj[0].astype(BF16),
        "ln3_g": vec(ln3_g), "ln3_b": vec(ln3_b),
    }


def kernel(x_prompt, x_sample, p_prompt, p_sample, state_gla, cache_conv, cache_kv_w128, cache_kv_w512, cache_kv_w2048, w_in, w_gk_b, b_gk, gla_norm, w_br_gla, w_br_dil, w_out, ln1_g, ln1_b, w_up, conv_w, conv_b, w_down, ln2_g, ln2_b, w_ple_gate, w_ple_proj, ln3_g, ln3_b):
    assert w_in.shape[0] == DEPTH == 1
    wts = _prep_weights(w_in, w_gk_b, b_gk, gla_norm, w_br_gla, w_br_dil, w_out, ln1_g, ln1_b, w_up, conv_w, conv_b,
                        w_down, ln2_g, ln2_b, w_ple_gate, w_ple_proj, ln3_g, ln3_b)
    caches = (cache_kv_w128[0], cache_kv_w512[0], cache_kv_w2048[0])
    (y_p, s_p, conv_p, kv_p), (y_s, s_s, conv_s, kv_s) = _forward(
        x_prompt, x_sample, p_prompt[0], p_sample[0], state_gla[0], cache_conv[0], caches, wts)
    return (y_p, y_s, s_p[None], s_s[None], conv_p[None], conv_s[None],
            kv_p[0][None], kv_p[1][None], kv_p[2][None], kv_s[0][None], kv_s[1][None], kv_s[2][None])
```

```python
import functools

import jax
import jax.numpy as jnp
from jax import lax
from jax.experimental import pallas as pl
from jax.experimental.pallas import tpu as pltpu

F32 = jnp.float32
BF16 = jnp.bfloat16

D_MODEL = 1024
DEPTH = 1
PAST_LEN = 8192
GLA_HEADS = 4
GLA_DK = 128
GLA_DV = 256
GLA_QK = GLA_HEADS * GLA_DK
GLA_V = GLA_HEADS * GLA_DV
GLA_RANK = 16
GLA_NORMALIZER = 16.0
DIL_GROUPS = ((128, 1), (512, 4), (2048, 16))
N_GROUPS = len(DIL_GROUPS)
DIL_HEADS = 8
DIL_HD = 64
DIL_WIDTH = DIL_HEADS * DIL_HD
DIL_QKV = N_GROUPS * DIL_WIDTH
ROPE_THETA = 10000.0
D_FF = 2816
CONV_W = 3
PLE_DIM = 256
ALPHA = (2.0 * DEPTH) ** 0.25
NORM_EPS = 1e-5
COL_GLA = 2 * GLA_QK + 2 * GLA_V
COL_LR = COL_GLA + GLA_RANK
COL_DIL = COL_LR + 3 * DIL_QKV
NEG_FILL = -1e30

LANES = 128
SUBLANES = 8
VMEM_BYTES_V7X = 64 * 1024 * 1024
VMEM_LIMIT_CAP = 60000 * 1024

ATT_BLK = 128
GLA_CHUNK = 128
GLA_SUB = 16
FF_CHUNK = D_FF // 2
CACHE_RING = 3


def _vmem_limit(nbytes):
    return int(min(max(2 * nbytes, 16 * 1024 * 1024), VMEM_LIMIT_CAP))


def _nbytes(shape, dtype):
    n = 1
    for s in shape:
        n *= s
    return n * jnp.dtype(dtype).itemsize


def _sigmoid(x):
    return 1.0 / (1.0 + jnp.exp(-x))


def _log_sigmoid(x):
    return jnp.minimum(x, 0.0) - jnp.log1p(jnp.exp(-jnp.abs(x)))


def _layer_norm(z, g, b):
    mu = jnp.mean(z, axis=-1, keepdims=True)
    d = z - mu
    var = jnp.mean(d * d, axis=-1, keepdims=True)
    return d * lax.rsqrt(var + NORM_EPS) * g + b


def _log2(n):
    assert n > 0 and n & (n - 1) == 0, n
    return n.bit_length() - 1


def _div(x, n):
    return x >> _log2(n)


def _mod(x, n):
    _log2(n)
    return x & (n - 1)


def _resident(shape):
    return pl.BlockSpec(shape, lambda *_: (0,) * len(shape), pipeline_mode=pl.Buffered(1))


def _dot(a, b):
    return jnp.dot(a, b, preferred_element_type=F32)


def _dot_nt(a, b):
    return lax.dot_general(a, b, (((1,), (1,)), ((), ())), preferred_element_type=F32)


def _gla_proj_body(x_ref, w_ref, wlr_ref, wgkb_ref, bgk_ref, q_ref, k_ref, v_ref, g_ref, ld_ref):
    xb = x_ref[...].astype(BF16)
    cw = GLA_QK

    def proj(lo):
        return _dot_nt(xb, w_ref[lo:lo + cw, :])

    q_ref[...] = (proj(0) * GLA_DK ** -0.5).astype(q_ref.dtype)
    k_ref[...] = proj(GLA_QK).astype(k_ref.dtype)
    for c in range(GLA_V // cw):
        v_ref[:, c * cw:(c + 1) * cw] = proj(2 * GLA_QK + c * cw).astype(v_ref.dtype)
        gg = proj(2 * GLA_QK + GLA_V + c * cw)
        g_ref[:, c * cw:(c + 1) * cw] = (gg * _sigmoid(gg)).astype(g_ref.dtype)
    glr = _dot_nt(xb, wlr_ref[...])
    z = _dot(glr.astype(BF16), wgkb_ref[...]) + bgk_ref[...]
    ld_ref[...] = _log_sigmoid(z) * (1.0 / GLA_NORMALIZER)


def _gla_proj(x2d, w_gla, w_lr, w_gkb, b_gk, *, tm, act_dtype):
    m = x2d.shape[0]
    row = lambda i: (i, 0)
    out_shape = [
        jax.ShapeDtypeStruct((m, GLA_QK), act_dtype), jax.ShapeDtypeStruct((m, GLA_QK), act_dtype),
        jax.ShapeDtypeStruct((m, GLA_V), act_dtype), jax.ShapeDtypeStruct((m, GLA_V), act_dtype),
        jax.ShapeDtypeStruct((m, GLA_QK), F32),
    ]
    weights = [w_gla, w_lr, w_gkb, b_gk]
    est = (2 * _nbytes((tm, D_MODEL), F32) + sum(_nbytes(w.shape, w.dtype) for w in weights)
           + 2 * sum(_nbytes((tm, s.shape[1]), s.dtype) for s in out_shape) + 2 * _nbytes((tm, GLA_QK), F32))
    return pl.pallas_call(
        _gla_proj_body,
        grid=(m // tm,),
        in_specs=[pl.BlockSpec((tm, D_MODEL), row)] + [_resident(w.shape) for w in weights],
        out_specs=[pl.BlockSpec((tm, s.shape[1]), row) for s in out_shape],
        out_shape=out_shape,
        compiler_params=pltpu.CompilerParams(dimension_semantics=("arbitrary",), vmem_limit_bytes=_vmem_limit(est)),
        name="gla_proj",
    )(x2d, w_gla, w_lr, w_gkb, b_gk)


def _dil_proj_body(x_ref, w_ref, inv_ref, *refs, tm, seq_len, pos_offset, regroup):
    qkv_refs, kv_refs = refs[:N_GROUPS], refs[N_GROUPS:2 * N_GROUPS]
    trig_ref = refs[2 * N_GROUPS]
    stage_ref = refs[2 * N_GROUPS + 1] if regroup else None
    i = pl.program_id(0)
    xb = x_ref[...].astype(BF16)
    half = DIL_HD // 2
    tiles_per_seq = max(seq_len // tm, 1)
    it = _mod(i, tiles_per_seq)
    lane = lax.broadcasted_iota(jnp.int32, (tm, LANES), 1)

    @pl.when(i == 0)
    def _():
        rel = _mod(lax.broadcasted_iota(jnp.int32, (tm, LANES), 0), seq_len).astype(F32) * inv_ref[...]
        trig_ref[0] = jnp.cos(rel)
        trig_ref[1] = jnp.sin(rel)

    base = (pos_offset + it * tm).astype(F32) * inv_ref[...]
    cos_b, sin_b = jnp.cos(base), jnp.sin(base)
    cos = cos_b * trig_ref[0] - sin_b * trig_ref[1]
    sin = sin_b * trig_ref[0] + cos_b * trig_ref[1]
    first_half = _mod(lane, DIL_HD) < half
    sin_signed = jnp.where(first_half, -sin, sin)

    def rope(xc):
        partner = jnp.where(first_half, pltpu.roll(xc, LANES - half, 1), pltpu.roll(xc, half, 1))
        return xc * cos + partner * sin_signed

    n_slab = DIL_WIDTH // LANES
    for g, (w, r) in enumerate(DIL_GROUPS):
        hq = _dot_nt(xb, w_ref[g * DIL_WIDTH:(g + 1) * DIL_WIDTH, :])
        hk = _dot_nt(xb, w_ref[DIL_QKV + g * DIL_WIDTH:DIL_QKV + (g + 1) * DIL_WIDTH, :])
        hv = _dot_nt(xb, w_ref[2 * DIL_QKV + g * DIL_WIDTH:2 * DIL_QKV + (g + 1) * DIL_WIDTH, :])
        slabs = []
        for part, h in enumerate((hq, hk, hv)):
            for c in range(n_slab):
                xc = h[:, c * LANES:(c + 1) * LANES]
                slabs.append(xc if part == 2 else rope(xc) * (DIL_HD ** -0.5 if part == 0 else 1.0))
        if not regroup:
            for j, xc in enumerate(slabs):
                qkv_refs[g][:, j * LANES:(j + 1) * LANES] = xc.astype(qkv_refs[g].dtype)
                if j >= n_slab:
                    kv_refs[g][:, (j - n_slab) * LANES:(j - n_slab + 1) * LANES] = xc
            continue
        for j, xc in enumerate(slabs):
            if r == 1:
                qkv_refs[g][0, 0, :, j * LANES:(j + 1) * LANES] = xc.astype(qkv_refs[g].dtype)
            else:
                stage_ref[g - 1, j] = xc
        for c in range(r if r > 1 else 0):
            for j in range(3 * n_slab):
                qkv_refs[g][0, c, :, j * LANES:(j + 1) * LANES] = (
                    stage_ref[g - 1, j, pl.ds(c, tm // r, stride=r), :].astype(qkv_refs[g].dtype))
        keep = min(w, tm)

        @pl.when(it >= tiles_per_seq - w // keep)
        def _(g=g, keep=keep, slabs=slabs):
            for j in range(2 * n_slab):
                kv_refs[g][0, j // n_slab, (j % n_slab) * LANES:(j % n_slab + 1) * LANES, :] = (
                    slabs[n_slab + j][tm - keep:, :].T)


def _dil_proj(x2d, w_dil, inv_freq, *, tm, seq_len, pos_offset, act_dtype):
    m = x2d.shape[0]
    n_seq = m // seq_len
    regroup = seq_len >= tm
    row = lambda i: (i, 0)
    tiles_per_seq = max(seq_len // tm, 1)
    qkv_shapes, qkv_specs, kv_shapes, kv_specs, scratch = [], [], [], [], []
    for w, r in DIL_GROUPS:
        if regroup:
            assert tm % r == 0 and seq_len % tm == 0
            keep = min(w, tm)
            nblk = w // keep
            qkv_shapes.append(jax.ShapeDtypeStruct((n_seq, r, seq_len // r, 3 * DIL_WIDTH), act_dtype))
            qkv_specs.append(pl.BlockSpec((1, r, tm // r, 3 * DIL_WIDTH),
                                          lambda i: (i // tiles_per_seq, 0, i % tiles_per_seq, 0)))
            kv_shapes.append(jax.ShapeDtypeStruct((n_seq, 2, DIL_WIDTH, w), F32))
            kv_specs.append(pl.BlockSpec(
                (1, 2, DIL_WIDTH, keep),
                functools.partial(lambda i, nblk: (i // tiles_per_seq, 0, 0,
                                                   jnp.maximum(i % tiles_per_seq - (tiles_per_seq - nblk), 0)), nblk=nblk)))
        else:
            assert w >= seq_len
            qkv_shapes.append(jax.ShapeDtypeStruct((m, 3 * DIL_WIDTH), act_dtype))
            qkv_specs.append(pl.BlockSpec((tm, 3 * DIL_WIDTH), row))
            kv_shapes.append(jax.ShapeDtypeStruct((m, 2 * DIL_WIDTH), F32))
            kv_specs.append(pl.BlockSpec((tm, 2 * DIL_WIDTH), row))
    est = (2 * _nbytes((tm, D_MODEL), F32) + _nbytes(w_dil.shape, BF16)
           + 2 * N_GROUPS * _nbytes((tm, 3 * DIL_WIDTH), act_dtype)
           + 2 * sum(_nbytes(s.block_shape, F32) for s in kv_specs) + 3 * _nbytes((tm, DIL_WIDTH), F32))
    scratch.append(pltpu.VMEM((2, tm, LANES), F32))
    if regroup:
        assert DIL_GROUPS[0][1] == 1 and all(r > 1 for _, r in DIL_GROUPS[1:])
        scratch.append(pltpu.VMEM((N_GROUPS - 1, 3 * DIL_WIDTH // LANES, tm, LANES), F32))
        est += _nbytes((N_GROUPS - 1, tm, 3 * DIL_WIDTH), F32)
    body = functools.partial(_dil_proj_body, tm=tm, seq_len=seq_len, pos_offset=pos_offset, regroup=regroup)
    outs = pl.pallas_call(
        body,
        grid=(m // tm,),
        in_specs=[pl.BlockSpec((tm, D_MODEL), row), _resident(w_dil.shape), _resident(inv_freq.shape)],
        out_specs=qkv_specs + kv_specs,
        out_shape=qkv_shapes + kv_shapes,
        scratch_shapes=scratch,
        compiler_params=pltpu.CompilerParams(dimension_semantics=("arbitrary",), vmem_limit_bytes=_vmem_limit(est)),
        name="dil_proj",
    )(x2d, w_dil, inv_freq)
    return outs[:N_GROUPS], outs[N_GROUPS:]


def _dil_proj_short_body(x_ref, wq_ref, wk_ref, wv_ref, inv_ref, qkv_ref, kv_ref, stage_ref, *, seq_len, pos_offset):
    m = x_ref.shape[0]
    n_seq = m // seq_len
    xb = x_ref[...].astype(BF16)
    half = DIL_HD // 2
    row = lax.broadcasted_iota(jnp.int32, (m, LANES), 0)
    lane = lax.broadcasted_iota(jnp.int32, (m, LANES), 1)
    ang = (pos_offset + _mod(row, seq_len)).astype(F32) * inv_ref[...]
    cos, sin = jnp.cos(ang), jnp.sin(ang)
    first_half = _mod(lane, DIL_HD) < half
    sin_signed = jnp.where(first_half, -sin, sin)
    n_slab = DIL_WIDTH // LANES
    for part, w_ref in enumerate((wq_ref, wk_ref, wv_ref)):
        h = _dot_nt(xb, w_ref[...])
        for c in range(n_slab):
            xc = h[:, c * LANES:(c + 1) * LANES]
            if part < 2:
                partner = jnp.where(first_half, pltpu.roll(xc, LANES - half, 1), pltpu.roll(xc, half, 1))
                xc = (xc * cos + partner * sin_signed) * (DIL_HD ** -0.5 if part == 0 else 1.0)
            j = part * n_slab + c
            qkv_ref[0, :, j * LANES:(j + 1) * LANES] = xc
            if part:
                stage_ref[j - n_slab] = xc
    for t in range(seq_len):
        for j in range(2 * n_slab):
            kv_ref[0, t, j * LANES:(j + 1) * LANES, :] = stage_ref[j, pl.ds(t, n_seq, stride=seq_len), :].T


def _dil_proj_short(x2d, w_dil, inv_freq, *, seq_len, pos_offset):
    m = x2d.shape[0]
    n_seq = m // seq_len
    assert n_seq % SUBLANES == 0 and all(w >= seq_len for w, _ in DIL_GROUPS)
    w_spec = lambda part: pl.BlockSpec((DIL_WIDTH, D_MODEL), lambda g: (part * N_GROUPS + g, 0))
    est = (_nbytes((m, D_MODEL), F32) + 6 * _nbytes((DIL_WIDTH, D_MODEL), BF16) + 2 * _nbytes((m, 3 * DIL_WIDTH), F32)
           + 2 * _nbytes((seq_len, 2 * DIL_WIDTH, n_seq), F32) + _nbytes((m, 2 * DIL_WIDTH), F32) + 4 * _nbytes((m, DIL_WIDTH), F32))
    return pl.pallas_call(
        functools.partial(_dil_proj_short_body, seq_len=seq_len, pos_offset=pos_offset),
        grid=(N_GROUPS,),
        in_specs=[_resident((m, D_MODEL)), w_spec(0), w_spec(1), w_spec(2), _resident(inv_freq.shape)],
        out_specs=[pl.BlockSpec((1, m, 3 * DIL_WIDTH), lambda g: (g, 0, 0)),
                   pl.BlockSpec((1, seq_len, 2 * DIL_WIDTH, n_seq), lambda g: (g, 0, 0, 0))],
        out_shape=[jax.ShapeDtypeStruct((N_GROUPS, m, 3 * DIL_WIDTH), F32),
                   jax.ShapeDtypeStruct((N_GROUPS, seq_len, 2 * DIL_WIDTH, n_seq), F32)],
        scratch_shapes=[pltpu.VMEM((2 * DIL_WIDTH // LANES, m, LANES), F32)],
        compiler_params=pltpu.CompilerParams(dimension_semantics=("arbitrary",), vmem_limit_bytes=_vmem_limit(est)),
        name="dil_proj_short",
    )(x2d, w_dil, w_dil, w_dil, inv_freq)


def _cumsum_rows(x):
    n = x.shape[0]
    row = lax.broadcasted_iota(jnp.int32, x.shape, 0)
    s = 1
    while s < n:
        x = x + jnp.where(row >= s, pltpu.roll(x, s, 0), 0.0)
        s *= 2
    return x


def _pad_rows(x, n):
    if x.shape[0] == n:
        return x
    return jnp.concatenate([x, jnp.zeros((n - x.shape[0], x.shape[1]), x.dtype)], axis=0)


def _gla_scan_body(q_ref, k_ref, v_ref, g_ref, ld_ref, s0_ref, nw_ref, o_ref, sout_ref, *scratch, nseq, cc, carry):
    chunk = pl.program_id(1)
    sub = GLA_SUB
    cp = cc if cc % sub == 0 else sub * (cc // sub + 1)
    nsub = cp // sub
    st_ref = scratch[0] if carry else None

    if carry:
        @pl.when(chunk == 0)
        def _():
            for s in range(nseq):
                for h in range(GLA_HEADS):
                    st_ref[s * GLA_HEADS + h] = s0_ref[s, h].T

    row = lax.broadcasted_iota(jnp.int32, (cp, GLA_QK), 0)
    arow = lax.broadcasted_iota(jnp.int32, (cp, cp), 0)
    acol = lax.broadcasted_iota(jnp.int32, (cp, cp), 1)
    for s in range(nseq):
        q = _pad_rows(q_ref[s].astype(F32), cp)
        k = _pad_rows(k_ref[s].astype(F32), cp)
        b = _cumsum_rows(_pad_rows(ld_ref[s], cp))
        starts = [jnp.zeros((1, GLA_QK), F32)] + [b[i * sub - 1:i * sub, :] for i in range(1, nsub)]
        r = jnp.concatenate([jnp.broadcast_to(st, (sub, GLA_QK)) for st in starts], axis=0)
        qh = q * jnp.exp(b - r)
        qt = (qh * jnp.exp(r)).astype(BF16)
        b_last = b[cp - 1:cp, :]
        kl32 = k * jnp.exp(b_last - b)
        kl = kl32.astype(BF16)
        a_rows = [[] for _ in range(GLA_HEADS)]
        for i in range(nsub):
            e = jnp.where(row < (i + 1) * sub, starts[i] - b, NEG_FILL)
            ki = (k * jnp.exp(e)).astype(BF16)
            qi = qh[i * sub:(i + 1) * sub, :].astype(BF16)
            for h in range(GLA_HEADS):
                hs = slice(h * GLA_DK, (h + 1) * GLA_DK)
                a_rows[h].append(_dot_nt(qi[:, hs], ki[:, hs]))
        d_last = jnp.exp(b_last)
        for h in range(GLA_HEADS):
            hs = slice(h * GLA_DK, (h + 1) * GLA_DK)
            vs = slice(h * GLA_DV, (h + 1) * GLA_DV)
            a = jnp.where(acol <= arow, jnp.concatenate(a_rows[h], axis=0), 0.0).astype(BF16)
            v = _pad_rows(v_ref[s][:, vs].astype(F32), cp)
            if carry:
                st = st_ref[s * GLA_HEADS + h]
                inter = _dot_nt(qt[:, hs], st.astype(BF16))
            else:
                st = s0_ref[s, h]
                inter = _dot(qt[:, hs], st.astype(BF16))
            o = (inter + _dot(a, v.astype(BF16)))[:cc]
            o = o * lax.rsqrt(jnp.mean(o * o, axis=-1, keepdims=True) + NORM_EPS) * nw_ref[...]
            o_ref[s, :, vs] = (o * g_ref[s][:, vs].astype(F32)).astype(o_ref.dtype)
            v_tile = _pad_rows(v, GLA_CHUNK)
            if carry:
                st_ref[s * GLA_HEADS + h] = d_last[:, hs] * st + _dot(v_tile.T.astype(BF16), _pad_rows(kl[:, hs], GLA_CHUNK))
            else:
                aug = jnp.concatenate([kl32[:, hs], jnp.broadcast_to(d_last[:, hs], (SUBLANES, GLA_DK)),
                                       jnp.zeros((GLA_CHUNK - cp - SUBLANES, GLA_DK), F32)], axis=0).T
                sout_ref[s, h] = aug[:, cp:cp + 1] * st + _dot(aug.astype(BF16), v_tile.astype(BF16))

    if carry:
        @pl.when(chunk == pl.num_programs(1) - 1)
        def _():
            for s in range(nseq):
                for h in range(GLA_HEADS):
                    sout_ref[s, h] = st_ref[s * GLA_HEADS + h].T


def _gla_scan(q, k, v, g, ld, s0, norm_w, *, nseq, cc):
    n_seq, t, _ = q.shape
    assert n_seq % nseq == 0 and t % cc == 0 and cc <= GLA_CHUNK
    blk = lambda sg, c: (sg, c, 0)
    sblk = lambda sg, c: (sg, 0, 0, 0)
    carry = not (t == cc and cc + GLA_SUB + SUBLANES <= GLA_CHUNK)
    body = functools.partial(_gla_scan_body, nseq=nseq, cc=cc, carry=carry)
    est = (2 * nseq * cc * (2 * GLA_QK + 2 * GLA_V) * jnp.dtype(q.dtype).itemsize + 2 * nseq * cc * GLA_QK * 4
           + 2 * nseq * cc * GLA_V * jnp.dtype(q.dtype).itemsize + 5 * nseq * GLA_HEADS * GLA_DK * GLA_DV * 4)
    return pl.pallas_call(
        body,
        grid=(n_seq // nseq, t // cc),
        in_specs=[pl.BlockSpec((nseq, cc, GLA_QK), blk), pl.BlockSpec((nseq, cc, GLA_QK), blk),
                  pl.BlockSpec((nseq, cc, GLA_V), blk), pl.BlockSpec((nseq, cc, GLA_V), blk),
                  pl.BlockSpec((nseq, cc, GLA_QK), blk),
                  pl.BlockSpec((nseq, GLA_HEADS, GLA_DK, GLA_DV), sblk),
                  pl.BlockSpec(norm_w.shape, lambda sg, c: (0, 0))],
        out_specs=[pl.BlockSpec((nseq, cc, GLA_V), blk), pl.BlockSpec((nseq, GLA_HEADS, GLA_DK, GLA_DV), sblk)],
        out_shape=[jax.ShapeDtypeStruct((n_seq, t, GLA_V), q.dtype),
                   jax.ShapeDtypeStruct((n_seq, GLA_HEADS, GLA_DK, GLA_DV), F32)],
        scratch_shapes=[pltpu.VMEM((nseq * GLA_HEADS, GLA_DV, GLA_DK), F32)] if carry else [],
        compiler_params=pltpu.CompilerParams(dimension_semantics=("arbitrary", "arbitrary"),
                                             vmem_limit_bytes=_vmem_limit(est)),
        name="gla_scan",
    )(q, k, v, g, ld, s0, norm_w)


def _dil_attn_block(ins, outs, prev_ref, carry_ref, n, blocks_per_residue):
    row = lax.broadcasted_iota(jnp.int32, (ATT_BLK, 2 * ATT_BLK), 0)
    col = lax.broadcasted_iota(jnp.int32, (ATT_BLK, 2 * ATT_BLK), 1)
    dist = row + ATT_BLK - col
    band = (dist >= 0) & (dist <= ATT_BLK)
    lane = lax.broadcasted_iota(jnp.int32, (ATT_BLK, LANES), 1)
    for g in range(N_GROUPS):
        q_ref, kc_ref, vc_ref = ins[3 * g:3 * g + 3]
        o_ref, lse_ref = outs[2 * g:2 * g + 2]
        jb = _mod(n, blocks_per_residue[g])
        valid = band & ((col >= ATT_BLK) | (jb > 0))
        q, k_cur, v_cur = q_ref[0, 0], kc_ref[0, 0], vc_ref[0, 0]
        kk = jnp.concatenate([prev_ref[g, 0], k_cur], axis=0)
        vv = jnp.concatenate([prev_ref[g, 1], v_cur], axis=0)
        carry_ref[g, 0] = k_cur
        carry_ref[g, 1] = v_cur
        lse_tile = jnp.zeros((ATT_BLK, LANES), F32)
        for p in range(DIL_WIDTH // LANES):
            ls = slice(p * LANES, (p + 1) * LANES)
            q2, k2, v2 = q[:, ls], kk[:, ls], vv[:, ls]
            o_pair = jnp.zeros((ATT_BLK, LANES), F32)
            for e in range(LANES // DIL_HD):
                sel = (lane >= DIL_HD) if e else (lane < DIL_HD)
                s = _dot_nt(jnp.where(sel, q2, jnp.zeros_like(q2)), k2)
                s = jnp.where(valid, s, NEG_FILL)
                m = jnp.max(s, axis=-1, keepdims=True)
                pe = jnp.exp(s - m)
                den = jnp.sum(pe, axis=-1, keepdims=True)
                pv = _dot(pe.astype(BF16), v2)
                o_pair = jnp.where(sel, pv * (1.0 / den), o_pair)
                lse_tile = jnp.where(lane == p * (LANES // DIL_HD) + e, m + jnp.log(den), lse_tile)
            o_ref[0, 0, :, ls] = o_pair.astype(o_ref.dtype)
        lse_ref[0, 0] = lse_tile


def _dec_attn_seq(qkv_refs, c_refs, o_ref, t_new):
    nq = DIL_HEADS * t_new
    lane_w = lax.broadcasted_iota(jnp.int32, (nq, DIL_WIDTH), 1)
    row_w = lax.broadcasted_iota(jnp.int32, (nq, DIL_WIDTH), 0)
    own_head = _div(lane_w, DIL_HD) == _div(row_w, t_new)
    scores, values = [], []
    for g, (w, r) in enumerate(DIL_GROUPS):
        qkv = qkv_refs[g][0].astype(F32)
        q, k_new, v_new = qkv[:, :DIL_WIDTH], qkv[:, DIL_WIDTH:2 * DIL_WIDTH], qkv[:, 2 * DIL_WIDTH:]
        q_rep = jnp.broadcast_to(q[None], (DIL_HEADS, t_new, DIL_WIDTH)).reshape(nq, DIL_WIDTH)
        q_bd = jnp.where(own_head, q_rep, 0.0).astype(BF16)
        kt_c, vt_c = c_refs[g][0].astype(BF16), c_refs[g][1].astype(BF16)
        t_q = _mod(lax.broadcasted_iota(jnp.int32, (nq, w), 0), t_new)
        j = lax.broadcasted_iota(jnp.int32, (nq, w), 1)
        ok_c = (j >= t_q) & (_mod(w + t_q - j, r) == 0)
        scores.append(jnp.where(ok_c, _dot(q_bd, kt_c), NEG_FILL))
        values.append(vt_c)
        t_n = _mod(lax.broadcasted_iota(jnp.int32, (nq, LANES), 0), t_new)
        u = lax.broadcasted_iota(jnp.int32, (nq, LANES), 1)
        ok_n = (u <= t_n) & (_mod(t_n - u, r) == 0)
        scores.append(jnp.where(ok_n, _dot_nt(q_bd, _pad_rows(k_new, LANES).astype(BF16)), NEG_FILL))
        values.append(_pad_rows(v_new, LANES).astype(BF16))
    m = functools.reduce(jnp.maximum, [jnp.max(s, axis=-1, keepdims=True) for s in scores])
    den = jnp.zeros((nq, 1), F32)
    acc = jnp.zeros((nq, DIL_WIDTH), F32)
    for idx, (s, v) in enumerate(zip(scores, values)):
        pe = jnp.exp(s - m)
        den = den + jnp.sum(pe, axis=-1, keepdims=True)
        pe = pe.astype(BF16)
        acc = acc + (_dot(pe, v) if idx % 2 else _dot_nt(pe, v))
    acc = jnp.where(own_head, acc * (1.0 / den), 0.0)
    o_ref[0] = jnp.sum(acc.reshape(DIL_HEADS, t_new, DIL_WIDTH), axis=0)


def _cache_copy(c_hbm, buf, sem, g, seq, slot):
    return pltpu.make_async_copy(c_hbm[g].at[seq], buf[g].at[slot], sem.at[g, slot])


def _attention_body(*refs, blocks_per_residue, n_blocks, t_new):
    n_p = 3 * N_GROUPS
    ins_p, qkv_s, c_hbm = refs[:n_p], refs[n_p:n_p + N_GROUPS], refs[n_p + N_GROUPS:n_p + 2 * N_GROUPS]
    outs_p, o_s = refs[n_p + 2 * N_GROUPS:n_p + 4 * N_GROUPS], refs[n_p + 4 * N_GROUPS]
    bufs, sem = refs[n_p + 4 * N_GROUPS + 1:n_p + 5 * N_GROUPS + 1], refs[n_p + 5 * N_GROUPS + 1]
    prev_ref = refs[n_p + 5 * N_GROUPS + 2]
    i, n_steps = pl.program_id(0), pl.num_programs(0)

    @pl.when(i == 0)
    def _():
        for s in range(CACHE_RING - 1):
            for g in range(N_GROUPS):
                _cache_copy(c_hbm, bufs, sem, g, s, s).start()
        prev_ref[...] = jnp.zeros(prev_ref.shape, prev_ref.dtype)

    ahead = i + (CACHE_RING - 1)

    @pl.when(ahead < n_steps)
    def _():
        for g in range(N_GROUPS):
            _cache_copy(c_hbm, bufs, sem, g, ahead, ahead % CACHE_RING).start()

    parity = _mod(i, 2)
    _dil_attn_block(ins_p, outs_p, prev_ref.at[1 - parity], prev_ref.at[parity], _mod(i, n_blocks), blocks_per_residue)
    slot = i % CACHE_RING
    for g in range(N_GROUPS):
        _cache_copy(c_hbm, bufs, sem, g, i, slot).wait()
    _dec_attn_seq(qkv_s, [bufs[g].at[slot] for g in range(N_GROUPS)], o_s, t_new)


def _attention(qkv_p, qkv_s, caches, *, n_seq_p, seq_len, n_seq_s, t_new):
    args, in_specs, out_specs, out_shape, bpr, scratch = [], [], [], [], [], []
    n_blocks = seq_len // ATT_BLK
    assert n_seq_s == n_seq_p * n_blocks
    assert n_seq_s >= CACHE_RING
    for g, (w, r) in enumerate(DIL_GROUPS):
        assert w // r == ATT_BLK and seq_len % (r * ATT_BLK) == 0
        nb = n_blocks // r
        bpr.append(nb)
        cur = lambda i, nb, part: (i // n_blocks, (i % n_blocks) // nb, i % nb, part)
        blk = (1, 1, ATT_BLK, DIL_WIDTH)
        for part in range(3):
            args.append(qkv_p[g])
            in_specs.append(pl.BlockSpec(blk, functools.partial(cur, nb=nb, part=part)))
        omap = functools.partial(cur, nb=nb, part=0)
        out_specs += [pl.BlockSpec(blk, omap), pl.BlockSpec((1, 1, ATT_BLK, LANES), omap)]
        out_shape += [jax.ShapeDtypeStruct((n_seq_p, r, seq_len // r, DIL_WIDTH), qkv_p[g].dtype),
                      jax.ShapeDtypeStruct((n_seq_p, r, seq_len // r, LANES), F32)]
    est = 2 * N_GROUPS * (6 * _nbytes((ATT_BLK, DIL_WIDTH), qkv_p[0].dtype) + _nbytes((ATT_BLK, LANES), F32))
    for g in range(N_GROUPS):
        args.append(qkv_s[g].reshape(n_seq_s, t_new, 3 * DIL_WIDTH))
        in_specs.append(pl.BlockSpec((1, t_new, 3 * DIL_WIDTH), lambda i: (i, 0, 0)))
    for g, (w, r) in enumerate(DIL_GROUPS):
        args.append(jnp.transpose(caches[g], (0, 2, 3, 4, 1)).reshape(n_seq_s, 2, DIL_WIDTH, w))
        in_specs.append(pl.BlockSpec(memory_space=pl.ANY))
        scratch.append(pltpu.VMEM((CACHE_RING, 2, DIL_WIDTH, w), F32))
        est += CACHE_RING * _nbytes((2, DIL_WIDTH, w), F32) + 3 * _nbytes((DIL_HEADS * t_new + DIL_WIDTH, w), F32)
    scratch.append(pltpu.SemaphoreType.DMA((N_GROUPS, CACHE_RING)))
    scratch.append(pltpu.VMEM((2, N_GROUPS, 2, ATT_BLK, DIL_WIDTH), qkv_p[0].dtype))
    out_specs.append(pl.BlockSpec((1, t_new, DIL_WIDTH), lambda i: (i, 0, 0)))
    out_shape.append(jax.ShapeDtypeStruct((n_seq_s, t_new, DIL_WIDTH), F32))
    outs = pl.pallas_call(
        functools.partial(_attention_body, blocks_per_residue=tuple(bpr), n_blocks=n_blocks, t_new=t_new),
        grid=(n_seq_s,),
        in_specs=in_specs, out_specs=out_specs, out_shape=out_shape, scratch_shapes=scratch,
        compiler_params=pltpu.CompilerParams(dimension_semantics=("arbitrary",), vmem_limit_bytes=_vmem_limit(est)),
        name="attention",
    )(*args)
    return ([outs[2 * g] for g in range(N_GROUPS)], [outs[2 * g + 1] for g in range(N_GROUPS)],
            outs[2 * N_GROUPS].reshape(n_seq_s * t_new, DIL_WIDTH))


def _row_order(src_ref, stage_ref, r):
    if r == 1:
        return src_ref[0, 0].astype(F32)
    n = src_ref.shape[2]
    n_slab = src_ref.shape[3] // LANES
    for c in range(r):
        for j in range(n_slab):
            stage_ref[j, pl.ds(c, n, stride=r), :] = src_ref[0, c, :, j * LANES:(j + 1) * LANES].astype(F32)
    slabs = [stage_ref[j] for j in range(n_slab)]
    return slabs[0] if n_slab == 1 else jnp.concatenate(slabs, axis=1)


def _merge_body(*refs, n_dil_inputs):
    x_ref, og_ref = refs[:2]
    dil_refs = refs[2:2 + n_dil_inputs]
    (wg_ref, wa_ref, wb_ref, wo_ref, ex_ref, g_ref, b_ref, y_ref) = refs[2 + n_dil_inputs:10 + n_dil_inputs]
    x = x_ref[...]
    xb = x.astype(BF16)
    if n_dil_inputs == 1:
        yb_in = dil_refs[0][...].astype(F32)
    else:
        ostage_ref, lstage_ref = refs[10 + n_dil_inputs:]
        stages = [(None, None)] + [(ostage_ref.at[g - 1], lstage_ref.at[g - 1]) for g in range(1, N_GROUPS)]
        os_ = [_row_order(dil_refs[g], stages[g][0], DIL_GROUPS[g][1]) for g in range(N_GROUPS)]
        lses = [_row_order(dil_refs[N_GROUPS + g], stages[g][1], DIL_GROUPS[g][1]) for g in range(N_GROUPS)]
        top = functools.reduce(jnp.maximum, lses)
        es = [jnp.exp(l - top) for l in lses]
        inv = 1.0 / functools.reduce(jnp.add, es)
        yb_in = jnp.zeros(os_[0].shape, F32)
        for o, e in zip(os_, es):
            wgt = e * inv
            hi = wgt.astype(BF16)
            lo = (wgt - hi.astype(F32)).astype(BF16)
            wide = _dot(hi, ex_ref[...]) + _dot(lo, ex_ref[...])
            yb_in = yb_in + wide * o
    ya = _dot(og_ref[...].astype(BF16), wa_ref[...])
    yb = _dot(yb_in.astype(BF16), wb_ref[...])
    ga = _dot_nt(xb, wg_ref[:D_MODEL, :])
    gb = _dot_nt(xb, wg_ref[D_MODEL:, :])
    mix = _sigmoid(ga) * ya + _sigmoid(gb) * yb
    z = ALPHA * x + _dot(mix.astype(BF16), wo_ref[...])
    y_ref[...] = _layer_norm(z, g_ref[...], b_ref[...])


def _merge(x2d, o_gla, dil_inputs, w_gates, w_br_gla, w_br_dil, w_out, expand, ln_g, ln_b, *, tm, seq_len):
    m = x2d.shape[0]
    row = lambda i: (i, 0)
    tiles_per_seq = max(seq_len // tm, 1)
    acts, act_specs, scratch = [x2d, o_gla], [pl.BlockSpec((tm, D_MODEL), row), pl.BlockSpec((tm, GLA_V), row)], []
    est = 2 * _nbytes((tm, D_MODEL), F32) + 2 * _nbytes((tm, GLA_V), o_gla.dtype)
    for a in dil_inputs:
        acts.append(a)
        if a.ndim == 2:
            act_specs.append(pl.BlockSpec((tm, a.shape[1]), row))
        else:
            r = a.shape[1]
            assert seq_len % tm == 0 and tm % r == 0
            act_specs.append(pl.BlockSpec((1, r, tm // r, a.shape[3]), lambda i: (i // tiles_per_seq, 0, i % tiles_per_seq, 0)))
        est += 2 * _nbytes((tm, a.shape[-1]), a.dtype)
    if len(dil_inputs) > 1:
        scratch = [pltpu.VMEM((N_GROUPS - 1, DIL_WIDTH // LANES, tm, LANES), F32),
                   pltpu.VMEM((N_GROUPS - 1, 1, tm, LANES), F32)]
        est += (N_GROUPS - 1) * _nbytes((tm, DIL_WIDTH + LANES), F32)
    weights = [w_gates, w_br_gla, w_br_dil, w_out, expand, ln_g, ln_b]
    est += sum(_nbytes(w.shape, w.dtype) for w in weights) + 6 * _nbytes((tm, D_MODEL), F32)
    return pl.pallas_call(
        functools.partial(_merge_body, n_dil_inputs=len(dil_inputs)),
        grid=(m // tm,),
        in_specs=act_specs + [_resident(w.shape) for w in weights],
        out_specs=pl.BlockSpec((tm, D_MODEL), row),
        out_shape=jax.ShapeDtypeStruct((m, D_MODEL), F32),
        scratch_shapes=scratch,
        compiler_params=pltpu.CompilerParams(dimension_semantics=("arbitrary",), vmem_limit_bytes=_vmem_limit(est)),
        name="merge",
    )(*acts, *weights)


def _ffn_body(*refs, tm, tiles_per_seq, seq_len, tail_rows):
    long_seq = seq_len >= tm
    if long_seq:
        x_ref, halo_ref, cp_ref, pe_ref = refs[:4]
    else:
        x_ref, cp0_ref, cp1_ref, pe_ref = refs[:4]
    (wup_ref, cw_ref, cb_ref, wdn_ref, g2_ref, b2_ref, wpg_ref, wpp_ref, g3_ref, b3_ref, y_ref, tail_ref) = refs[4:]
    i = pl.program_id(0)
    x = x_ref[...]
    xb = x.astype(BF16)
    row = lax.broadcasted_iota(jnp.int32, (tm, FF_CHUNK), 0)
    acc = jnp.zeros((tm, D_MODEL), F32)
    for c in range(D_FF // FF_CHUNK):
        cs = slice(c * FF_CHUNK, (c + 1) * FF_CHUNK)
        a = _dot(xb, wup_ref[:, cs])
        u = _dot(xb, wup_ref[:, D_FF + c * FF_CHUNK:D_FF + (c + 1) * FF_CHUNK])
        if long_seq:
            a_halo = _dot(halo_ref[...].astype(BF16), wup_ref[:, cs])
            prev = jnp.where(i % tiles_per_seq == 0, cp_ref[0][:, cs], a_halo)
            p1, p2 = prev[SUBLANES - 1:SUBLANES, :], prev[SUBLANES - 2:SUBLANES - 1, :]
            t = row
        else:
            p1, p2 = cp1_ref[:, cs], cp0_ref[:, cs]
            t = _mod(row, seq_len)
        am1 = jnp.where(t == 0, p1, pltpu.roll(a, 1, 0))
        am2 = jnp.where(t == 0, p2, jnp.where(t == 1, p1, pltpu.roll(a, 2, 0)))
        cw = cw_ref[:, cs]
        conv = cb_ref[:, cs] + cw[0:1, :] * am2 + cw[1:2, :] * am1 + cw[2:3, :] * a
        gelu = 0.5 * conv * (1.0 + lax.erf(conv * (2.0 ** -0.5)))
        acc = acc + _dot((gelu * u).astype(BF16), wdn_ref[cs, :])
        tail_ref[:, cs] = a[tm - tail_rows:, :]
    x2 = _layer_norm(ALPHA * x + acc, g2_ref[...], b2_ref[...])
    gate = _sigmoid(_dot(x2.astype(BF16), wpg_ref[...]))
    emb = _dot(pe_ref[...].astype(BF16), wpp_ref[...])
    y_ref[...] = _layer_norm(ALPHA * x2 + gate * emb, g3_ref[...], b3_ref[...])


def _ffn(x2d, conv_prev, pe2d, w_up, conv_w, conv_b, w_down, ln2_g, ln2_b, w_pg, w_pp, ln3_g, ln3_b, *, tm, seq_len):
    m = x2d.shape[0]
    n_seq = m // seq_len
    row = lambda i: (i, 0)
    n_tiles = m // tm
    if seq_len >= tm:
        tiles_per_seq, tail_rows = seq_len // tm, SUBLANES
        cp = jnp.concatenate([jnp.zeros((n_seq, SUBLANES - (CONV_W - 1), D_FF), F32), conv_prev], axis=1)
        acts = [x2d, x2d, cp, pe2d]
        act_specs = [pl.BlockSpec((tm, D_MODEL), row),
                     pl.BlockSpec((SUBLANES, D_MODEL), lambda i: (jnp.maximum(i * (tm // SUBLANES) - 1, 0), 0)),
                     pl.BlockSpec((1, SUBLANES, D_FF), lambda i: (i // tiles_per_seq, 0, 0)),
                     pl.BlockSpec((tm, PLE_DIM), row)]
        act_bytes = 2 * _nbytes((tm, D_MODEL), F32) + _nbytes((SUBLANES, D_FF), F32)
    else:
        assert tm % seq_len == 0 and seq_len >= CONV_W - 1
        tiles_per_seq, tail_rows = 1, tm
        cp0 = jnp.repeat(conv_prev[:, 0, :], seq_len, axis=0)
        cp1 = jnp.repeat(conv_prev[:, 1, :], seq_len, axis=0)
        acts = [x2d, cp0, cp1, pe2d]
        act_specs = [pl.BlockSpec((tm, D_MODEL), row), pl.BlockSpec((tm, D_FF), row), pl.BlockSpec((tm, D_FF), row),
                     pl.BlockSpec((tm, PLE_DIM), row)]
        act_bytes = _nbytes((tm, D_MODEL), F32) + 2 * _nbytes((tm, D_FF), F32)
    weights = [w_up, conv_w, conv_b, w_down, ln2_g, ln2_b, w_pg, w_pp, ln3_g, ln3_b]
    est = (2 * act_bytes + sum(_nbytes(w.shape, w.dtype) for w in weights) + 4 * _nbytes((tm, D_MODEL), F32)
           + 2 * _nbytes((tail_rows, D_FF), F32) + 6 * _nbytes((tm, FF_CHUNK), F32))
    y, tail = pl.pallas_call(
        functools.partial(_ffn_body, tm=tm, tiles_per_seq=tiles_per_seq, seq_len=seq_len, tail_rows=tail_rows),
        grid=(n_tiles,),
        in_specs=act_specs + [_resident(w.shape) for w in weights],
        out_specs=[pl.BlockSpec((tm, D_MODEL), row), pl.BlockSpec((tail_rows, D_FF), row)],
        out_shape=[jax.ShapeDtypeStruct((m, D_MODEL), F32), jax.ShapeDtypeStruct((n_tiles * tail_rows, D_FF), F32)],
        compiler_params=pltpu.CompilerParams(dimension_semantics=("arbitrary",), vmem_limit_bytes=_vmem_limit(est)),
        name="ffn",
    )(*acts, *weights)
    return y, tail


def _tile_rows(m):
    return min(m, 512)


def _mixers(x, s0, pos_offset, wts, *, prompt):
    n_seq, t, _ = x.shape
    m = n_seq * t
    x2d = x.reshape(m, D_MODEL)
    act_dtype = BF16 if prompt else F32
    tm = _tile_rows(m)
    q, k, v, g, ld = _gla_proj(x2d, wts["w_gla"], wts["w_lr"], wts["w_gkb"], wts["b_gk"], tm=tm, act_dtype=act_dtype)
    if prompt:
        qkv, kv_new = _dil_proj(x2d, wts["w_dil"], wts["inv_freq"], tm=tm, seq_len=t, pos_offset=pos_offset, act_dtype=act_dtype)
        kv_out = [jnp.transpose(kvn.reshape(n_seq, 2, DIL_HEADS, DIL_HD, -1), (0, 4, 1, 2, 3)) for kvn in kv_new]
    else:
        qkv, kv_new = _dil_proj_short(x2d, wts["w_dil"], wts["inv_freq"], seq_len=t, pos_offset=pos_offset)
        kv_out = [jnp.transpose(kv_new[g].reshape(t, 2, DIL_HEADS, DIL_HD, n_seq), (4, 0, 1, 2, 3)) for g in range(N_GROUPS)]
        qkv = [qkv[g] for g in range(N_GROUPS)]
    r3 = lambda a: a.reshape(n_seq, t, a.shape[-1])
    o_gla, s_new = _gla_scan(r3(q), r3(k), r3(v), r3(g), r3(ld), s0, wts["gla_norm"],
                             nseq=n_seq if prompt else 8, cc=min(t, GLA_CHUNK))
    return dict(x2d=x2d, tm=tm, n_seq=n_seq, t=t, o_gla=o_gla.reshape(m, GLA_V), s_new=s_new, qkv=qkv, kv_out=kv_out)


def _finish(mix, dil_inputs, pe, conv_prev, wts, *, prompt):
    n_seq, t, tm = mix["n_seq"], mix["t"], mix["tm"]
    m = n_seq * t
    x1 = _merge(mix["x2d"], mix["o_gla"], dil_inputs, wts["w_gates"], wts["w_br_gla"], wts["w_br_dil"], wts["w_out"],
                wts["expand"], wts["ln1_g"], wts["ln1_b"], tm=tm, seq_len=t)
    tm_ffn = tm if prompt else min(tm, 256)
    y, tail = _ffn(x1, conv_prev, pe.reshape(m, PLE_DIM), wts["w_up"], wts["conv_w"], wts["conv_b"], wts["w_down"],
                   wts["ln2_g"], wts["ln2_b"], wts["w_ple_gate"], wts["w_ple_proj"], wts["ln3_g"], wts["ln3_b"],
                   tm=tm_ffn, seq_len=t)
    if t >= tm_ffn:
        conv_new = tail.reshape(n_seq, t // tm_ffn, SUBLANES, D_FF)[:, -1, SUBLANES - (CONV_W - 1):, :]
    else:
        conv_new = tail.reshape(n_seq, t, D_FF)[:, t - (CONV_W - 1):, :]
    return y.reshape(n_seq, t, D_MODEL), conv_new


def _forward(x_p, x_s, pe_p, pe_s, s0_s, conv_s, caches, wts):
    n_p = x_p.shape[0]
    mix_p = _mixers(x_p, jnp.zeros((n_p, GLA_HEADS, GLA_DK, GLA_DV), s0_s.dtype), 0, wts, prompt=True)
    mix_s = _mixers(x_s, s0_s, PAST_LEN, wts, prompt=False)
    o_p, lse_p, o_s = _attention(mix_p["qkv"], mix_s["qkv"], caches, n_seq_p=n_p, seq_len=x_p.shape[1],
                                 n_seq_s=x_s.shape[0], t_new=x_s.shape[1])
    y_p, conv_new_p = _finish(mix_p, o_p + lse_p, pe_p, jnp.zeros((n_p, CONV_W - 1, D_FF), x_p.dtype), wts, prompt=True)
    y_s, conv_new_s = _finish(mix_s, [o_s], pe_s, conv_s, wts, prompt=False)
    return (y_p, mix_p["s_new"], conv_new_p, mix_p["kv_out"]), (y_s, mix_s["s_new"], conv_new_s, mix_s["kv_out"])


def _prep_weights(w_in, w_gk_b, b_gk, gla_norm, w_br_gla, w_br_dil, w_out, ln1_g, ln1_b, w_up, conv_w, conv_b, w_down,
                  ln2_g, ln2_b, w_ple_gate, w_ple_proj, ln3_g, ln3_b):
    w_in_t = jnp.transpose(w_in[0])
    rows_bf16 = lambda lo, hi: w_in_t[lo:hi].astype(BF16)
    vec = lambda a: a[0].reshape(1, -1).astype(F32)
    half = DIL_HD // 2
    inv = ROPE_THETA ** (-jnp.arange(half, dtype=F32) / half)
    head_of_lane = jnp.arange(DIL_WIDTH) // DIL_HD
    expand = (jnp.arange(LANES)[:, None] == head_of_lane[None, :]).astype(BF16)
    return {
        "w_gla": rows_bf16(0, COL_GLA),
        "w_lr": jnp.pad(rows_bf16(COL_GLA, COL_LR), ((0, LANES - GLA_RANK), (0, 0))),
        "w_gkb": jnp.pad(w_gk_b[0].astype(BF16), ((0, LANES - GLA_RANK), (0, 0))),
        "b_gk": vec(b_gk),
        "w_dil": rows_bf16(COL_LR, COL_DIL),
        "w_gates": rows_bf16(COL_DIL, w_in_t.shape[0]),
        "inv_freq": jnp.tile(inv, LANES // half).reshape(1, LANES),
        "gla_norm": vec(gla_norm),
        "w_br_gla": w_br_gla[0].astype(BF16), "w_br_dil": w_br_dil[0].astype(BF16), "w_out": w_out[0].astype(BF16),
        "expand": expand,
        "ln1_g": vec(ln1_g), "ln1_b": vec(ln1_b),
        "w_up": w_up[0].astype(BF16),
        "conv_w": jnp.pad(conv_w[0].astype(F32), ((0, SUBLANES - CONV_W), (0, 0))),
        "conv_b": vec(conv_b),
        "w_down": w_down[0].astype(BF16),
        "ln2_g": vec(ln2_g), "ln2_b": vec(ln2_b),
        "w_ple_gate": w_ple_gate[0].astype(BF16), "w_ple_proj": w_ple_proj[0].astype(BF16),
        "ln3_g": vec(ln3_g), "ln3_b": vec(ln3_b),
    }


def kernel(x_prompt, x_sample, p_prompt, p_sample, state_gla, cache_conv, cache_kv_w128, cache_kv_w512, cache_kv_w2048, w_in, w_gk_b, b_gk, gla_norm, w_br_gla, w_br_dil, w_out, ln1_g, ln1_b, w_up, conv_w, conv_b, w_down, ln2_g, ln2_b, w_ple_gate, w_ple_proj, ln3_g, ln3_b):
    assert w_in.shape[0] == DEPTH == 1
    wts = _prep_weights(w_in, w_gk_b, b_gk, gla_norm, w_br_gla, w_br_dil, w_out, ln1_g, ln1_b, w_up, conv_w, conv_b,
                        w_down, ln2_g, ln2_b, w_ple_gate, w_ple_proj, ln3_g, ln3_b)
    caches = (cache_kv_w128[0], cache_kv_w512[0], cache_kv_w2048[0])
    (y_p, s_p, conv_p, kv_p), (y_s, s_s, conv_s, kv_s) = _forward(
        x_prompt, x_sample, p_prompt[0], p_sample[0], state_gla[0], cache_conv[0], caches, wts)
    return (y_p, y_s, s_p[None], s_s[None], conv_p[None], conv_s[None],
            kv_p[0][None], kv_p[1][None], kv_p[2][None], kv_s[0][None], kv_s[1][None], kv_s[2][None])
```

```python
import functools

import jax
import jax.numpy as jnp
from jax import lax
from jax.experimental import pallas as pl
from jax.experimental.pallas import tpu as pltpu

F32 = jnp.float32
BF16 = jnp.bfloat16

D_MODEL = 1024
DEPTH = 1
PAST_LEN = 8192
GLA_HEADS = 4
GLA_DK = 128
GLA_DV = 256
GLA_QK = GLA_HEADS * GLA_DK
GLA_V = GLA_HEADS * GLA_DV
GLA_RANK = 16
GLA_NORMALIZER = 16.0
DIL_GROUPS = ((128, 1), (512, 4), (2048, 16))
N_GROUPS = len(DIL_GROUPS)
DIL_HEADS = 8
DIL_HD = 64
DIL_WIDTH = DIL_HEADS * DIL_HD
DIL_QKV = N_GROUPS * DIL_WIDTH
ROPE_THETA = 10000.0
D_FF = 2816
CONV_W = 3
PLE_DIM = 256
ALPHA = (2.0 * DEPTH) ** 0.25
NORM_EPS = 1e-5
COL_GLA = 2 * GLA_QK + 2 * GLA_V
COL_LR = COL_GLA + GLA_RANK
COL_DIL = COL_LR + 3 * DIL_QKV
NEG_FILL = -1e30

LANES = 128
SUBLANES = 8
VMEM_BYTES_V7X = 64 * 1024 * 1024
VMEM_LIMIT_CAP = 60000 * 1024

ATT_BLK = 128
GLA_CHUNK = 128
GLA_SUB = 16
FF_CHUNK = D_FF // 2
CACHE_RING = 3


def _vmem_limit(nbytes):
    return int(min(max(2 * nbytes, 16 * 1024 * 1024), VMEM_LIMIT_CAP))


def _nbytes(shape, dtype):
    n = 1
    for s in shape:
        n *= s
    return n * jnp.dtype(dtype).itemsize


def _sigmoid(x):
    return 1.0 / (1.0 + jnp.exp(-x))


def _log_sigmoid(x):
    return jnp.minimum(x, 0.0) - jnp.log1p(jnp.exp(-jnp.abs(x)))


def _layer_norm(z, g, b):
    mu = jnp.mean(z, axis=-1, keepdims=True)
    d = z - mu
    var = jnp.mean(d * d, axis=-1, keepdims=True)
    return d * lax.rsqrt(var + NORM_EPS) * g + b


def _log2(n):
    assert n > 0 and n & (n - 1) == 0, n
    return n.bit_length() - 1


def _div(x, n):
    return x >> _log2(n)


def _mod(x, n):
    _log2(n)
    return x & (n - 1)


def _resident(shape):
    return pl.BlockSpec(shape, lambda *_: (0,) * len(shape), pipeline_mode=pl.Buffered(1))


def _dot(a, b):
    return jnp.dot(a, b, preferred_element_type=F32)


def _dot_nt(a, b):
    return lax.dot_general(a, b, (((1,), (1,)), ((), ())), preferred_element_type=F32)


def _gla_proj_body(x_ref, w_ref, wlr_ref, wgkb_ref, bgk_ref, q_ref, k_ref, v_ref, g_ref, ld_ref):
    xb = x_ref[...].astype(BF16)
    cw = GLA_QK

    def proj(lo):
        return _dot_nt(xb, w_ref[lo:lo + cw, :])

    glr = _dot_nt(xb, wlr_ref[...])
    z = _dot(glr.astype(BF16), wgkb_ref[...]) + bgk_ref[...]
    ld_ref[...] = _log_sigmoid(z) * (1.0 / GLA_NORMALIZER)
    q_ref[...] = (proj(0) * GLA_DK ** -0.5).astype(q_ref.dtype)
    k_ref[...] = proj(GLA_QK).astype(k_ref.dtype)
    for c in range(GLA_V // cw):
        v_ref[:, c * cw:(c + 1) * cw] = proj(2 * GLA_QK + c * cw).astype(v_ref.dtype)
        gg = proj(2 * GLA_QK + GLA_V + c * cw)
        g_ref[:, c * cw:(c + 1) * cw] = (gg * _sigmoid(gg)).astype(g_ref.dtype)


def _gla_proj(x2d, w_gla, w_lr, w_gkb, b_gk, *, tm, act_dtype):
    m = x2d.shape[0]
    row = lambda i: (i, 0)
    out_shape = [
        jax.ShapeDtypeStruct((m, GLA_QK), act_dtype), jax.ShapeDtypeStruct((m, GLA_QK), act_dtype),
        jax.ShapeDtypeStruct((m, GLA_V), act_dtype), jax.ShapeDtypeStruct((m, GLA_V), act_dtype),
        jax.ShapeDtypeStruct((m, GLA_QK), F32),
    ]
    weights = [w_gla, w_lr, w_gkb, b_gk]
    est = (2 * _nbytes((tm, D_MODEL), F32) + sum(_nbytes(w.shape, w.dtype) for w in weights)
           + 2 * sum(_nbytes((tm, s.shape[1]), s.dtype) for s in out_shape) + 2 * _nbytes((tm, GLA_QK), F32))
    return pl.pallas_call(
        _gla_proj_body,
        grid=(m // tm,),
        in_specs=[pl.BlockSpec((tm, D_MODEL), row)] + [_resident(w.shape) for w in weights],
        out_specs=[pl.BlockSpec((tm, s.shape[1]), row) for s in out_shape],
        out_shape=out_shape,
        compiler_params=pltpu.CompilerParams(dimension_semantics=("arbitrary",), vmem_limit_bytes=_vmem_limit(est)),
        name="gla_proj",
    )(x2d, w_gla, w_lr, w_gkb, b_gk)


def _dil_proj_body(x_ref, w_ref, inv_ref, *refs, tm, seq_len, pos_offset, regroup):
    qkv_refs, kv_refs = refs[:N_GROUPS], refs[N_GROUPS:2 * N_GROUPS]
    trig_ref = refs[2 * N_GROUPS]
    stage_ref = refs[2 * N_GROUPS + 1] if regroup else None
    i = pl.program_id(0)
    xb = x_ref[...].astype(BF16)
    half = DIL_HD // 2
    tiles_per_seq = max(seq_len // tm, 1)
    it = _mod(i, tiles_per_seq)
    lane = lax.broadcasted_iota(jnp.int32, (tm, LANES), 1)

    @pl.when(i == 0)
    def _():
        rel = _mod(lax.broadcasted_iota(jnp.int32, (tm, LANES), 0), seq_len).astype(F32) * inv_ref[...]
        trig_ref[0] = jnp.cos(rel)
        trig_ref[1] = jnp.sin(rel)

    base = (pos_offset + it * tm).astype(F32) * inv_ref[...]
    cos_b, sin_b = jnp.cos(base), jnp.sin(base)
    cos = cos_b * trig_ref[0] - sin_b * trig_ref[1]
    sin = sin_b * trig_ref[0] + cos_b * trig_ref[1]
    first_half = _mod(lane, DIL_HD) < half
    sin_signed = jnp.where(first_half, -sin, sin)

    def rope(xc):
        partner = jnp.where(first_half, pltpu.roll(xc, LANES - half, 1), pltpu.roll(xc, half, 1))
        return xc * cos + partner * sin_signed

    n_slab = DIL_WIDTH // LANES
    for g, (w, r) in enumerate(DIL_GROUPS):
        hq = _dot_nt(xb, w_ref[g * DIL_WIDTH:(g + 1) * DIL_WIDTH, :])
        hk = _dot_nt(xb, w_ref[DIL_QKV + g * DIL_WIDTH:DIL_QKV + (g + 1) * DIL_WIDTH, :])
        hv = _dot_nt(xb, w_ref[2 * DIL_QKV + g * DIL_WIDTH:2 * DIL_QKV + (g + 1) * DIL_WIDTH, :])
        slabs = []
        for part, h in enumerate((hq, hk, hv)):
            for c in range(n_slab):
                xc = h[:, c * LANES:(c + 1) * LANES]
                slabs.append(xc if part == 2 else rope(xc) * (DIL_HD ** -0.5 if part == 0 else 1.0))
        if not regroup:
            for j, xc in enumerate(slabs):
                qkv_refs[g][:, j * LANES:(j + 1) * LANES] = xc.astype(qkv_refs[g].dtype)
                if j >= n_slab:
                    kv_refs[g][:, (j - n_slab) * LANES:(j - n_slab + 1) * LANES] = xc
            continue
        for j, xc in enumerate(slabs):
            if r == 1:
                qkv_refs[g][0, 0, :, j * LANES:(j + 1) * LANES] = xc.astype(qkv_refs[g].dtype)
            else:
                stage_ref[g - 1, j] = xc
        for c in range(r if r > 1 else 0):
            for j in range(3 * n_slab):
                qkv_refs[g][0, c, :, j * LANES:(j + 1) * LANES] = (
                    stage_ref[g - 1, j, pl.ds(c, tm // r, stride=r), :].astype(qkv_refs[g].dtype))
        keep = min(w, tm)

        @pl.when(it >= tiles_per_seq - w // keep)
        def _(g=g, keep=keep, slabs=slabs):
            for j in range(2 * n_slab):
                kv_refs[g][0, j // n_slab, (j % n_slab) * LANES:(j % n_slab + 1) * LANES, :] = (
                    slabs[n_slab + j][tm - keep:, :].T)


def _dil_proj(x2d, w_dil, inv_freq, *, tm, seq_len, pos_offset, act_dtype):
    m = x2d.shape[0]
    n_seq = m // seq_len
    regroup = seq_len >= tm
    row = lambda i: (i, 0)
    tiles_per_seq = max(seq_len // tm, 1)
    qkv_shapes, qkv_specs, kv_shapes, kv_specs, scratch = [], [], [], [], []
    for w, r in DIL_GROUPS:
        if regroup:
            assert tm % r == 0 and seq_len % tm == 0
            keep = min(w, tm)
            nblk = w // keep
            qkv_shapes.append(jax.ShapeDtypeStruct((n_seq, r, seq_len // r, 3 * DIL_WIDTH), act_dtype))
            qkv_specs.append(pl.BlockSpec((1, r, tm // r, 3 * DIL_WIDTH),
                                          lambda i: (i // tiles_per_seq, 0, i % tiles_per_seq, 0)))
            kv_shapes.append(jax.ShapeDtypeStruct((n_seq, 2, DIL_WIDTH, w), F32))
            kv_specs.append(pl.BlockSpec(
                (1, 2, DIL_WIDTH, keep),
                functools.partial(lambda i, nblk: (i // tiles_per_seq, 0, 0,
                                                   jnp.maximum(i % tiles_per_seq - (tiles_per_seq - nblk), 0)), nblk=nblk)))
        else:
            assert w >= seq_len
            qkv_shapes.append(jax.ShapeDtypeStruct((m, 3 * DIL_WIDTH), act_dtype))
            qkv_specs.append(pl.BlockSpec((tm, 3 * DIL_WIDTH), row))
            kv_shapes.append(jax.ShapeDtypeStruct((m, 2 * DIL_WIDTH), F32))
            kv_specs.append(pl.BlockSpec((tm, 2 * DIL_WIDTH), row))
    est = (2 * _nbytes((tm, D_MODEL), F32) + _nbytes(w_dil.shape, BF16)
           + 2 * N_GROUPS * _nbytes((tm, 3 * DIL_WIDTH), act_dtype)
           + 2 * sum(_nbytes(s.block_shape, F32) for s in kv_specs) + 3 * _nbytes((tm, DIL_WIDTH), F32))
    scratch.append(pltpu.VMEM((2, tm, LANES), F32))
    if regroup:
        assert DIL_GROUPS[0][1] == 1 and all(r > 1 for _, r in DIL_GROUPS[1:])
        scratch.append(pltpu.VMEM((N_GROUPS - 1, 3 * DIL_WIDTH // LANES, tm, LANES), F32))
        est += _nbytes((N_GROUPS - 1, tm, 3 * DIL_WIDTH), F32)
    body = functools.partial(_dil_proj_body, tm=tm, seq_len=seq_len, pos_offset=pos_offset, regroup=regroup)
    outs = pl.pallas_call(
        body,
        grid=(m // tm,),
        in_specs=[pl.BlockSpec((tm, D_MODEL), row), _resident(w_dil.shape), _resident(inv_freq.shape)],
        out_specs=qkv_specs + kv_specs,
        out_shape=qkv_shapes + kv_shapes,
        scratch_shapes=scratch,
        compiler_params=pltpu.CompilerParams(dimension_semantics=("arbitrary",), vmem_limit_bytes=_vmem_limit(est)),
        name="dil_proj",
    )(x2d, w_dil, inv_freq)
    return outs[:N_GROUPS], outs[N_GROUPS:]


def _dil_proj_short_body(x_ref, wq_ref, wk_ref, wv_ref, inv_ref, qkv_ref, kv_ref, stage_ref, *, seq_len, pos_offset):
    m = x_ref.shape[0]
    n_seq = m // seq_len
    xb = x_ref[...].astype(BF16)
    half = DIL_HD // 2
    row = lax.broadcasted_iota(jnp.int32, (m, LANES), 0)
    lane = lax.broadcasted_iota(jnp.int32, (m, LANES), 1)
    ang = (pos_offset + _mod(row, seq_len)).astype(F32) * inv_ref[...]
    cos, sin = jnp.cos(ang), jnp.sin(ang)
    first_half = _mod(lane, DIL_HD) < half
    sin_signed = jnp.where(first_half, -sin, sin)
    n_slab = DIL_WIDTH // LANES
    for part, w_ref in enumerate((wq_ref, wk_ref, wv_ref)):
        h = _dot_nt(xb, w_ref[...])
        for c in range(n_slab):
            xc = h[:, c * LANES:(c + 1) * LANES]
            if part < 2:
                partner = jnp.where(first_half, pltpu.roll(xc, LANES - half, 1), pltpu.roll(xc, half, 1))
                xc = (xc * cos + partner * sin_signed) * (DIL_HD ** -0.5 if part == 0 else 1.0)
            j = part * n_slab + c
            qkv_ref[0, :, j * LANES:(j + 1) * LANES] = xc
            if part:
                stage_ref[j - n_slab] = xc
    for t in range(seq_len):
        for j in range(2 * n_slab):
            kv_ref[0, t, j * LANES:(j + 1) * LANES, :] = stage_ref[j, pl.ds(t, n_seq, stride=seq_len), :].T


def _dil_proj_short(x2d, w_dil, inv_freq, *, seq_len, pos_offset):
    m = x2d.shape[0]
    n_seq = m // seq_len
    assert n_seq % SUBLANES == 0 and all(w >= seq_len for w, _ in DIL_GROUPS)
    w_spec = lambda part: pl.BlockSpec((DIL_WIDTH, D_MODEL), lambda g: (part * N_GROUPS + g, 0))
    est = (_nbytes((m, D_MODEL), F32) + 6 * _nbytes((DIL_WIDTH, D_MODEL), BF16) + 2 * _nbytes((m, 3 * DIL_WIDTH), F32)
           + 2 * _nbytes((seq_len, 2 * DIL_WIDTH, n_seq), F32) + _nbytes((m, 2 * DIL_WIDTH), F32) + 4 * _nbytes((m, DIL_WIDTH), F32))
    return pl.pallas_call(
        functools.partial(_dil_proj_short_body, seq_len=seq_len, pos_offset=pos_offset),
        grid=(N_GROUPS,),
        in_specs=[_resident((m, D_MODEL)), w_spec(0), w_spec(1), w_spec(2), _resident(inv_freq.shape)],
        out_specs=[pl.BlockSpec((1, m, 3 * DIL_WIDTH), lambda g: (g, 0, 0)),
                   pl.BlockSpec((1, seq_len, 2 * DIL_WIDTH, n_seq), lambda g: (g, 0, 0, 0))],
        out_shape=[jax.ShapeDtypeStruct((N_GROUPS, m, 3 * DIL_WIDTH), F32),
                   jax.ShapeDtypeStruct((N_GROUPS, seq_len, 2 * DIL_WIDTH, n_seq), F32)],
        scratch_shapes=[pltpu.VMEM((2 * DIL_WIDTH // LANES, m, LANES), F32)],
        compiler_params=pltpu.CompilerParams(dimension_semantics=("arbitrary",), vmem_limit_bytes=_vmem_limit(est)),
        name="dil_proj_short",
    )(x2d, w_dil, w_dil, w_dil, inv_freq)


def _cumsum_rows(x):
    n = x.shape[0]
    row = lax.broadcasted_iota(jnp.int32, x.shape, 0)
    s = 1
    while s < n:
        x = x + jnp.where(row >= s, pltpu.roll(x, s, 0), 0.0)
        s *= 2
    return x


def _pad_rows(x, n):
    if x.shape[0] == n:
        return x
    return jnp.concatenate([x, jnp.zeros((n - x.shape[0], x.shape[1]), x.dtype)], axis=0)


def _gla_scan_body(q_ref, k_ref, v_ref, g_ref, ld_ref, s0_ref, nw_ref, o_ref, sout_ref, *scratch, nseq, cc, carry):
    chunk = pl.program_id(1)
    sub = GLA_SUB
    cp = cc if cc % sub == 0 else sub * (cc // sub + 1)
    nsub = cp // sub
    st_ref = scratch[0] if carry else None

    if carry:
        @pl.when(chunk == 0)
        def _():
            for s in range(nseq):
                for h in range(GLA_HEADS):
                    st_ref[s * GLA_HEADS + h] = s0_ref[s, h].T

    arow = lax.broadcasted_iota(jnp.int32, (cp, cp), 0)
    acol = lax.broadcasted_iota(jnp.int32, (cp, cp), 1)
    for s in range(nseq):
        q = _pad_rows(q_ref[s].astype(F32), cp)
        k = _pad_rows(k_ref[s].astype(F32), cp)
        b = _cumsum_rows(_pad_rows(ld_ref[s], cp))
        b_last = b[cp - 1:cp, :]
        starts = [jnp.zeros((1, GLA_QK), F32)] + [b[i * sub - 1:i * sub, :] for i in range(1, nsub)]
        ends = starts[1:] + [b_last]
        rows_of = lambda vs: jnp.concatenate([jnp.broadcast_to(v, (sub, GLA_QK)) for v in vs], axis=0)
        r, e_own = rows_of(starts), rows_of(ends)
        qh = q * jnp.exp(b - r)
        qt = (qh * jnp.exp(r)).astype(BF16)
        kl32 = k * jnp.exp(b_last - b)
        kl = kl32.astype(BF16)
        k_end = k * jnp.exp(e_own - b)
        k_diag = (k * jnp.exp(r - b)).astype(BF16)
        a_rows = [[] for _ in range(GLA_HEADS)]
        for i in range(nsub):
            pieces = []
            if i:
                hop = jnp.exp(starts[i] - jnp.concatenate(ends[:i], axis=0))
                pieces = [(k_end[j * sub:(j + 1) * sub, :] * hop[j:j + 1, :]).astype(BF16) for j in range(i)]
            pieces.append(k_diag[i * sub:(i + 1) * sub, :])
            if i + 1 < nsub:
                pieces.append(jnp.zeros((cp - (i + 1) * sub, GLA_QK), BF16))
            ki = jnp.concatenate(pieces, axis=0) if len(pieces) > 1 else pieces[0]
            qi = qh[i * sub:(i + 1) * sub, :].astype(BF16)
            for h in range(GLA_HEADS):
                hs = slice(h * GLA_DK, (h + 1) * GLA_DK)
                a_rows[h].append(_dot_nt(qi[:, hs], ki[:, hs]))
        d_last = jnp.exp(b_last)
        for h in range(GLA_HEADS):
            hs = slice(h * GLA_DK, (h + 1) * GLA_DK)
            vs = slice(h * GLA_DV, (h + 1) * GLA_DV)
            a = jnp.where(acol <= arow, jnp.concatenate(a_rows[h], axis=0), 0.0).astype(BF16)
            v = _pad_rows(v_ref[s][:, vs].astype(F32), cp)
            if carry:
                st = st_ref[s * GLA_HEADS + h]
                inter = _dot_nt(qt[:, hs], st.astype(BF16))
            else:
                st = s0_ref[s, h]
                inter = _dot(qt[:, hs], st.astype(BF16))
            o = (inter + _dot(a, v.astype(BF16)))[:cc]
            o = o * lax.rsqrt(jnp.mean(o * o, axis=-1, keepdims=True) + NORM_EPS) * nw_ref[...]
            o_ref[s, :, vs] = (o * g_ref[s][:, vs].astype(F32)).astype(o_ref.dtype)
            v_tile = _pad_rows(v, GLA_CHUNK)
            if carry:
                st_ref[s * GLA_HEADS + h] = d_last[:, hs] * st + _dot(v_tile.T.astype(BF16), _pad_rows(kl[:, hs], GLA_CHUNK))
            else:
                aug = jnp.concatenate([kl32[:, hs], jnp.broadcast_to(d_last[:, hs], (SUBLANES, GLA_DK)),
                                       jnp.zeros((GLA_CHUNK - cp - SUBLANES, GLA_DK), F32)], axis=0).T
                sout_ref[s, h] = aug[:, cp:cp + 1] * st + _dot(aug.astype(BF16), v_tile.astype(BF16))

    if carry:
        @pl.when(chunk == pl.num_programs(1) - 1)
        def _():
            for s in range(nseq):
                for h in range(GLA_HEADS):
                    sout_ref[s, h] = st_ref[s * GLA_HEADS + h].T


def _gla_scan(q, k, v, g, ld, s0, norm_w, *, nseq, cc):
    n_seq, t, _ = q.shape
    assert n_seq % nseq == 0 and t % cc == 0 and cc <= GLA_CHUNK
    blk = lambda sg, c: (sg, c, 0)
    sblk = lambda sg, c: (sg, 0, 0, 0)
    carry = not (t == cc and cc + GLA_SUB + SUBLANES <= GLA_CHUNK)
    body = functools.partial(_gla_scan_body, nseq=nseq, cc=cc, carry=carry)
    est = (2 * nseq * cc * (2 * GLA_QK + 2 * GLA_V) * jnp.dtype(q.dtype).itemsize + 2 * nseq * cc * GLA_QK * 4
           + 2 * nseq * cc * GLA_V * jnp.dtype(q.dtype).itemsize + 5 * nseq * GLA_HEADS * GLA_DK * GLA_DV * 4)
    return pl.pallas_call(
        body,
        grid=(n_seq // nseq, t // cc),
        in_specs=[pl.BlockSpec((nseq, cc, GLA_QK), blk), pl.BlockSpec((nseq, cc, GLA_QK), blk),
                  pl.BlockSpec((nseq, cc, GLA_V), blk), pl.BlockSpec((nseq, cc, GLA_V), blk),
                  pl.BlockSpec((nseq, cc, GLA_QK), blk),
                  pl.BlockSpec((nseq, GLA_HEADS, GLA_DK, GLA_DV), sblk),
                  pl.BlockSpec(norm_w.shape, lambda sg, c: (0, 0))],
        out_specs=[pl.BlockSpec((nseq, cc, GLA_V), blk), pl.BlockSpec((nseq, GLA_HEADS, GLA_DK, GLA_DV), sblk)],
        out_shape=[jax.ShapeDtypeStruct((n_seq, t, GLA_V), q.dtype),
                   jax.ShapeDtypeStruct((n_seq, GLA_HEADS, GLA_DK, GLA_DV), F32)],
        scratch_shapes=[pltpu.VMEM((nseq * GLA_HEADS, GLA_DV, GLA_DK), F32)] if carry else [],
        compiler_params=pltpu.CompilerParams(dimension_semantics=("arbitrary", "arbitrary"),
                                             vmem_limit_bytes=_vmem_limit(est)),
        name="gla_scan",
    )(q, k, v, g, ld, s0, norm_w)


def _dil_attn_block(ins, outs, prev_ref, carry_ref, n, blocks_per_residue):
    row = lax.broadcasted_iota(jnp.int32, (ATT_BLK, 2 * ATT_BLK), 0)
    col = lax.broadcasted_iota(jnp.int32, (ATT_BLK, 2 * ATT_BLK), 1)
    dist = row + ATT_BLK - col
    band = (dist >= 0) & (dist <= ATT_BLK)
    lane = lax.broadcasted_iota(jnp.int32, (ATT_BLK, LANES), 1)
    for g in range(N_GROUPS):
        q_ref, kc_ref, vc_ref = ins[3 * g:3 * g + 3]
        o_ref, lse_ref = outs[2 * g:2 * g + 2]
        jb = _mod(n, blocks_per_residue[g])
        valid = band & ((col >= ATT_BLK) | (jb > 0))
        q, k_cur, v_cur = q_ref[0, 0], kc_ref[0, 0], vc_ref[0, 0]
        kk = jnp.concatenate([prev_ref[g, 0], k_cur], axis=0)
        vv = jnp.concatenate([prev_ref[g, 1], v_cur], axis=0)
        carry_ref[g, 0] = k_cur
        carry_ref[g, 1] = v_cur
        lse_tile = jnp.zeros((ATT_BLK, LANES), F32)
        for p in range(DIL_WIDTH // LANES):
            ls = slice(p * LANES, (p + 1) * LANES)
            q2, k2, v2 = q[:, ls], kk[:, ls], vv[:, ls]
            o_pair = jnp.zeros((ATT_BLK, LANES), F32)
            for e in range(LANES // DIL_HD):
                sel = (lane >= DIL_HD) if e else (lane < DIL_HD)
                s = _dot_nt(jnp.where(sel, q2, jnp.zeros_like(q2)), k2)
                s = jnp.where(valid, s, NEG_FILL)
                m = jnp.max(s, axis=-1, keepdims=True)
                pe = jnp.exp(s - m)
                den = jnp.sum(pe, axis=-1, keepdims=True)
                pv = _dot(pe.astype(BF16), v2)
                o_pair = jnp.where(sel, pv * (1.0 / den), o_pair)
                lse_tile = jnp.where(lane == p * (LANES // DIL_HD) + e, m + jnp.log(den), lse_tile)
            o_ref[0, 0, :, ls] = o_pair.astype(o_ref.dtype)
        lse_ref[0, 0] = lse_tile


def _dec_key_masks(mask_ref, t_new):
    nq = DIL_HEADS * t_new
    off = 0
    for w, r in DIL_GROUPS:
        t_q = _mod(lax.broadcasted_iota(jnp.int32, (nq, w), 0), t_new)
        j = lax.broadcasted_iota(jnp.int32, (nq, w), 1)
        mask_ref[:, off:off + w] = ((j >= t_q) & (_mod(w + t_q - j, r) == 0)).astype(F32)
        t_n = _mod(lax.broadcasted_iota(jnp.int32, (nq, LANES), 0), t_new)
        u = lax.broadcasted_iota(jnp.int32, (nq, LANES), 1)
        ok_n = (u <= t_n) & (_mod(t_n - u, r) == 0)
        mask_ref[:, off + w:off + w + LANES] = ok_n.astype(F32)
        off += w + LANES


def _dec_attn_seq(qkv_refs, c_refs, o_ref, mask_ref, t_new):
    nq = DIL_HEADS * t_new
    off = 0
    lane_w = lax.broadcasted_iota(jnp.int32, (nq, DIL_WIDTH), 1)
    row_w = lax.broadcasted_iota(jnp.int32, (nq, DIL_WIDTH), 0)
    own_head = _div(lane_w, DIL_HD) == _div(row_w, t_new)
    scores, values = [], []
    for g, (w, r) in enumerate(DIL_GROUPS):
        qkv = qkv_refs[g][0].astype(F32)
        q, k_new, v_new = qkv[:, :DIL_WIDTH], qkv[:, DIL_WIDTH:2 * DIL_WIDTH], qkv[:, 2 * DIL_WIDTH:]
        q_rep = jnp.broadcast_to(q[None], (DIL_HEADS, t_new, DIL_WIDTH)).reshape(nq, DIL_WIDTH)
        q_bd = jnp.where(own_head, q_rep, 0.0).astype(BF16)
        kt_c, vt_c = c_refs[g][0].astype(BF16), c_refs[g][1].astype(BF16)
        ok_c = mask_ref[:, off:off + w] > 0.5
        ok_n = mask_ref[:, off + w:off + w + LANES] > 0.5
        off += w + LANES
        scores.append(jnp.where(ok_c, _dot(q_bd, kt_c), NEG_FILL))
        values.append(vt_c)
        scores.append(jnp.where(ok_n, _dot_nt(q_bd, _pad_rows(k_new, LANES).astype(BF16)), NEG_FILL))
        values.append(_pad_rows(v_new, LANES).astype(BF16))
    m = functools.reduce(jnp.maximum, [jnp.max(s, axis=-1, keepdims=True) for s in scores])
    den = jnp.zeros((nq, 1), F32)
    acc = jnp.zeros((nq, DIL_WIDTH), F32)
    for idx, (s, v) in enumerate(zip(scores, values)):
        pe = jnp.exp(s - m)
        den = den + jnp.sum(pe, axis=-1, keepdims=True)
        pe = pe.astype(BF16)
        acc = acc + (_dot(pe, v) if idx % 2 else _dot_nt(pe, v))
    acc = jnp.where(own_head, acc * (1.0 / den), 0.0)
    o_ref[0] = jnp.sum(acc.reshape(DIL_HEADS, t_new, DIL_WIDTH), axis=0)


def _cache_copy(c_hbm, buf, sem, g, seq, slot):
    return pltpu.make_async_copy(c_hbm[g].at[seq], buf[g].at[slot], sem.at[g, slot])


def _attention_body(*refs, blocks_per_residue, n_blocks, t_new):
    n_p = 3 * N_GROUPS
    ins_p, qkv_s, c_hbm = refs[:n_p], refs[n_p:n_p + N_GROUPS], refs[n_p + N_GROUPS:n_p + 2 * N_GROUPS]
    outs_p, o_s = refs[n_p + 2 * N_GROUPS:n_p + 4 * N_GROUPS], refs[n_p + 4 * N_GROUPS]
    bufs, sem = refs[n_p + 4 * N_GROUPS + 1:n_p + 5 * N_GROUPS + 1], refs[n_p + 5 * N_GROUPS + 1]
    prev_ref, mask_ref = refs[n_p + 5 * N_GROUPS + 2], refs[n_p + 5 * N_GROUPS + 3]
    i, n_steps = pl.program_id(0), pl.num_programs(0)

    @pl.when(i == 0)
    def _():
        for s in range(CACHE_RING - 1):
            for g in range(N_GROUPS):
                _cache_copy(c_hbm, bufs, sem, g, s, s).start()
        prev_ref[...] = jnp.zeros(prev_ref.shape, prev_ref.dtype)
        _dec_key_masks(mask_ref, t_new)

    ahead = i + (CACHE_RING - 1)

    @pl.when(ahead < n_steps)
    def _():
        for g in range(N_GROUPS):
            _cache_copy(c_hbm, bufs, sem, g, ahead, ahead % CACHE_RING).start()

    slot = i % CACHE_RING
    for g in range(N_GROUPS):
        _cache_copy(c_hbm, bufs, sem, g, i, slot).wait()
    parity = _mod(i, 2)
    _dil_attn_block(ins_p, outs_p, prev_ref.at[1 - parity], prev_ref.at[parity], _mod(i, n_blocks), blocks_per_residue)
    _dec_attn_seq(qkv_s, [bufs[g].at[slot] for g in range(N_GROUPS)], o_s, mask_ref, t_new)


def _attention(qkv_p, qkv_s, caches, *, n_seq_p, seq_len, n_seq_s, t_new):
    args, in_specs, out_specs, out_shape, bpr, scratch = [], [], [], [], [], []
    n_blocks = seq_len // ATT_BLK
    assert n_seq_s == n_seq_p * n_blocks
    assert n_seq_s >= CACHE_RING
    for g, (w, r) in enumerate(DIL_GROUPS):
        assert w // r == ATT_BLK and seq_len % (r * ATT_BLK) == 0
        nb = n_blocks // r
        bpr.append(nb)
        cur = lambda i, nb, part: (i // n_blocks, (i % n_blocks) // nb, i % nb, part)
        blk = (1, 1, ATT_BLK, DIL_WIDTH)
        for part in range(3):
            args.append(qkv_p[g])
            in_specs.append(pl.BlockSpec(blk, functools.partial(cur, nb=nb, part=part)))
        omap = functools.partial(cur, nb=nb, part=0)
        out_specs += [pl.BlockSpec(blk, omap), pl.BlockSpec((1, 1, ATT_BLK, LANES), omap)]
        out_shape += [jax.ShapeDtypeStruct((n_seq_p, r, seq_len // r, DIL_WIDTH), qkv_p[g].dtype),
                      jax.ShapeDtypeStruct((n_seq_p, r, seq_len // r, LANES), F32)]
    est = 2 * N_GROUPS * (6 * _nbytes((ATT_BLK, DIL_WIDTH), qkv_p[0].dtype) + _nbytes((ATT_BLK, LANES), F32))
    for g in range(N_GROUPS):
        args.append(qkv_s[g].reshape(n_seq_s, t_new, 3 * DIL_WIDTH))
        in_specs.append(pl.BlockSpec((1, t_new, 3 * DIL_WIDTH), lambda i: (i, 0, 0)))
    for g, (w, r) in enumerate(DIL_GROUPS):
        args.append(jnp.transpose(caches[g], (0, 2, 3, 4, 1)).reshape(n_seq_s, 2, DIL_WIDTH, w))
        in_specs.append(pl.BlockSpec(memory_space=pl.ANY))
        scratch.append(pltpu.VMEM((CACHE_RING, 2, DIL_WIDTH, w), F32))
        est += CACHE_RING * _nbytes((2, DIL_WIDTH, w), F32) + 3 * _nbytes((DIL_HEADS * t_new + DIL_WIDTH, w), F32)
    scratch.append(pltpu.SemaphoreType.DMA((N_GROUPS, CACHE_RING)))
    scratch.append(pltpu.VMEM((2, N_GROUPS, 2, ATT_BLK, DIL_WIDTH), qkv_p[0].dtype))
    scratch.append(pltpu.VMEM((DIL_HEADS * t_new, sum(w + LANES for w, _ in DIL_GROUPS)), F32))
    out_specs.append(pl.BlockSpec((1, t_new, DIL_WIDTH), lambda i: (i, 0, 0)))
    out_shape.append(jax.ShapeDtypeStruct((n_seq_s, t_new, DIL_WIDTH), F32))
    outs = pl.pallas_call(
        functools.partial(_attention_body, blocks_per_residue=tuple(bpr), n_blocks=n_blocks, t_new=t_new),
        grid=(n_seq_s,),
        in_specs=in_specs, out_specs=out_specs, out_shape=out_shape, scratch_shapes=scratch,
        compiler_params=pltpu.CompilerParams(dimension_semantics=("arbitrary",), vmem_limit_bytes=_vmem_limit(est)),
        name="attention",
    )(*args)
    return ([outs[2 * g] for g in range(N_GROUPS)], [outs[2 * g + 1] for g in range(N_GROUPS)],
            outs[2 * N_GROUPS].reshape(n_seq_s * t_new, DIL_WIDTH))


def _row_order(src_ref, stage_ref, r):
    if r == 1:
        return src_ref[0, 0].astype(F32)
    n = src_ref.shape[2]
    n_slab = src_ref.shape[3] // LANES
    for c in range(r):
        for j in range(n_slab):
            stage_ref[j, pl.ds(c, n, stride=r), :] = src_ref[0, c, :, j * LANES:(j + 1) * LANES].astype(F32)
    slabs = [stage_ref[j] for j in range(n_slab)]
    return slabs[0] if n_slab == 1 else jnp.concatenate(slabs, axis=1)


def _merge_body(*refs, n_dil_inputs):
    x_ref, og_ref = refs[:2]
    dil_refs = refs[2:2 + n_dil_inputs]
    (wg_ref, wa_ref, wb_ref, wo_ref, ex_ref, g_ref, b_ref, y_ref) = refs[2 + n_dil_inputs:10 + n_dil_inputs]
    x = x_ref[...]
    xb = x.astype(BF16)
    if n_dil_inputs == 1:
        yb_in = dil_refs[0][...].astype(F32)
    else:
        ostage_ref, lstage_ref = refs[10 + n_dil_inputs:]
        stages = [(None, None)] + [(ostage_ref.at[g - 1], lstage_ref.at[g - 1]) for g in range(1, N_GROUPS)]
        os_ = [_row_order(dil_refs[g], stages[g][0], DIL_GROUPS[g][1]) for g in range(N_GROUPS)]
        lses = [_row_order(dil_refs[N_GROUPS + g], stages[g][1], DIL_GROUPS[g][1]) for g in range(N_GROUPS)]
        top = functools.reduce(jnp.maximum, lses)
        es = [jnp.exp(l - top) for l in lses]
        inv = 1.0 / functools.reduce(jnp.add, es)
        yb_in = jnp.zeros(os_[0].shape, F32)
        for o, e in zip(os_, es):
            wgt = e * inv
            hi = wgt.astype(BF16)
            lo = (wgt - hi.astype(F32)).astype(BF16)
            wide = _dot(hi, ex_ref[...]) + _dot(lo, ex_ref[...])
            yb_in = yb_in + wide * o
    ya = _dot(og_ref[...].astype(BF16), wa_ref[...])
    yb = _dot(yb_in.astype(BF16), wb_ref[...])
    ga = _dot_nt(xb, wg_ref[:D_MODEL, :])
    gb = _dot_nt(xb, wg_ref[D_MODEL:, :])
    mix = _sigmoid(ga) * ya + _sigmoid(gb) * yb
    z = ALPHA * x + _dot(mix.astype(BF16), wo_ref[...])
    y_ref[...] = _layer_norm(z, g_ref[...], b_ref[...])


def _merge(x2d, o_gla, dil_inputs, w_gates, w_br_gla, w_br_dil, w_out, expand, ln_g, ln_b, *, tm, seq_len):
    m = x2d.shape[0]
    row = lambda i: (i, 0)
    tiles_per_seq = max(seq_len // tm, 1)
    acts, act_specs, scratch = [x2d, o_gla], [pl.BlockSpec((tm, D_MODEL), row), pl.BlockSpec((tm, GLA_V), row)], []
    est = 2 * _nbytes((tm, D_MODEL), F32) + 2 * _nbytes((tm, GLA_V), o_gla.dtype)
    for a in dil_inputs:
        acts.append(a)
        if a.ndim == 2:
            act_specs.append(pl.BlockSpec((tm, a.shape[1]), row))
        else:
            r = a.shape[1]
            assert seq_len % tm == 0 and tm % r == 0
            act_specs.append(pl.BlockSpec((1, r, tm // r, a.shape[3]), lambda i: (i // tiles_per_seq, 0, i % tiles_per_seq, 0)))
        est += 2 * _nbytes((tm, a.shape[-1]), a.dtype)
    if len(dil_inputs) > 1:
        scratch = [pltpu.VMEM((N_GROUPS - 1, DIL_WIDTH // LANES, tm, LANES), F32),
                   pltpu.VMEM((N_GROUPS - 1, 1, tm, LANES), F32)]
        est += (N_GROUPS - 1) * _nbytes((tm, DIL_WIDTH + LANES), F32)
    weights = [w_gates, w_br_gla, w_br_dil, w_out, expand, ln_g, ln_b]
    est += sum(_nbytes(w.shape, w.dtype) for w in weights) + 6 * _nbytes((tm, D_MODEL), F32)
    return pl.pallas_call(
        functools.partial(_merge_body, n_dil_inputs=len(dil_inputs)),
        grid=(m // tm,),
        in_specs=act_specs + [_resident(w.shape) for w in weights],
        out_specs=pl.BlockSpec((tm, D_MODEL), row),
        out_shape=jax.ShapeDtypeStruct((m, D_MODEL), F32),
        scratch_shapes=scratch,
        compiler_params=pltpu.CompilerParams(dimension_semantics=("arbitrary",), vmem_limit_bytes=_vmem_limit(est)),
        name="merge",
    )(*acts, *weights)


def _ffn_body(*refs, tm, tiles_per_seq, seq_len, tail_rows):
    long_seq = seq_len >= tm
    if long_seq:
        x_ref, halo_ref, cp_ref, pe_ref = refs[:4]
    else:
        x_ref, cp0_ref, cp1_ref, pe_ref = refs[:4]
    (wup_ref, cw_ref, cb_ref, wdn_ref, g2_ref, b2_ref, wpg_ref, wpp_ref, g3_ref, b3_ref, y_ref, tail_ref) = refs[4:]
    i = pl.program_id(0)
    x = x_ref[...]
    xb = x.astype(BF16)
    row = lax.broadcasted_iota(jnp.int32, (tm, FF_CHUNK), 0)
    acc = jnp.zeros((tm, D_MODEL), F32)
    for c in range(D_FF // FF_CHUNK):
        cs = slice(c * FF_CHUNK, (c + 1) * FF_CHUNK)
        a = _dot(xb, wup_ref[:, cs])
        u = _dot(xb, wup_ref[:, D_FF + c * FF_CHUNK:D_FF + (c + 1) * FF_CHUNK])
        if long_seq:
            a_halo = _dot(halo_ref[...].astype(BF16), wup_ref[:, cs])
            prev = jnp.where(i % tiles_per_seq == 0, cp_ref[0][:, cs], a_halo)
            p1, p2 = prev[SUBLANES - 1:SUBLANES, :], prev[SUBLANES - 2:SUBLANES - 1, :]
            t = row
        else:
            p1, p2 = cp1_ref[:, cs], cp0_ref[:, cs]
            t = _mod(row, seq_len)
        am1 = jnp.where(t == 0, p1, pltpu.roll(a, 1, 0))
        am2 = jnp.where(t == 0, p2, jnp.where(t == 1, p1, pltpu.roll(a, 2, 0)))
        cw = cw_ref[:, cs]
        conv = cb_ref[:, cs] + cw[0:1, :] * am2 + cw[1:2, :] * am1 + cw[2:3, :] * a
        gelu = 0.5 * conv * (1.0 + lax.erf(conv * (2.0 ** -0.5)))
        acc = acc + _dot((gelu * u).astype(BF16), wdn_ref[cs, :])
        tail_ref[:, cs] = a[tm - tail_rows:, :]
    x2 = _layer_norm(ALPHA * x + acc, g2_ref[...], b2_ref[...])
    gate = _sigmoid(_dot(x2.astype(BF16), wpg_ref[...]))
    emb = _dot(pe_ref[...].astype(BF16), wpp_ref[...])
    y_ref[...] = _layer_norm(ALPHA * x2 + gate * emb, g3_ref[...], b3_ref[...])


def _ffn(x2d, conv_prev, pe2d, w_up, conv_w, conv_b, w_down, ln2_g, ln2_b, w_pg, w_pp, ln3_g, ln3_b, *, tm, seq_len):
    m = x2d.shape[0]
    n_seq = m // seq_len
    row = lambda i: (i, 0)
    n_tiles = m // tm
    if seq_len >= tm:
        tiles_per_seq, tail_rows = seq_len // tm, SUBLANES
        cp = jnp.concatenate([jnp.zeros((n_seq, SUBLANES - (CONV_W - 1), D_FF), F32), conv_prev], axis=1)
        acts = [x2d, x2d, cp, pe2d]
        act_specs = [pl.BlockSpec((tm, D_MODEL), row),
                     pl.BlockSpec((SUBLANES, D_MODEL), lambda i: (jnp.maximum(i * (tm // SUBLANES) - 1, 0), 0)),
                     pl.BlockSpec((1, SUBLANES, D_FF), lambda i: (i // tiles_per_seq, 0, 0)),
                     pl.BlockSpec((tm, PLE_DIM), row)]
        act_bytes = 2 * _nbytes((tm, D_MODEL), F32) + _nbytes((SUBLANES, D_FF), F32)
    else:
        assert tm % seq_len == 0 and seq_len >= CONV_W - 1
        tiles_per_seq, tail_rows = 1, tm
        cp0 = jnp.repeat(conv_prev[:, 0, :], seq_len, axis=0)
        cp1 = jnp.repeat(conv_prev[:, 1, :], seq_len, axis=0)
        acts = [x2d, cp0, cp1, pe2d]
        act_specs = [pl.BlockSpec((tm, D_MODEL), row), pl.BlockSpec((tm, D_FF), row), pl.BlockSpec((tm, D_FF), row),
                     pl.BlockSpec((tm, PLE_DIM), row)]
        act_bytes = _nbytes((tm, D_MODEL), F32) + 2 * _nbytes((tm, D_FF), F32)
    weights = [w_up, conv_w, conv_b, w_down, ln2_g, ln2_b, w_pg, w_pp, ln3_g, ln3_b]
    est = (2 * act_bytes + sum(_nbytes(w.shape, w.dtype) for w in weights) + 4 * _nbytes((tm, D_MODEL), F32)
           + 2 * _nbytes((tail_rows, D_FF), F32) + 6 * _nbytes((tm, FF_CHUNK), F32))
    y, tail = pl.pallas_call(
        functools.partial(_ffn_body, tm=tm, tiles_per_seq=tiles_per_seq, seq_len=seq_len, tail_rows=tail_rows),
        grid=(n_tiles,),
        in_specs=act_specs + [_resident(w.shape) for w in weights],
        out_specs=[pl.BlockSpec((tm, D_MODEL), row), pl.BlockSpec((tail_rows, D_FF), row)],
        out_shape=[jax.ShapeDtypeStruct((m, D_MODEL), F32), jax.ShapeDtypeStruct((n_tiles * tail_rows, D_FF), F32)],
        compiler_params=pltpu.CompilerParams(dimension_semantics=("arbitrary",), vmem_limit_bytes=_vmem_limit(est)),
        name="ffn",
    )(*acts, *weights)
    return y, tail


def _tile_rows(m):
    return min(m, 512)


def _mixers(x, s0, pos_offset, wts, *, prompt):
    n_seq, t, _ = x.shape
    m = n_seq * t
    x2d = x.reshape(m, D_MODEL)
    act_dtype = BF16 if prompt else F32
    tm = _tile_rows(m)
    q, k, v, g, ld = _gla_proj(x2d, wts["w_gla"], wts["w_lr"], wts["w_gkb"], wts["b_gk"], tm=tm, act_dtype=act_dtype)
    if prompt:
        qkv, kv_new = _dil_proj(x2d, wts["w_dil"], wts["inv_freq"], tm=tm, seq_len=t, pos_offset=pos_offset, act_dtype=act_dtype)
        kv_out = [jnp.transpose(kvn.reshape(n_seq, 2, DIL_HEADS, DIL_HD, -1), (0, 4, 1, 2, 3)) for kvn in kv_new]
    else:
        qkv, kv_new = _dil_proj_short(x2d, wts["w_dil"], wts["inv_freq"], seq_len=t, pos_offset=pos_offset)
        kv_out = [jnp.transpose(kv_new[g].reshape(t, 2, DIL_HEADS, DIL_HD, n_seq), (4, 0, 1, 2, 3)) for g in range(N_GROUPS)]
        qkv = [qkv[g] for g in range(N_GROUPS)]
    r3 = lambda a: a.reshape(n_seq, t, a.shape[-1])
    o_gla, s_new = _gla_scan(r3(q), r3(k), r3(v), r3(g), r3(ld), s0, wts["gla_norm"],
                             nseq=n_seq if prompt else 8, cc=min(t, GLA_CHUNK))
    return dict(x2d=x2d, tm=tm, n_seq=n_seq, t=t, o_gla=o_gla.reshape(m, GLA_V), s_new=s_new, qkv=qkv, kv_out=kv_out)


def _finish(mix, dil_inputs, pe, conv_prev, wts, *, prompt):
    n_seq, t, tm = mix["n_seq"], mix["t"], mix["tm"]
    m = n_seq * t
    x1 = _merge(mix["x2d"], mix["o_gla"], dil_inputs, wts["w_gates"], wts["w_br_gla"], wts["w_br_dil"], wts["w_out"],
                wts["expand"], wts["ln1_g"], wts["ln1_b"], tm=tm, seq_len=t)
    tm_ffn = tm if prompt else min(tm, 256)
    y, tail = _ffn(x1, conv_prev, pe.reshape(m, PLE_DIM), wts["w_up"], wts["conv_w"], wts["conv_b"], wts["w_down"],
                   wts["ln2_g"], wts["ln2_b"], wts["w_ple_gate"], wts["w_ple_proj"], wts["ln3_g"], wts["ln3_b"],
                   tm=tm_ffn, seq_len=t)
    if t >= tm_ffn:
        conv_new = tail.reshape(n_seq, t // tm_ffn, SUBLANES, D_FF)[:, -1, SUBLANES - (CONV_W - 1):, :]
    else:
        conv_new = tail.reshape(n_seq, t, D_FF)[:, t - (CONV_W - 1):, :]
    return y.reshape(n_seq, t, D_MODEL), conv_new


def _forward(x_p, x_s, pe_p, pe_s, s0_s, conv_s, caches, wts):
    n_p = x_p.shape[0]
    mix_p = _mixers(x_p, jnp.zeros((n_p, GLA_HEADS, GLA_DK, GLA_DV), s0_s.dtype), 0, wts, prompt=True)
    mix_s = _mixers(x_s, s0_s, PAST_LEN, wts, prompt=False)
    o_p, lse_p, o_s = _attention(mix_p["qkv"], mix_s["qkv"], caches, n_seq_p=n_p, seq_len=x_p.shape[1],
                                 n_seq_s=x_s.shape[0], t_new=x_s.shape[1])
    y_p, conv_new_p = _finish(mix_p, o_p + lse_p, pe_p, jnp.zeros((n_p, CONV_W - 1, D_FF), x_p.dtype), wts, prompt=True)
    y_s, conv_new_s = _finish(mix_s, [o_s], pe_s, conv_s, wts, prompt=False)
    return (y_p, mix_p["s_new"], conv_new_p, mix_p["kv_out"]), (y_s, mix_s["s_new"], conv_new_s, mix_s["kv_out"])


def _prep_weights(w_in, w_gk_b, b_gk, gla_norm, w_br_gla, w_br_dil, w_out, ln1_g, ln1_b, w_up, conv_w, conv_b, w_down,
                  ln2_g, ln2_b, w_ple_gate, w_ple_proj, ln3_g, ln3_b):
    w_in_t = jnp.transpose(w_in[0])
    rows_bf16 = lambda lo, hi: w_in_t[lo:hi].astype(BF16)
    vec = lambda a: a[0].reshape(1, -1).astype(F32)
    half = DIL_HD // 2
    inv = ROPE_THETA ** (-jnp.arange(half, dtype=F32) / half)
    head_of_lane = jnp.arange(DIL_WIDTH) // DIL_HD
    expand = (jnp.arange(LANES)[:, None] == head_of_lane[None, :]).astype(BF16)
    return {
        "w_gla": rows_bf16(0, COL_GLA),
        "w_lr": jnp.pad(rows_bf16(COL_GLA, COL_LR), ((0, LANES - GLA_RANK), (0, 0))),
        "w_gkb": jnp.pad(w_gk_b[0].astype(BF16), ((0, LANES - GLA_RANK), (0, 0))),
        "b_gk": vec(b_gk),
        "w_dil": rows_bf16(COL_LR, COL_DIL),
        "w_gates": rows_bf16(COL_DIL, w_in_t.shape[0]),
        "inv_freq": jnp.tile(inv, LANES // half).reshape(1, LANES),
        "gla_norm": vec(gla_norm),
        "w_br_gla": w_br_gla[0].astype(BF16), "w_br_dil": w_br_dil[0].astype(BF16), "w_out": w_out[0].astype(BF16),
        "expand": expand,
        "ln1_g": vec(ln1_g), "ln1_b": vec(ln1_b),
        "w_up": w_up[0].astype(BF16),
        "conv_w": jnp.pad(conv_w[0].astype(F32), ((0, SUBLANES - CONV_W), (0, 0))),
        "conv_b": vec(conv_b),
        "w_down": w_down[0].astype(BF16),
        "ln2_g": vec(ln2_g), "ln2_b": vec(ln2_b),
        "w_ple_gate": w_ple_gate[0].astype(BF16), "w_ple_proj": w_ple_proj[0].astype(BF16),
        "ln3_g": vec(ln3_g), "ln3_b": vec(ln3_b),
    }


def kernel(x_prompt, x_sample, p_prompt, p_sample, state_gla, cache_conv, cache_kv_w128, cache_kv_w512, cache_kv_w2048, w_in, w_gk_b, b_gk, gla_norm, w_br_gla, w_br_dil, w_out, ln1_g, ln1_b, w_up, conv_w, conv_b, w_down, ln2_g, ln2_b, w_ple_gate, w_ple_proj, ln3_g, ln3_b):
    assert w_in.shape[0] == DEPTH == 1
    wts = _prep_weights(w_in, w_gk_b, b_gk, gla_norm, w_br_gla, w_br_dil, w_out, ln1_g, ln1_b, w_up, conv_w, conv_b,
                        w_down, ln2_g, ln2_b, w_ple_gate, w_ple_proj, ln3_g, ln3_b)
    caches = (cache_kv_w128[0], cache_kv_w512[0], cache_kv_w2048[0])
    (y_p, s_p, conv_p, kv_p), (y_s, s_s, conv_s, kv_s) = _forward(
        x_prompt, x_sample, p_prompt[0], p_sample[0], state_gla[0], cache_conv[0], caches, wts)
    return (y_p, y_s, s_p[None], s_s[None], conv_p[None], conv_s[None],
            kv_p[0][None], kv_p[1][None], kv_p[2][None], kv_s[0][None], kv_s[1][None], kv_s[2][None])
```

```python
import functools

import jax
import jax.numpy as jnp
from jax import lax
from jax.experimental import pallas as pl
from jax.experimental.pallas import tpu as pltpu

F32 = jnp.float32
BF16 = jnp.bfloat16

D_MODEL = 1024
DEPTH = 1
PAST_LEN = 8192
GLA_HEADS = 4
GLA_DK = 128
GLA_DV = 256
GLA_QK = GLA_HEADS * GLA_DK
GLA_V = GLA_HEADS * GLA_DV
GLA_RANK = 16
GLA_NORMALIZER = 16.0
DIL_GROUPS = ((128, 1), (512, 4), (2048, 16))
N_GROUPS = len(DIL_GROUPS)
DIL_HEADS = 8
DIL_HD = 64
DIL_WIDTH = DIL_HEADS * DIL_HD
DIL_QKV = N_GROUPS * DIL_WIDTH
ROPE_THETA = 10000.0
D_FF = 2816
CONV_W = 3
PLE_DIM = 256
ALPHA = (2.0 * DEPTH) ** 0.25
NORM_EPS = 1e-5
COL_GLA = 2 * GLA_QK + 2 * GLA_V
COL_LR = COL_GLA + GLA_RANK
COL_DIL = COL_LR + 3 * DIL_QKV
NEG_FILL = -1e30

LANES = 128
SUBLANES = 8
VMEM_BYTES_V7X = 64 * 1024 * 1024
VMEM_LIMIT_CAP = 60000 * 1024

ATT_BLK = 128
GLA_CHUNK = 128
GLA_SUB = 16
FF_CHUNK = D_FF // 2
CACHE_RING = 3


def _vmem_limit(nbytes):
    return int(min(max(2 * nbytes, 16 * 1024 * 1024), VMEM_LIMIT_CAP))


def _nbytes(shape, dtype):
    n = 1
    for s in shape:
        n *= s
    return n * jnp.dtype(dtype).itemsize


def _sigmoid(x):
    return 1.0 / (1.0 + jnp.exp(-x))


def _log_sigmoid(x):
    return jnp.minimum(x, 0.0) - jnp.log1p(jnp.exp(-jnp.abs(x)))


def _layer_norm(z, g, b):
    mu = jnp.mean(z, axis=-1, keepdims=True)
    d = z - mu
    var = jnp.mean(d * d, axis=-1, keepdims=True)
    return d * lax.rsqrt(var + NORM_EPS) * g + b


def _log2(n):
    assert n > 0 and n & (n - 1) == 0, n
    return n.bit_length() - 1


def _div(x, n):
    return x >> _log2(n)


def _mod(x, n):
    _log2(n)
    return x & (n - 1)


def _resident(shape):
    return pl.BlockSpec(shape, lambda *_: (0,) * len(shape), pipeline_mode=pl.Buffered(1))


def _dot(a, b):
    return jnp.dot(a, b, preferred_element_type=F32)


def _dot_nt(a, b):
    return lax.dot_general(a, b, (((1,), (1,)), ((), ())), preferred_element_type=F32)


def _gla_proj_body(x_ref, w_ref, wlr_ref, wgkb_ref, bgk_ref, q_ref, k_ref, v_ref, g_ref, ld_ref):
    xb = x_ref[...].astype(BF16)
    cw = GLA_QK

    def proj(lo):
        return _dot_nt(xb, w_ref[lo:lo + cw, :])

    glr = _dot_nt(xb, wlr_ref[...])
    z = _dot(glr.astype(BF16), wgkb_ref[...]) + bgk_ref[...]
    ld_ref[...] = _log_sigmoid(z) * (1.0 / GLA_NORMALIZER)
    q_ref[...] = (proj(0) * GLA_DK ** -0.5).astype(q_ref.dtype)
    k_ref[...] = proj(GLA_QK).astype(k_ref.dtype)
    for c in range(GLA_V // cw):
        v_ref[:, c * cw:(c + 1) * cw] = proj(2 * GLA_QK + c * cw).astype(v_ref.dtype)
        gg = proj(2 * GLA_QK + GLA_V + c * cw)
        g_ref[:, c * cw:(c + 1) * cw] = (gg * _sigmoid(gg)).astype(g_ref.dtype)


def _gla_proj(x2d, w_gla, w_lr, w_gkb, b_gk, *, tm, act_dtype):
    m = x2d.shape[0]
    row = lambda i: (i, 0)
    out_shape = [
        jax.ShapeDtypeStruct((m, GLA_QK), act_dtype), jax.ShapeDtypeStruct((m, GLA_QK), act_dtype),
        jax.ShapeDtypeStruct((m, GLA_V), act_dtype), jax.ShapeDtypeStruct((m, GLA_V), act_dtype),
        jax.ShapeDtypeStruct((m, GLA_QK), F32),
    ]
    weights = [w_gla, w_lr, w_gkb, b_gk]
    est = (2 * _nbytes((tm, D_MODEL), F32) + sum(_nbytes(w.shape, w.dtype) for w in weights)
           + 2 * sum(_nbytes((tm, s.shape[1]), s.dtype) for s in out_shape) + 2 * _nbytes((tm, GLA_QK), F32))
    return pl.pallas_call(
        _gla_proj_body,
        grid=(m // tm,),
        in_specs=[pl.BlockSpec((tm, D_MODEL), row)] + [_resident(w.shape) for w in weights],
        out_specs=[pl.BlockSpec((tm, s.shape[1]), row) for s in out_shape],
        out_shape=out_shape,
        compiler_params=pltpu.CompilerParams(dimension_semantics=("arbitrary",), vmem_limit_bytes=_vmem_limit(est)),
        name="gla_proj",
    )(x2d, w_gla, w_lr, w_gkb, b_gk)


def _dil_proj_body(x_ref, w_ref, inv_ref, *refs, tm, seq_len, pos_offset, regroup):
    qkv_refs, kv_refs = refs[:N_GROUPS], refs[N_GROUPS:2 * N_GROUPS]
    trig_ref = refs[2 * N_GROUPS]
    stage_ref = refs[2 * N_GROUPS + 1] if regroup else None
    i = pl.program_id(0)
    xb = x_ref[...].astype(BF16)
    half = DIL_HD // 2
    tiles_per_seq = max(seq_len // tm, 1)
    it = _mod(i, tiles_per_seq)
    lane = lax.broadcasted_iota(jnp.int32, (tm, LANES), 1)

    @pl.when(i == 0)
    def _():
        rel = _mod(lax.broadcasted_iota(jnp.int32, (tm, LANES), 0), seq_len).astype(F32) * inv_ref[...]
        trig_ref[0] = jnp.cos(rel)
        trig_ref[1] = jnp.sin(rel)

    base = (pos_offset + it * tm).astype(F32) * inv_ref[...]
    cos_b, sin_b = jnp.cos(base), jnp.sin(base)
    cos = cos_b * trig_ref[0] - sin_b * trig_ref[1]
    sin = sin_b * trig_ref[0] + cos_b * trig_ref[1]
    first_half = _mod(lane, DIL_HD) < half
    sin_signed = jnp.where(first_half, -sin, sin)

    def rope(xc):
        partner = jnp.where(first_half, pltpu.roll(xc, LANES - half, 1), pltpu.roll(xc, half, 1))
        return xc * cos + partner * sin_signed

    n_slab = DIL_WIDTH // LANES
    for g, (w, r) in enumerate(DIL_GROUPS):
        hq = _dot_nt(xb, w_ref[g * DIL_WIDTH:(g + 1) * DIL_WIDTH, :])
        hk = _dot_nt(xb, w_ref[DIL_QKV + g * DIL_WIDTH:DIL_QKV + (g + 1) * DIL_WIDTH, :])
        hv = _dot_nt(xb, w_ref[2 * DIL_QKV + g * DIL_WIDTH:2 * DIL_QKV + (g + 1) * DIL_WIDTH, :])
        slabs = []
        for part, h in enumerate((hq, hk, hv)):
            for c in range(n_slab):
                xc = h[:, c * LANES:(c + 1) * LANES]
                slabs.append(xc if part == 2 else rope(xc) * (DIL_HD ** -0.5 if part == 0 else 1.0))
        if not regroup:
            for j, xc in enumerate(slabs):
                qkv_refs[g][:, j * LANES:(j + 1) * LANES] = xc.astype(qkv_refs[g].dtype)
                if j >= n_slab:
                    kv_refs[g][:, (j - n_slab) * LANES:(j - n_slab + 1) * LANES] = xc
            continue
        for j, xc in enumerate(slabs):
            if r == 1:
                qkv_refs[g][0, 0, :, j * LANES:(j + 1) * LANES] = xc.astype(qkv_refs[g].dtype)
            else:
                stage_ref[g - 1, j] = xc
        for c in range(r if r > 1 else 0):
            for j in range(3 * n_slab):
                qkv_refs[g][0, c, :, j * LANES:(j + 1) * LANES] = (
                    stage_ref[g - 1, j, pl.ds(c, tm // r, stride=r), :].astype(qkv_refs[g].dtype))
        keep = min(w, tm)

        @pl.when(it >= tiles_per_seq - w // keep)
        def _(g=g, keep=keep, slabs=slabs):
            for j in range(2 * n_slab):
                kv_refs[g][0, j // n_slab, (j % n_slab) * LANES:(j % n_slab + 1) * LANES, :] = (
                    slabs[n_slab + j][tm - keep:, :].T)


def _dil_proj(x2d, w_dil, inv_freq, *, tm, seq_len, pos_offset, act_dtype):
    m = x2d.shape[0]
    n_seq = m // seq_len
    regroup = seq_len >= tm
    row = lambda i: (i, 0)
    tiles_per_seq = max(seq_len // tm, 1)
    qkv_shapes, qkv_specs, kv_shapes, kv_specs, scratch = [], [], [], [], []
    for w, r in DIL_GROUPS:
        if regroup:
            assert tm % r == 0 and seq_len % tm == 0
            keep = min(w, tm)
            nblk = w // keep
            qkv_shapes.append(jax.ShapeDtypeStruct((n_seq, r, seq_len // r, 3 * DIL_WIDTH), act_dtype))
            qkv_specs.append(pl.BlockSpec((1, r, tm // r, 3 * DIL_WIDTH),
                                          lambda i: (i // tiles_per_seq, 0, i % tiles_per_seq, 0)))
            kv_shapes.append(jax.ShapeDtypeStruct((n_seq, 2, DIL_WIDTH, w), F32))
            kv_specs.append(pl.BlockSpec(
                (1, 2, DIL_WIDTH, keep),
                functools.partial(lambda i, nblk: (i // tiles_per_seq, 0, 0,
                                                   jnp.maximum(i % tiles_per_seq - (tiles_per_seq - nblk), 0)), nblk=nblk)))
        else:
            assert w >= seq_len
            qkv_shapes.append(jax.ShapeDtypeStruct((m, 3 * DIL_WIDTH), act_dtype))
            qkv_specs.append(pl.BlockSpec((tm, 3 * DIL_WIDTH), row))
            kv_shapes.append(jax.ShapeDtypeStruct((m, 2 * DIL_WIDTH), F32))
            kv_specs.append(pl.BlockSpec((tm, 2 * DIL_WIDTH), row))
    est = (2 * _nbytes((tm, D_MODEL), F32) + _nbytes(w_dil.shape, BF16)
           + 2 * N_GROUPS * _nbytes((tm, 3 * DIL_WIDTH), act_dtype)
           + 2 * sum(_nbytes(s.block_shape, F32) for s in kv_specs) + 3 * _nbytes((tm, DIL_WIDTH), F32))
    scratch.append(pltpu.VMEM((2, tm, LANES), F32))
    if regroup:
        assert DIL_GROUPS[0][1] == 1 and all(r > 1 for _, r in DIL_GROUPS[1:])
        scratch.append(pltpu.VMEM((N_GROUPS - 1, 3 * DIL_WIDTH // LANES, tm, LANES), F32))
        est += _nbytes((N_GROUPS - 1, tm, 3 * DIL_WIDTH), F32)
    body = functools.partial(_dil_proj_body, tm=tm, seq_len=seq_len, pos_offset=pos_offset, regroup=regroup)
    outs = pl.pallas_call(
        body,
        grid=(m // tm,),
        in_specs=[pl.BlockSpec((tm, D_MODEL), row), _resident(w_dil.shape), _resident(inv_freq.shape)],
        out_specs=qkv_specs + kv_specs,
        out_shape=qkv_shapes + kv_shapes,
        scratch_shapes=scratch,
        compiler_params=pltpu.CompilerParams(dimension_semantics=("arbitrary",), vmem_limit_bytes=_vmem_limit(est)),
        name="dil_proj",
    )(x2d, w_dil, inv_freq)
    return outs[:N_GROUPS], outs[N_GROUPS:]


def _dil_proj_short_body(x_ref, wq_ref, wk_ref, wv_ref, inv_ref, qkv_ref, kv_ref, stage_ref, *, seq_len, pos_offset):
    m = x_ref.shape[0]
    n_seq = m // seq_len
    xb = x_ref[...].astype(BF16)
    half = DIL_HD // 2
    row = lax.broadcasted_iota(jnp.int32, (m, LANES), 0)
    lane = lax.broadcasted_iota(jnp.int32, (m, LANES), 1)
    ang = (pos_offset + _mod(row, seq_len)).astype(F32) * inv_ref[...]
    cos, sin = jnp.cos(ang), jnp.sin(ang)
    first_half = _mod(lane, DIL_HD) < half
    sin_signed = jnp.where(first_half, -sin, sin)
    n_slab = DIL_WIDTH // LANES
    for part, w_ref in enumerate((wq_ref, wk_ref, wv_ref)):
        h = _dot_nt(xb, w_ref[...])
        for c in range(n_slab):
            xc = h[:, c * LANES:(c + 1) * LANES]
            if part < 2:
                partner = jnp.where(first_half, pltpu.roll(xc, LANES - half, 1), pltpu.roll(xc, half, 1))
                xc = (xc * cos + partner * sin_signed) * (DIL_HD ** -0.5 if part == 0 else 1.0)
            j = part * n_slab + c
            qkv_ref[0, :, j * LANES:(j + 1) * LANES] = xc
            if part:
                stage_ref[j - n_slab] = xc
    for t in range(seq_len):
        for j in range(2 * n_slab):
            kv_ref[0, t, j * LANES:(j + 1) * LANES, :] = stage_ref[j, pl.ds(t, n_seq, stride=seq_len), :].T


def _dil_proj_short(x2d, w_dil, inv_freq, *, seq_len, pos_offset):
    m = x2d.shape[0]
    n_seq = m // seq_len
    assert n_seq % SUBLANES == 0 and all(w >= seq_len for w, _ in DIL_GROUPS)
    w_spec = lambda part: pl.BlockSpec((DIL_WIDTH, D_MODEL), lambda g: (part * N_GROUPS + g, 0))
    est = (_nbytes((m, D_MODEL), F32) + 6 * _nbytes((DIL_WIDTH, D_MODEL), BF16) + 2 * _nbytes((m, 3 * DIL_WIDTH), F32)
           + 2 * _nbytes((seq_len, 2 * DIL_WIDTH, n_seq), F32) + _nbytes((m, 2 * DIL_WIDTH), F32) + 4 * _nbytes((m, DIL_WIDTH), F32))
    return pl.pallas_call(
        functools.partial(_dil_proj_short_body, seq_len=seq_len, pos_offset=pos_offset),
        grid=(N_GROUPS,),
        in_specs=[_resident((m, D_MODEL)), w_spec(0), w_spec(1), w_spec(2), _resident(inv_freq.shape)],
        out_specs=[pl.BlockSpec((1, m, 3 * DIL_WIDTH), lambda g: (g, 0, 0)),
                   pl.BlockSpec((1, seq_len, 2 * DIL_WIDTH, n_seq), lambda g: (g, 0, 0, 0))],
        out_shape=[jax.ShapeDtypeStruct((N_GROUPS, m, 3 * DIL_WIDTH), F32),
                   jax.ShapeDtypeStruct((N_GROUPS, seq_len, 2 * DIL_WIDTH, n_seq), F32)],
        scratch_shapes=[pltpu.VMEM((2 * DIL_WIDTH // LANES, m, LANES), F32)],
        compiler_params=pltpu.CompilerParams(dimension_semantics=("arbitrary",), vmem_limit_bytes=_vmem_limit(est)),
        name="dil_proj_short",
    )(x2d, w_dil, w_dil, w_dil, inv_freq)


def _cumsum_rows(x):
    n = x.shape[0]
    row = lax.broadcasted_iota(jnp.int32, x.shape, 0)
    s = 1
    while s < n:
        x = x + jnp.where(row >= s, pltpu.roll(x, s, 0), 0.0)
        s *= 2
    return x


def _pad_rows(x, n):
    if x.shape[0] == n:
        return x
    return jnp.concatenate([x, jnp.zeros((n - x.shape[0], x.shape[1]), x.dtype)], axis=0)


def _gla_scan_body(q_ref, k_ref, v_ref, g_ref, ld_ref, s0_ref, nw_ref, o_ref, sout_ref, *scratch, nseq, cc, carry):
    chunk = pl.program_id(1)
    sub = GLA_SUB
    cp = cc if cc % sub == 0 else sub * (cc // sub + 1)
    nsub = cp // sub
    st_ref = scratch[0] if carry else None

    if carry:
        @pl.when(chunk == 0)
        def _():
            for s in range(nseq):
                for h in range(GLA_HEADS):
                    st_ref[s * GLA_HEADS + h] = s0_ref[s, h].T

    arow = lax.broadcasted_iota(jnp.int32, (cp, cp), 0)
    acol = lax.broadcasted_iota(jnp.int32, (cp, cp), 1)
    for s in range(nseq):
        q = _pad_rows(q_ref[s].astype(F32), cp)
        k = _pad_rows(k_ref[s].astype(F32), cp)
        b = _cumsum_rows(_pad_rows(ld_ref[s], cp))
        b_last = b[cp - 1:cp, :]
        starts = [jnp.zeros((1, GLA_QK), F32)] + [b[i * sub - 1:i * sub, :] for i in range(1, nsub)]
        ends = starts[1:] + [b_last]
        rows_of = lambda vs: jnp.concatenate([jnp.broadcast_to(v, (sub, GLA_QK)) for v in vs], axis=0)
        r, e_own = rows_of(starts), rows_of(ends)
        qh = q * jnp.exp(b - r)
        qt = (qh * jnp.exp(r)).astype(BF16)
        kl32 = k * jnp.exp(b_last - b)
        kl = kl32.astype(BF16)
        k_end = k * jnp.exp(e_own - b)
        k_diag = (k * jnp.exp(r - b)).astype(BF16)
        a_rows = [[] for _ in range(GLA_HEADS)]
        for i in range(nsub):
            pieces = []
            if i:
                hop = jnp.exp(starts[i] - jnp.concatenate(ends[:i], axis=0))
                pieces = [(k_end[j * sub:(j + 1) * sub, :] * hop[j:j + 1, :]).astype(BF16) for j in range(i)]
            pieces.append(k_diag[i * sub:(i + 1) * sub, :])
            if i + 1 < nsub:
                pieces.append(jnp.zeros((cp - (i + 1) * sub, GLA_QK), BF16))
            ki = jnp.concatenate(pieces, axis=0) if len(pieces) > 1 else pieces[0]
            qi = qh[i * sub:(i + 1) * sub, :].astype(BF16)
            for h in range(GLA_HEADS):
                hs = slice(h * GLA_DK, (h + 1) * GLA_DK)
                a_rows[h].append(_dot_nt(qi[:, hs], ki[:, hs]))
        d_last = jnp.exp(b_last)
        for h in range(GLA_HEADS):
            hs = slice(h * GLA_DK, (h + 1) * GLA_DK)
            vs = slice(h * GLA_DV, (h + 1) * GLA_DV)
            a = jnp.where(acol <= arow, jnp.concatenate(a_rows[h], axis=0), 0.0).astype(BF16)
            v = _pad_rows(v_ref[s][:, vs].astype(F32), cp)
            if carry:
                st = st_ref[s * GLA_HEADS + h]
                inter = _dot_nt(qt[:, hs], st.astype(BF16))
            else:
                st = s0_ref[s, h]
                inter = _dot(qt[:, hs], st.astype(BF16))
            o = (inter + _dot(a, v.astype(BF16)))[:cc]
            o = o * lax.rsqrt(jnp.mean(o * o, axis=-1, keepdims=True) + NORM_EPS) * nw_ref[...]
            o_ref[s, :, vs] = (o * g_ref[s][:, vs].astype(F32)).astype(o_ref.dtype)
            v_tile = _pad_rows(v, GLA_CHUNK)
            if carry:
                st_ref[s * GLA_HEADS + h] = d_last[:, hs] * st + _dot(v_tile.T.astype(BF16), _pad_rows(kl[:, hs], GLA_CHUNK))
            else:
                aug = jnp.concatenate([kl32[:, hs], jnp.broadcast_to(d_last[:, hs], (SUBLANES, GLA_DK)),
                                       jnp.zeros((GLA_CHUNK - cp - SUBLANES, GLA_DK), F32)], axis=0).T
                sout_ref[s, h] = aug[:, cp:cp + 1] * st + _dot(aug.astype(BF16), v_tile.astype(BF16))

    if carry:
        @pl.when(chunk == pl.num_programs(1) - 1)
        def _():
            for s in range(nseq):
                for h in range(GLA_HEADS):
                    sout_ref[s, h] = st_ref[s * GLA_HEADS + h].T


def _gla_scan(q, k, v, g, ld, s0, norm_w, *, nseq, cc):
    n_seq, t, _ = q.shape
    assert n_seq % nseq == 0 and t % cc == 0 and cc <= GLA_CHUNK
    blk = lambda sg, c: (sg, c, 0)
    sblk = lambda sg, c: (sg, 0, 0, 0)
    carry = not (t == cc and cc + GLA_SUB + SUBLANES <= GLA_CHUNK)
    body = functools.partial(_gla_scan_body, nseq=nseq, cc=cc, carry=carry)
    est = (2 * nseq * cc * (2 * GLA_QK + 2 * GLA_V) * jnp.dtype(q.dtype).itemsize + 2 * nseq * cc * GLA_QK * 4
           + 2 * nseq * cc * GLA_V * jnp.dtype(q.dtype).itemsize + 5 * nseq * GLA_HEADS * GLA_DK * GLA_DV * 4)
    return pl.pallas_call(
        body,
        grid=(n_seq // nseq, t // cc),
        in_specs=[pl.BlockSpec((nseq, cc, GLA_QK), blk), pl.BlockSpec((nseq, cc, GLA_QK), blk),
                  pl.BlockSpec((nseq, cc, GLA_V), blk), pl.BlockSpec((nseq, cc, GLA_V), blk),
                  pl.BlockSpec((nseq, cc, GLA_QK), blk),
                  pl.BlockSpec((nseq, GLA_HEADS, GLA_DK, GLA_DV), sblk),
                  pl.BlockSpec(norm_w.shape, lambda sg, c: (0, 0))],
        out_specs=[pl.BlockSpec((nseq, cc, GLA_V), blk), pl.BlockSpec((nseq, GLA_HEADS, GLA_DK, GLA_DV), sblk)],
        out_shape=[jax.ShapeDtypeStruct((n_seq, t, GLA_V), q.dtype),
                   jax.ShapeDtypeStruct((n_seq, GLA_HEADS, GLA_DK, GLA_DV), F32)],
        scratch_shapes=[pltpu.VMEM((nseq * GLA_HEADS, GLA_DV, GLA_DK), F32)] if carry else [],
        compiler_params=pltpu.CompilerParams(dimension_semantics=("arbitrary", "arbitrary"),
                                             vmem_limit_bytes=_vmem_limit(est)),
        name="gla_scan",
    )(q, k, v, g, ld, s0, norm_w)


def _dil_attn_block(ins, outs, prev_ref, carry_ref, n, blocks_per_residue):
    row = lax.broadcasted_iota(jnp.int32, (ATT_BLK, 2 * ATT_BLK), 0)
    col = lax.broadcasted_iota(jnp.int32, (ATT_BLK, 2 * ATT_BLK), 1)
    dist = row + ATT_BLK - col
    band = (dist >= 0) & (dist <= ATT_BLK)
    lane = lax.broadcasted_iota(jnp.int32, (ATT_BLK, LANES), 1)
    for g in range(N_GROUPS):
        q_ref, kc_ref, vc_ref = ins[3 * g:3 * g + 3]
        o_ref, lse_ref = outs[2 * g:2 * g + 2]
        jb = _mod(n, blocks_per_residue[g])
        valid = band & ((col >= ATT_BLK) | (jb > 0))
        q, k_cur, v_cur = q_ref[0, 0], kc_ref[0, 0], vc_ref[0, 0]
        kk = jnp.concatenate([prev_ref[g, 0], k_cur], axis=0)
        vv = jnp.concatenate([prev_ref[g, 1], v_cur], axis=0)
        carry_ref[g, 0] = k_cur
        carry_ref[g, 1] = v_cur
        lse_tile = jnp.zeros((ATT_BLK, LANES), F32)
        for p in range(DIL_WIDTH // LANES):
            ls = slice(p * LANES, (p + 1) * LANES)
            q2, k2, v2 = q[:, ls], kk[:, ls], vv[:, ls]
            o_pair = jnp.zeros((ATT_BLK, LANES), F32)
            for e in range(LANES // DIL_HD):
                sel = (lane >= DIL_HD) if e else (lane < DIL_HD)
                s = _dot_nt(jnp.where(sel, q2, jnp.zeros_like(q2)), k2)
                s = jnp.where(valid, s, NEG_FILL)
                m = jnp.max(s, axis=-1, keepdims=True)
                pe = jnp.exp(s - m)
                den = jnp.sum(pe, axis=-1, keepdims=True)
                pv = _dot(pe.astype(BF16), v2)
                o_pair = jnp.where(sel, pv * (1.0 / den), o_pair)
                lse_tile = jnp.where(lane == p * (LANES // DIL_HD) + e, m + jnp.log(den), lse_tile)
            o_ref[0, 0, :, ls] = o_pair.astype(o_ref.dtype)
        lse_ref[0, 0] = lse_tile


def _dec_key_masks(mask_ref, t_new):
    nq = DIL_HEADS * t_new
    off = 0
    for w, r in DIL_GROUPS:
        t_q = _mod(lax.broadcasted_iota(jnp.int32, (nq, w), 0), t_new)
        j = lax.broadcasted_iota(jnp.int32, (nq, w), 1)
        mask_ref[:, off:off + w] = ((j >= t_q) & (_mod(w + t_q - j, r) == 0)).astype(F32)
        t_n = _mod(lax.broadcasted_iota(jnp.int32, (nq, LANES), 0), t_new)
        u = lax.broadcasted_iota(jnp.int32, (nq, LANES), 1)
        ok_n = (u <= t_n) & (_mod(t_n - u, r) == 0)
        mask_ref[:, off + w:off + w + LANES] = ok_n.astype(F32)
        off += w + LANES


def _dec_attn_seq(qkv_refs, c_refs, o_ref, mask_ref, t_new):
    nq = DIL_HEADS * t_new
    off = 0
    lane_w = lax.broadcasted_iota(jnp.int32, (nq, DIL_WIDTH), 1)
    row_w = lax.broadcasted_iota(jnp.int32, (nq, DIL_WIDTH), 0)
    own_head = _div(lane_w, DIL_HD) == _div(row_w, t_new)
    scores, values = [], []
    for g, (w, r) in enumerate(DIL_GROUPS):
        qkv = qkv_refs[g][0, 0].astype(F32)
        q, k_new, v_new = qkv[:, :DIL_WIDTH], qkv[:, DIL_WIDTH:2 * DIL_WIDTH], qkv[:, 2 * DIL_WIDTH:]
        q_rep = jnp.broadcast_to(q[None], (DIL_HEADS, t_new, DIL_WIDTH)).reshape(nq, DIL_WIDTH)
        q_bd = jnp.where(own_head, q_rep, 0.0).astype(BF16)
        kt_c, vt_c = c_refs[g][0].astype(BF16), c_refs[g][1].astype(BF16)
        ok_c = mask_ref[:, off:off + w] > 0.5
        ok_n = mask_ref[:, off + w:off + w + LANES] > 0.5
        off += w + LANES
        scores.append(jnp.where(ok_c, _dot(q_bd, kt_c), NEG_FILL))
        values.append(vt_c)
        scores.append(jnp.where(ok_n, _dot_nt(q_bd, _pad_rows(k_new, LANES).astype(BF16)), NEG_FILL))
        values.append(_pad_rows(v_new, LANES).astype(BF16))
    m = functools.reduce(jnp.maximum, [jnp.max(s, axis=-1, keepdims=True) for s in scores])
    den = jnp.zeros((nq, 1), F32)
    acc = jnp.zeros((nq, DIL_WIDTH), F32)
    for idx, (s, v) in enumerate(zip(scores, values)):
        pe = jnp.exp(s - m)
        den = den + jnp.sum(pe, axis=-1, keepdims=True)
        pe = pe.astype(BF16)
        acc = acc + (_dot(pe, v) if idx % 2 else _dot_nt(pe, v))
    acc = jnp.where(own_head, acc * (1.0 / den), 0.0)
    o_ref[0] = jnp.sum(acc.reshape(DIL_HEADS, t_new, DIL_WIDTH), axis=0)


def _cache_copy(c_hbm, buf, sem, g, seq, slot):
    return pltpu.make_async_copy(c_hbm[g].at[seq], buf[g].at[slot], sem.at[g, slot])


def _attention_body(*refs, blocks_per_residue, n_blocks, t_new):
    n_p = 3 * N_GROUPS
    ins_p, qkv_s, c_hbm = refs[:n_p], refs[n_p:n_p + N_GROUPS], refs[n_p + N_GROUPS:n_p + 2 * N_GROUPS]
    outs_p, o_s = refs[n_p + 2 * N_GROUPS:n_p + 4 * N_GROUPS], refs[n_p + 4 * N_GROUPS]
    bufs, sem = refs[n_p + 4 * N_GROUPS + 1:n_p + 5 * N_GROUPS + 1], refs[n_p + 5 * N_GROUPS + 1]
    prev_ref, mask_ref = refs[n_p + 5 * N_GROUPS + 2], refs[n_p + 5 * N_GROUPS + 3]
    i, n_steps = pl.program_id(0), pl.num_programs(0)

    @pl.when(i == 0)
    def _():
        for s in range(CACHE_RING - 1):
            for g in range(N_GROUPS):
                _cache_copy(c_hbm, bufs, sem, g, s, s).start()
        prev_ref[...] = jnp.zeros(prev_ref.shape, prev_ref.dtype)
        _dec_key_masks(mask_ref, t_new)

    ahead = i + (CACHE_RING - 1)

    @pl.when(ahead < n_steps)
    def _():
        for g in range(N_GROUPS):
            _cache_copy(c_hbm, bufs, sem, g, ahead, ahead % CACHE_RING).start()

    slot = i % CACHE_RING
    for g in range(N_GROUPS):
        _cache_copy(c_hbm, bufs, sem, g, i, slot).wait()
    parity = _mod(i, 2)
    _dil_attn_block(ins_p, outs_p, prev_ref.at[1 - parity], prev_ref.at[parity], _mod(i, n_blocks), blocks_per_residue)
    _dec_attn_seq(qkv_s, [bufs[g].at[slot] for g in range(N_GROUPS)], o_s, mask_ref, t_new)


def _attention(qkv_p, qkv_s, caches, *, n_seq_p, seq_len, n_seq_s, t_new):
    args, in_specs, out_specs, out_shape, bpr, scratch = [], [], [], [], [], []
    n_blocks = seq_len // ATT_BLK
    assert n_seq_s == n_seq_p * n_blocks
    assert n_seq_s >= CACHE_RING
    for g, (w, r) in enumerate(DIL_GROUPS):
        assert w // r == ATT_BLK and seq_len % (r * ATT_BLK) == 0
        nb = n_blocks // r
        bpr.append(nb)
        cur = lambda i, nb, part: (i // n_blocks, (i % n_blocks) // nb, i % nb, part)
        blk = (1, 1, ATT_BLK, DIL_WIDTH)
        for part in range(3):
            args.append(qkv_p[g])
            in_specs.append(pl.BlockSpec(blk, functools.partial(cur, nb=nb, part=part)))
        omap = functools.partial(cur, nb=nb, part=0)
        out_specs += [pl.BlockSpec(blk, omap), pl.BlockSpec((1, 1, ATT_BLK, LANES), omap)]
        out_shape += [jax.ShapeDtypeStruct((n_seq_p, r, seq_len // r, DIL_WIDTH), qkv_p[g].dtype),
                      jax.ShapeDtypeStruct((n_seq_p, r, seq_len // r, LANES), F32)]
    est = 2 * N_GROUPS * (6 * _nbytes((ATT_BLK, DIL_WIDTH), qkv_p[0].dtype) + _nbytes((ATT_BLK, LANES), F32))
    for g in range(N_GROUPS):
        args.append(qkv_s.reshape(N_GROUPS, n_seq_s, t_new, 3 * DIL_WIDTH))
        in_specs.append(pl.BlockSpec((1, 1, t_new, 3 * DIL_WIDTH), functools.partial(lambda i, g: (g, i, 0, 0), g=g)))
    for g, (w, r) in enumerate(DIL_GROUPS):
        args.append(jnp.transpose(caches[g], (0, 2, 3, 4, 1)).reshape(n_seq_s, 2, DIL_WIDTH, w))
        in_specs.append(pl.BlockSpec(memory_space=pl.ANY))
        scratch.append(pltpu.VMEM((CACHE_RING, 2, DIL_WIDTH, w), F32))
        est += CACHE_RING * _nbytes((2, DIL_WIDTH, w), F32) + 3 * _nbytes((DIL_HEADS * t_new + DIL_WIDTH, w), F32)
    scratch.append(pltpu.SemaphoreType.DMA((N_GROUPS, CACHE_RING)))
    scratch.append(pltpu.VMEM((2, N_GROUPS, 2, ATT_BLK, DIL_WIDTH), qkv_p[0].dtype))
    scratch.append(pltpu.VMEM((DIL_HEADS * t_new, sum(w + LANES for w, _ in DIL_GROUPS)), F32))
    out_specs.append(pl.BlockSpec((1, t_new, DIL_WIDTH), lambda i: (i, 0, 0)))
    out_shape.append(jax.ShapeDtypeStruct((n_seq_s, t_new, DIL_WIDTH), F32))
    outs = pl.pallas_call(
        functools.partial(_attention_body, blocks_per_residue=tuple(bpr), n_blocks=n_blocks, t_new=t_new),
        grid=(n_seq_s,),
        in_specs=in_specs, out_specs=out_specs, out_shape=out_shape, scratch_shapes=scratch,
        compiler_params=pltpu.CompilerParams(dimension_semantics=("arbitrary",), vmem_limit_bytes=_vmem_limit(est)),
        name="attention",
    )(*args)
    return ([outs[2 * g] for g in range(N_GROUPS)], [outs[2 * g + 1] for g in range(N_GROUPS)],
            outs[2 * N_GROUPS].reshape(n_seq_s * t_new, DIL_WIDTH))


def _row_order(src_ref, stage_ref, r):
    if r == 1:
        return src_ref[0, 0].astype(F32)
    n = src_ref.shape[2]
    n_slab = src_ref.shape[3] // LANES
    for c in range(r):
        for j in range(n_slab):
            stage_ref[j, pl.ds(c, n, stride=r), :] = src_ref[0, c, :, j * LANES:(j + 1) * LANES].astype(F32)
    slabs = [stage_ref[j] for j in range(n_slab)]
    return slabs[0] if n_slab == 1 else jnp.concatenate(slabs, axis=1)


def _merge_body(*refs, n_dil_inputs):
    x_ref, og_ref = refs[:2]
    dil_refs = refs[2:2 + n_dil_inputs]
    (wg_ref, wa_ref, wb_ref, wo_ref, ex_ref, g_ref, b_ref, y_ref) = refs[2 + n_dil_inputs:10 + n_dil_inputs]
    x = x_ref[...]
    xb = x.astype(BF16)
    if n_dil_inputs == 1:
        yb_in = dil_refs[0][...].astype(F32)
    else:
        ostage_ref, lstage_ref = refs[10 + n_dil_inputs:]
        stages = [(None, None)] + [(ostage_ref.at[g - 1], lstage_ref.at[g - 1]) for g in range(1, N_GROUPS)]
        os_ = [_row_order(dil_refs[g], stages[g][0], DIL_GROUPS[g][1]) for g in range(N_GROUPS)]
        lses = [_row_order(dil_refs[N_GROUPS + g], stages[g][1], DIL_GROUPS[g][1]) for g in range(N_GROUPS)]
        top = functools.reduce(jnp.maximum, lses)
        es = [jnp.exp(l - top) for l in lses]
        inv = 1.0 / functools.reduce(jnp.add, es)
        yb_in = jnp.zeros(os_[0].shape, F32)
        for o, e in zip(os_, es):
            wgt = e * inv
            hi = wgt.astype(BF16)
            lo = (wgt - hi.astype(F32)).astype(BF16)
            wide = _dot(hi, ex_ref[...]) + _dot(lo, ex_ref[...])
            yb_in = yb_in + wide * o
    ya = _dot(og_ref[...].astype(BF16), wa_ref[...])
    yb = _dot(yb_in.astype(BF16), wb_ref[...])
    ga = _dot_nt(xb, wg_ref[:D_MODEL, :])
    gb = _dot_nt(xb, wg_ref[D_MODEL:, :])
    mix = _sigmoid(ga) * ya + _sigmoid(gb) * yb
    z = ALPHA * x + _dot(mix.astype(BF16), wo_ref[...])
    y_ref[...] = _layer_norm(z, g_ref[...], b_ref[...])


def _merge(x2d, o_gla, dil_inputs, w_gates, w_br_gla, w_br_dil, w_out, expand, ln_g, ln_b, *, tm, seq_len):
    m = x2d.shape[0]
    row = lambda i: (i, 0)
    tiles_per_seq = max(seq_len // tm, 1)
    acts, act_specs, scratch = [x2d, o_gla], [pl.BlockSpec((tm, D_MODEL), row), pl.BlockSpec((tm, GLA_V), row)], []
    est = 2 * _nbytes((tm, D_MODEL), F32) + 2 * _nbytes((tm, GLA_V), o_gla.dtype)
    for a in dil_inputs:
        acts.append(a)
        if a.ndim == 2:
            act_specs.append(pl.BlockSpec((tm, a.shape[1]), row))
        else:
            r = a.shape[1]
            assert seq_len % tm == 0 and tm % r == 0
            act_specs.append(pl.BlockSpec((1, r, tm // r, a.shape[3]), lambda i: (i // tiles_per_seq, 0, i % tiles_per_seq, 0)))
        est += 2 * _nbytes((tm, a.shape[-1]), a.dtype)
    if len(dil_inputs) > 1:
        scratch = [pltpu.VMEM((N_GROUPS - 1, DIL_WIDTH // LANES, tm, LANES), F32),
                   pltpu.VMEM((N_GROUPS - 1, 1, tm, LANES), F32)]
        est += (N_GROUPS - 1) * _nbytes((tm, DIL_WIDTH + LANES), F32)
    weights = [w_gates, w_br_gla, w_br_dil, w_out, expand, ln_g, ln_b]
    est += sum(_nbytes(w.shape, w.dtype) for w in weights) + 6 * _nbytes((tm, D_MODEL), F32)
    return pl.pallas_call(
        functools.partial(_merge_body, n_dil_inputs=len(dil_inputs)),
        grid=(m // tm,),
        in_specs=act_specs + [_resident(w.shape) for w in weights],
        out_specs=pl.BlockSpec((tm, D_MODEL), row),
        out_shape=jax.ShapeDtypeStruct((m, D_MODEL), F32),
        scratch_shapes=scratch,
        compiler_params=pltpu.CompilerParams(dimension_semantics=("arbitrary",), vmem_limit_bytes=_vmem_limit(est)),
        name="merge",
    )(*acts, *weights)


def _ffn_body(*refs, tm, tiles_per_seq, seq_len, tail_rows):
    long_seq = seq_len >= tm
    if long_seq:
        x_ref, halo_ref, cp_ref, pe_ref = refs[:4]
    else:
        x_ref, cp0_ref, cp1_ref, pe_ref = refs[:4]
    (wup_ref, cw_ref, cb_ref, wdn_ref, g2_ref, b2_ref, wpg_ref, wpp_ref, g3_ref, b3_ref, y_ref, tail_ref) = refs[4:]
    i = pl.program_id(0)
    x = x_ref[...]
    xb = x.astype(BF16)
    row = lax.broadcasted_iota(jnp.int32, (tm, FF_CHUNK), 0)
    acc = jnp.zeros((tm, D_MODEL), F32)
    for c in range(D_FF // FF_CHUNK):
        cs = slice(c * FF_CHUNK, (c + 1) * FF_CHUNK)
        a = _dot(xb, wup_ref[:, cs])
        u = _dot(xb, wup_ref[:, D_FF + c * FF_CHUNK:D_FF + (c + 1) * FF_CHUNK])
        if long_seq:
            a_halo = _dot(halo_ref[...].astype(BF16), wup_ref[:, cs])
            prev = jnp.where(i % tiles_per_seq == 0, cp_ref[0][:, cs], a_halo)
            p1, p2 = prev[SUBLANES - 1:SUBLANES, :], prev[SUBLANES - 2:SUBLANES - 1, :]
            t = row
        else:
            p1, p2 = cp1_ref[:, cs], cp0_ref[:, cs]
            t = _mod(row, seq_len)
        am1 = jnp.where(t == 0, p1, pltpu.roll(a, 1, 0))
        am2 = jnp.where(t == 0, p2, jnp.where(t == 1, p1, pltpu.roll(a, 2, 0)))
        cw = cw_ref[:, cs]
        conv = cb_ref[:, cs] + cw[0:1, :] * am2 + cw[1:2, :] * am1 + cw[2:3, :] * a
        gelu = 0.5 * conv * (1.0 + lax.erf(conv * (2.0 ** -0.5)))
        acc = acc + _dot((gelu * u).astype(BF16), wdn_ref[cs, :])
        tail_ref[:, cs] = a[tm - tail_rows:, :]
    x2 = _layer_norm(ALPHA * x + acc, g2_ref[...], b2_ref[...])
    gate = _sigmoid(_dot(x2.astype(BF16), wpg_ref[...]))
    emb = _dot(pe_ref[...].astype(BF16), wpp_ref[...])
    y_ref[...] = _layer_norm(ALPHA * x2 + gate * emb, g3_ref[...], b3_ref[...])


def _ffn(x2d, conv_prev, pe2d, w_up, conv_w, conv_b, w_down, ln2_g, ln2_b, w_pg, w_pp, ln3_g, ln3_b, *, tm, seq_len):
    m = x2d.shape[0]
    n_seq = m // seq_len
    row = lambda i: (i, 0)
    n_tiles = m // tm
    if seq_len >= tm:
        tiles_per_seq, tail_rows = seq_len // tm, SUBLANES
        cp = jnp.concatenate([jnp.zeros((n_seq, SUBLANES - (CONV_W - 1), D_FF), F32), conv_prev], axis=1)
        acts = [x2d, x2d, cp, pe2d]
        act_specs = [pl.BlockSpec((tm, D_MODEL), row),
                     pl.BlockSpec((SUBLANES, D_MODEL), lambda i: (jnp.maximum(i * (tm // SUBLANES) - 1, 0), 0)),
                     pl.BlockSpec((1, SUBLANES, D_FF), lambda i: (i // tiles_per_seq, 0, 0)),
                     pl.BlockSpec((tm, PLE_DIM), row)]
        act_bytes = 2 * _nbytes((tm, D_MODEL), F32) + _nbytes((SUBLANES, D_FF), F32)
    else:
        assert tm % seq_len == 0 and seq_len >= CONV_W - 1
        tiles_per_seq, tail_rows = 1, tm
        cp0 = jnp.repeat(conv_prev[:, 0, :], seq_len, axis=0)
        cp1 = jnp.repeat(conv_prev[:, 1, :], seq_len, axis=0)
        acts = [x2d, cp0, cp1, pe2d]
        act_specs = [pl.BlockSpec((tm, D_MODEL), row), pl.BlockSpec((tm, D_FF), row), pl.BlockSpec((tm, D_FF), row),
                     pl.BlockSpec((tm, PLE_DIM), row)]
        act_bytes = _nbytes((tm, D_MODEL), F32) + 2 * _nbytes((tm, D_FF), F32)
    weights = [w_up, conv_w, conv_b, w_down, ln2_g, ln2_b, w_pg, w_pp, ln3_g, ln3_b]
    est = (2 * act_bytes + sum(_nbytes(w.shape, w.dtype) for w in weights) + 4 * _nbytes((tm, D_MODEL), F32)
           + 2 * _nbytes((tail_rows, D_FF), F32) + 6 * _nbytes((tm, FF_CHUNK), F32))
    y, tail = pl.pallas_call(
        functools.partial(_ffn_body, tm=tm, tiles_per_seq=tiles_per_seq, seq_len=seq_len, tail_rows=tail_rows),
        grid=(n_tiles,),
        in_specs=act_specs + [_resident(w.shape) for w in weights],
        out_specs=[pl.BlockSpec((tm, D_MODEL), row), pl.BlockSpec((tail_rows, D_FF), row)],
        out_shape=[jax.ShapeDtypeStruct((m, D_MODEL), F32), jax.ShapeDtypeStruct((n_tiles * tail_rows, D_FF), F32)],
        compiler_params=pltpu.CompilerParams(dimension_semantics=("arbitrary",), vmem_limit_bytes=_vmem_limit(est)),
        name="ffn",
    )(*acts, *weights)
    return y, tail


def _tile_rows(m):
    return min(m, 512)


def _mixers(x, s0, pos_offset, wts, *, prompt):
    n_seq, t, _ = x.shape
    m = n_seq * t
    x2d = x.reshape(m, D_MODEL)
    act_dtype = BF16 if prompt else F32
    tm = _tile_rows(m)
    q, k, v, g, ld = _gla_proj(x2d, wts["w_gla"], wts["w_lr"], wts["w_gkb"], wts["b_gk"], tm=min(m, 2 * tm),
                               act_dtype=act_dtype)
    if prompt:
        qkv, kv_new = _dil_proj(x2d, wts["w_dil"], wts["inv_freq"], tm=tm, seq_len=t, pos_offset=pos_offset, act_dtype=act_dtype)
        kv_out = [jnp.transpose(kvn.reshape(n_seq, 2, DIL_HEADS, DIL_HD, -1), (0, 4, 1, 2, 3)) for kvn in kv_new]
    else:
        qkv, kv_new = _dil_proj_short(x2d, wts["w_dil"], wts["inv_freq"], seq_len=t, pos_offset=pos_offset)
        kv_out = [jnp.transpose(kv_new[g].reshape(t, 2, DIL_HEADS, DIL_HD, n_seq), (4, 0, 1, 2, 3)) for g in range(N_GROUPS)]
    r3 = lambda a: a.reshape(n_seq, t, a.shape[-1])
    o_gla, s_new = _gla_scan(r3(q), r3(k), r3(v), r3(g), r3(ld), s0, wts["gla_norm"],
                             nseq=n_seq if prompt else 8, cc=min(t, GLA_CHUNK))
    return dict(x2d=x2d, tm=tm, n_seq=n_seq, t=t, o_gla=o_gla.reshape(m, GLA_V), s_new=s_new, qkv=qkv, kv_out=kv_out)


def _finish(mix, dil_inputs, pe, conv_prev, wts, *, prompt):
    n_seq, t, tm = mix["n_seq"], mix["t"], mix["tm"]
    m = n_seq * t
    x1 = _merge(mix["x2d"], mix["o_gla"], dil_inputs, wts["w_gates"], wts["w_br_gla"], wts["w_br_dil"], wts["w_out"],
                wts["expand"], wts["ln1_g"], wts["ln1_b"], tm=tm, seq_len=t)
    tm_ffn = tm if prompt else min(tm, 256)
    y, tail = _ffn(x1, conv_prev, pe.reshape(m, PLE_DIM), wts["w_up"], wts["conv_w"], wts["conv_b"], wts["w_down"],
                   wts["ln2_g"], wts["ln2_b"], wts["w_ple_gate"], wts["w_ple_proj"], wts["ln3_g"], wts["ln3_b"],
                   tm=tm_ffn, seq_len=t)
    if t >= tm_ffn:
        conv_new = tail.reshape(n_seq, t // tm_ffn, SUBLANES, D_FF)[:, -1, SUBLANES - (CONV_W - 1):, :]
    else:
        conv_new = tail.reshape(n_seq, t, D_FF)[:, t - (CONV_W - 1):, :]
    return y.reshape(n_seq, t, D_MODEL), conv_new


def _forward(x_p, x_s, pe_p, pe_s, s0_s, conv_s, caches, wts):
    n_p = x_p.shape[0]
    mix_p = _mixers(x_p, jnp.zeros((n_p, GLA_HEADS, GLA_DK, GLA_DV), s0_s.dtype), 0, wts, prompt=True)
    mix_s = _mixers(x_s, s0_s, PAST_LEN, wts, prompt=False)
    o_p, lse_p, o_s = _attention(mix_p["qkv"], mix_s["qkv"], caches, n_seq_p=n_p, seq_len=x_p.shape[1],
                                 n_seq_s=x_s.shape[0], t_new=x_s.shape[1])
    y_p, conv_new_p = _finish(mix_p, o_p + lse_p, pe_p, jnp.zeros((n_p, CONV_W - 1, D_FF), x_p.dtype), wts, prompt=True)
    y_s, conv_new_s = _finish(mix_s, [o_s], pe_s, conv_s, wts, prompt=False)
    return (y_p, mix_p["s_new"], conv_new_p, mix_p["kv_out"]), (y_s, mix_s["s_new"], conv_new_s, mix_s["kv_out"])


def _prep_weights(w_in, w_gk_b, b_gk, gla_norm, w_br_gla, w_br_dil, w_out, ln1_g, ln1_b, w_up, conv_w, conv_b, w_down,
                  ln2_g, ln2_b, w_ple_gate, w_ple_proj, ln3_g, ln3_b):
    w_in_t = jnp.transpose(w_in[0])
    rows_bf16 = lambda lo, hi: w_in_t[lo:hi].astype(BF16)
    vec = lambda a: a[0].reshape(1, -1).astype(F32)
    half = DIL_HD // 2
    inv = ROPE_THETA ** (-jnp.arange(half, dtype=F32) / half)
    head_of_lane = jnp.arange(DIL_WIDTH) // DIL_HD
    expand = (jnp.arange(LANES)[:, None] == head_of_lane[None, :]).astype(BF16)
    return {
        "w_gla": rows_bf16(0, COL_GLA),
        "w_lr": jnp.pad(rows_bf16(COL_GLA, COL_LR), ((0, LANES - GLA_RANK), (0, 0))),
        "w_gkb": jnp.pad(w_gk_b[0].astype(BF16), ((0, LANES - GLA_RANK), (0, 0))),
        "b_gk": vec(b_gk),
        "w_dil": rows_bf16(COL_LR, COL_DIL),
        "w_gates": rows_bf16(COL_DIL, w_in_t.shape[0]),
        "inv_freq": jnp.tile(inv, LANES // half).reshape(1, LANES),
        "gla_norm": vec(gla_norm),
        "w_br_gla": w_br_gla[0].astype(BF16), "w_br_dil": w_br_dil[0].astype(BF16), "w_out": w_out[0].astype(BF16),
        "expand": expand,
        "ln1_g": vec(ln1_g), "ln1_b": vec(ln1_b),
        "w_up": w_up[0].astype(BF16),
        "conv_w": jnp.pad(conv_w[0].astype(F32), ((0, SUBLANES - CONV_W), (0, 0))),
        "conv_b": vec(conv_b),
        "w_down": w_down[0].astype(BF16),
        "ln2_g": vec(ln2_g), "ln2_b": vec(ln2_b),
        "w_ple_gate": w_ple_gate[0].astype(BF16), "w_ple_proj": w_ple_proj[0].astype(BF16),
        "ln3_g": vec(ln3_g), "ln3_b": vec(ln3_b),
    }


def kernel(x_prompt, x_sample, p_prompt, p_sample, state_gla, cache_conv, cache_kv_w128, cache_kv_w512, cache_kv_w2048, w_in, w_gk_b, b_gk, gla_norm, w_br_gla, w_br_dil, w_out, ln1_g, ln1_b, w_up, conv_w, conv_b, w_down, ln2_g, ln2_b, w_ple_gate, w_ple_proj, ln3_g, ln3_b):
    assert w_in.shape[0] == DEPTH == 1
    wts = _prep_weights(w_in, w_gk_b, b_gk, gla_norm, w_br_gla, w_br_dil, w_out, ln1_g, ln1_b, w_up, conv_w, conv_b,
                        w_down, ln2_g, ln2_b, w_ple_gate, w_ple_proj, ln3_g, ln3_b)
    caches = (cache_kv_w128[0], cache_kv_w512[0], cache_kv_w2048[0])
    (y_p, s_p, conv_p, kv_p), (y_s, s_s, conv_s, kv_s) = _forward(
        x_prompt, x_sample, p_prompt[0], p_sample[0], state_gla[0], cache_conv[0], caches, wts)
    return (y_p, y_s, s_p[None], s_s[None], conv_p[None], conv_s[None],
            kv_p[0][None], kv_p[1][None], kv_p[2][None], kv_s[0][None], kv_s[1][None], kv_s[2][None])
```

```python
import functools

import jax
import jax.numpy as jnp
from jax import lax
from jax.experimental import pallas as pl
from jax.experimental.pallas import tpu as pltpu

F32 = jnp.float32
BF16 = jnp.bfloat16

D_MODEL = 1024
DEPTH = 1
PAST_LEN = 8192
GLA_HEADS = 4
GLA_DK = 128
GLA_DV = 256
GLA_QK = GLA_HEADS * GLA_DK
GLA_V = GLA_HEADS * GLA_DV
GLA_RANK = 16
GLA_NORMALIZER = 16.0
DIL_GROUPS = ((128, 1), (512, 4), (2048, 16))
N_GROUPS = len(DIL_GROUPS)
DIL_HEADS = 8
DIL_HD = 64
DIL_WIDTH = DIL_HEADS * DIL_HD
DIL_QKV = N_GROUPS * DIL_WIDTH
ROPE_THETA = 10000.0
D_FF = 2816
CONV_W = 3
PLE_DIM = 256
ALPHA = (2.0 * DEPTH) ** 0.25
NORM_EPS = 1e-5
COL_GLA = 2 * GLA_QK + 2 * GLA_V
COL_LR = COL_GLA + GLA_RANK
COL_DIL = COL_LR + 3 * DIL_QKV
NEG_FILL = -1e30

LANES = 128
SUBLANES = 8
VMEM_BYTES_V7X = 64 * 1024 * 1024
VMEM_LIMIT_CAP = 60000 * 1024

ATT_BLK = 128
GLA_CHUNK = 128
GLA_SUB = 16
FF_CHUNK = D_FF // 2
CACHE_RING = 3
REGROUP_STRIDE = 4


def _vmem_limit(nbytes):
    return int(min(max(2 * nbytes, 16 * 1024 * 1024), VMEM_LIMIT_CAP))


def _nbytes(shape, dtype):
    n = 1
    for s in shape:
        n *= s
    return n * jnp.dtype(dtype).itemsize


def _sigmoid(x):
    return 1.0 / (1.0 + jnp.exp(-x))


def _log_sigmoid(x):
    return jnp.minimum(x, 0.0) - jnp.log1p(jnp.exp(-jnp.abs(x)))


def _layer_norm(z, g, b):
    mu = jnp.mean(z, axis=-1, keepdims=True)
    d = z - mu
    var = jnp.mean(d * d, axis=-1, keepdims=True)
    return d * lax.rsqrt(var + NORM_EPS) * g + b


def _log2(n):
    assert n > 0 and n & (n - 1) == 0, n
    return n.bit_length() - 1


def _div(x, n):
    return x >> _log2(n)


def _mod(x, n):
    _log2(n)
    return x & (n - 1)


def _resident(shape):
    return pl.BlockSpec(shape, lambda *_: (0,) * len(shape), pipeline_mode=pl.Buffered(1))


def _dot(a, b):
    return jnp.dot(a, b, preferred_element_type=F32)


def _dot_nt(a, b):
    return lax.dot_general(a, b, (((1,), (1,)), ((), ())), preferred_element_type=F32)


def _gla_proj_body(x_ref, w_ref, wlr_ref, wgkb_ref, bgk_ref, q_ref, k_ref, v_ref, g_ref, ld_ref):
    xb = x_ref[...].astype(BF16)
    cw = GLA_QK

    def proj(lo):
        return _dot_nt(xb, w_ref[lo:lo + cw, :])

    glr = _dot_nt(xb, wlr_ref[...])
    z = _dot(glr.astype(BF16), wgkb_ref[...]) + bgk_ref[...]
    ld_ref[...] = _log_sigmoid(z) * (1.0 / GLA_NORMALIZER)
    q_ref[...] = (proj(0) * GLA_DK ** -0.5).astype(q_ref.dtype)
    k_ref[...] = proj(GLA_QK).astype(k_ref.dtype)
    for c in range(GLA_V // cw):
        v_ref[:, c * cw:(c + 1) * cw] = proj(2 * GLA_QK + c * cw).astype(v_ref.dtype)
        gg = proj(2 * GLA_QK + GLA_V + c * cw)
        g_ref[:, c * cw:(c + 1) * cw] = (gg * _sigmoid(gg)).astype(g_ref.dtype)


def _gla_proj(x2d, w_gla, w_lr, w_gkb, b_gk, *, tm, act_dtype):
    m = x2d.shape[0]
    row = lambda i: (i, 0)
    out_shape = [
        jax.ShapeDtypeStruct((m, GLA_QK), act_dtype), jax.ShapeDtypeStruct((m, GLA_QK), act_dtype),
        jax.ShapeDtypeStruct((m, GLA_V), act_dtype), jax.ShapeDtypeStruct((m, GLA_V), act_dtype),
        jax.ShapeDtypeStruct((m, GLA_QK), F32),
    ]
    weights = [w_gla, w_lr, w_gkb, b_gk]
    est = (2 * _nbytes((tm, D_MODEL), F32) + sum(_nbytes(w.shape, w.dtype) for w in weights)
           + 2 * sum(_nbytes((tm, s.shape[1]), s.dtype) for s in out_shape) + 2 * _nbytes((tm, GLA_QK), F32))
    return pl.pallas_call(
        _gla_proj_body,
        grid=(m // tm,),
        in_specs=[pl.BlockSpec((tm, D_MODEL), row)] + [_resident(w.shape) for w in weights],
        out_specs=[pl.BlockSpec((tm, s.shape[1]), row) for s in out_shape],
        out_shape=out_shape,
        compiler_params=pltpu.CompilerParams(dimension_semantics=("arbitrary",), vmem_limit_bytes=_vmem_limit(est)),
        name="gla_proj",
    )(x2d, w_gla, w_lr, w_gkb, b_gk)


def _dil_proj_body(x_ref, w_ref, inv_ref, *refs, tm, seq_len, pos_offset, regroup):
    qkv_refs, kv_refs = refs[:N_GROUPS], refs[N_GROUPS:2 * N_GROUPS]
    trig_ref = refs[2 * N_GROUPS]
    stage_ref, hop_ref = refs[2 * N_GROUPS + 1:2 * N_GROUPS + 3] if regroup else (None, None)
    i = pl.program_id(0)
    xb = x_ref[...].astype(BF16)
    half = DIL_HD // 2
    tiles_per_seq = max(seq_len // tm, 1)
    it = _mod(i, tiles_per_seq)
    lane = lax.broadcasted_iota(jnp.int32, (tm, LANES), 1)

    @pl.when(i == 0)
    def _():
        rel = _mod(lax.broadcasted_iota(jnp.int32, (tm, LANES), 0), seq_len).astype(F32) * inv_ref[...]
        trig_ref[0] = jnp.cos(rel)
        trig_ref[1] = jnp.sin(rel)

    base = (pos_offset + it * tm).astype(F32) * inv_ref[...]
    cos_b, sin_b = jnp.cos(base), jnp.sin(base)
    cos = cos_b * trig_ref[0] - sin_b * trig_ref[1]
    sin = sin_b * trig_ref[0] + cos_b * trig_ref[1]
    first_half = _mod(lane, DIL_HD) < half
    sin_signed = jnp.where(first_half, -sin, sin)

    def rope(xc):
        partner = jnp.where(first_half, pltpu.roll(xc, LANES - half, 1), pltpu.roll(xc, half, 1))
        return xc * cos + partner * sin_signed

    n_slab = DIL_WIDTH // LANES
    for g, (w, r) in enumerate(DIL_GROUPS):
        hq = _dot_nt(xb, w_ref[g * DIL_WIDTH:(g + 1) * DIL_WIDTH, :])
        hk = _dot_nt(xb, w_ref[DIL_QKV + g * DIL_WIDTH:DIL_QKV + (g + 1) * DIL_WIDTH, :])
        hv = _dot_nt(xb, w_ref[2 * DIL_QKV + g * DIL_WIDTH:2 * DIL_QKV + (g + 1) * DIL_WIDTH, :])
        slabs = []
        for part, h in enumerate((hq, hk, hv)):
            for c in range(n_slab):
                xc = h[:, c * LANES:(c + 1) * LANES]
                slabs.append(xc if part == 2 else rope(xc) * (DIL_HD ** -0.5 if part == 0 else 1.0))
        if not regroup:
            for j, xc in enumerate(slabs):
                qkv_refs[g][:, j * LANES:(j + 1) * LANES] = xc.astype(qkv_refs[g].dtype)
                if j >= n_slab:
                    kv_refs[g][:, (j - n_slab) * LANES:(j - n_slab + 1) * LANES] = xc
            continue
        for j, xc in enumerate(slabs):
            if r == 1:
                qkv_refs[g][0, 0, :, j * LANES:(j + 1) * LANES] = xc.astype(qkv_refs[g].dtype)
            else:
                stage_ref[g - 1, j] = xc
        src_ref, r2 = stage_ref.at[g - 1], r
        if r > REGROUP_STRIDE:
            r2, n1 = r // REGROUP_STRIDE, tm // REGROUP_STRIDE
            assert r2 <= REGROUP_STRIDE
            for j in range(3 * n_slab):
                for c1 in range(REGROUP_STRIDE):
                    hop_ref[j, c1 * n1:(c1 + 1) * n1, :] = src_ref[j, pl.ds(c1, n1, stride=REGROUP_STRIDE), :]
            src_ref = hop_ref
        for c in range(r if r > 1 else 0):
            c1, c2 = (c % REGROUP_STRIDE, c // REGROUP_STRIDE) if r > REGROUP_STRIDE else (0, c)
            first = c1 * (tm // REGROUP_STRIDE) + c2
            for j in range(3 * n_slab):
                qkv_refs[g][0, c, :, j * LANES:(j + 1) * LANES] = (
                    src_ref[j, pl.ds(first, tm // r, stride=r2), :].astype(qkv_refs[g].dtype))
        keep = min(w, tm)

        @pl.when(it >= tiles_per_seq - w // keep)
        def _(g=g, keep=keep, slabs=slabs):
            for j in range(2 * n_slab):
                kv_refs[g][0, j // n_slab, (j % n_slab) * LANES:(j % n_slab + 1) * LANES, :] = (
                    slabs[n_slab + j][tm - keep:, :].T)


def _dil_proj(x2d, w_dil, inv_freq, *, tm, seq_len, pos_offset, act_dtype):
    m = x2d.shape[0]
    n_seq = m // seq_len
    regroup = seq_len >= tm
    row = lambda i: (i, 0)
    tiles_per_seq = max(seq_len // tm, 1)
    qkv_shapes, qkv_specs, kv_shapes, kv_specs, scratch = [], [], [], [], []
    for w, r in DIL_GROUPS:
        if regroup:
            assert tm % r == 0 and seq_len % tm == 0
            keep = min(w, tm)
            nblk = w // keep
            qkv_shapes.append(jax.ShapeDtypeStruct((n_seq, r, seq_len // r, 3 * DIL_WIDTH), act_dtype))
            qkv_specs.append(pl.BlockSpec((1, r, tm // r, 3 * DIL_WIDTH),
                                          lambda i: (i // tiles_per_seq, 0, i % tiles_per_seq, 0)))
            kv_shapes.append(jax.ShapeDtypeStruct((n_seq, 2, DIL_WIDTH, w), F32))
            kv_specs.append(pl.BlockSpec(
                (1, 2, DIL_WIDTH, keep),
                functools.partial(lambda i, nblk: (i // tiles_per_seq, 0, 0,
                                                   jnp.maximum(i % tiles_per_seq - (tiles_per_seq - nblk), 0)), nblk=nblk)))
        else:
            assert w >= seq_len
            qkv_shapes.append(jax.ShapeDtypeStruct((m, 3 * DIL_WIDTH), act_dtype))
            qkv_specs.append(pl.BlockSpec((tm, 3 * DIL_WIDTH), row))
            kv_shapes.append(jax.ShapeDtypeStruct((m, 2 * DIL_WIDTH), F32))
            kv_specs.append(pl.BlockSpec((tm, 2 * DIL_WIDTH), row))
    est = (2 * _nbytes((tm, D_MODEL), F32) + _nbytes(w_dil.shape, BF16)
           + 2 * N_GROUPS * _nbytes((tm, 3 * DIL_WIDTH), act_dtype)
           + 2 * sum(_nbytes(s.block_shape, F32) for s in kv_specs) + 3 * _nbytes((tm, DIL_WIDTH), F32))
    scratch.append(pltpu.VMEM((2, tm, LANES), F32))
    if regroup:
        assert DIL_GROUPS[0][1] == 1 and all(r > 1 for _, r in DIL_GROUPS[1:])
        scratch.append(pltpu.VMEM((N_GROUPS - 1, 3 * DIL_WIDTH // LANES, tm, LANES), F32))
        scratch.append(pltpu.VMEM((3 * DIL_WIDTH // LANES, tm, LANES), F32))
        est += _nbytes((N_GROUPS, tm, 3 * DIL_WIDTH), F32)
    body = functools.partial(_dil_proj_body, tm=tm, seq_len=seq_len, pos_offset=pos_offset, regroup=regroup)
    outs = pl.pallas_call(
        body,
        grid=(m // tm,),
        in_specs=[pl.BlockSpec((tm, D_MODEL), row), _resident(w_dil.shape), _resident(inv_freq.shape)],
        out_specs=qkv_specs + kv_specs,
        out_shape=qkv_shapes + kv_shapes,
        scratch_shapes=scratch,
        compiler_params=pltpu.CompilerParams(dimension_semantics=("arbitrary",), vmem_limit_bytes=_vmem_limit(est)),
        name="dil_proj",
    )(x2d, w_dil, inv_freq)
    return outs[:N_GROUPS], outs[N_GROUPS:]


def _dil_proj_short_body(x_ref, wq_ref, wk_ref, wv_ref, inv_ref, qkv_ref, kv_ref, stage_ref, *, seq_len, pos_offset):
    m = x_ref.shape[0]
    n_seq = m // seq_len
    xb = x_ref[...].astype(BF16)
    half = DIL_HD // 2
    row = lax.broadcasted_iota(jnp.int32, (m, LANES), 0)
    lane = lax.broadcasted_iota(jnp.int32, (m, LANES), 1)
    ang = (pos_offset + _mod(row, seq_len)).astype(F32) * inv_ref[...]
    cos, sin = jnp.cos(ang), jnp.sin(ang)
    first_half = _mod(lane, DIL_HD) < half
    sin_signed = jnp.where(first_half, -sin, sin)
    n_slab = DIL_WIDTH // LANES
    for part, w_ref in enumerate((wq_ref, wk_ref, wv_ref)):
        h = _dot_nt(xb, w_ref[...])
        for c in range(n_slab):
            xc = h[:, c * LANES:(c + 1) * LANES]
            if part < 2:
                partner = jnp.where(first_half, pltpu.roll(xc, LANES - half, 1), pltpu.roll(xc, half, 1))
                xc = (xc * cos + partner * sin_signed) * (DIL_HD ** -0.5 if part == 0 else 1.0)
            j = part * n_slab + c
            qkv_ref[0, :, j * LANES:(j + 1) * LANES] = xc
            if part:
                stage_ref[j - n_slab] = xc
    for t in range(seq_len):
        for j in range(2 * n_slab):
            kv_ref[0, t, j * LANES:(j + 1) * LANES, :] = stage_ref[j, pl.ds(t, n_seq, stride=seq_len), :].T


def _dil_proj_short(x2d, w_dil, inv_freq, *, seq_len, pos_offset):
    m = x2d.shape[0]
    n_seq = m // seq_len
    assert n_seq % SUBLANES == 0 and all(w >= seq_len for w, _ in DIL_GROUPS)
    w_spec = lambda part: pl.BlockSpec((DIL_WIDTH, D_MODEL), lambda g: (part * N_GROUPS + g, 0))
    est = (_nbytes((m, D_MODEL), F32) + 6 * _nbytes((DIL_WIDTH, D_MODEL), BF16) + 2 * _nbytes((m, 3 * DIL_WIDTH), F32)
           + 2 * _nbytes((seq_len, 2 * DIL_WIDTH, n_seq), F32) + _nbytes((m, 2 * DIL_WIDTH), F32) + 4 * _nbytes((m, DIL_WIDTH), F32))
    return pl.pallas_call(
        functools.partial(_dil_proj_short_body, seq_len=seq_len, pos_offset=pos_offset),
        grid=(N_GROUPS,),
        in_specs=[_resident((m, D_MODEL)), w_spec(0), w_spec(1), w_spec(2), _resident(inv_freq.shape)],
        out_specs=[pl.BlockSpec((1, m, 3 * DIL_WIDTH), lambda g: (g, 0, 0)),
                   pl.BlockSpec((1, seq_len, 2 * DIL_WIDTH, n_seq), lambda g: (g, 0, 0, 0))],
        out_shape=[jax.ShapeDtypeStruct((N_GROUPS, m, 3 * DIL_WIDTH), F32),
                   jax.ShapeDtypeStruct((N_GROUPS, seq_len, 2 * DIL_WIDTH, n_seq), F32)],
        scratch_shapes=[pltpu.VMEM((2 * DIL_WIDTH // LANES, m, LANES), F32)],
        compiler_params=pltpu.CompilerParams(dimension_semantics=("arbitrary",), vmem_limit_bytes=_vmem_limit(est)),
        name="dil_proj_short",
    )(x2d, w_dil, w_dil, w_dil, inv_freq)


def _cumsum_rows(x):
    n = x.shape[0]
    row = lax.broadcasted_iota(jnp.int32, x.shape, 0)
    s = 1
    while s < n:
        x = x + jnp.where(row >= s, pltpu.roll(x, s, 0), 0.0)
        s *= 2
    return x


def _pad_rows(x, n):
    if x.shape[0] == n:
        return x
    return jnp.concatenate([x, jnp.zeros((n - x.shape[0], x.shape[1]), x.dtype)], axis=0)


def _gla_scan_body(q_ref, k_ref, v_ref, g_ref, ld_ref, s0_ref, nw_ref, o_ref, sout_ref, *scratch, nseq, cc, carry):
    chunk = pl.program_id(1)
    sub = GLA_SUB
    cp = cc if cc % sub == 0 else sub * (cc // sub + 1)
    nsub = cp // sub
    st_ref = scratch[0] if carry else None

    if carry:
        @pl.when(chunk == 0)
        def _():
            for s in range(nseq):
                for h in range(GLA_HEADS):
                    st_ref[s * GLA_HEADS + h] = s0_ref[s, h].T

    arow = lax.broadcasted_iota(jnp.int32, (cp, cp), 0)
    acol = lax.broadcasted_iota(jnp.int32, (cp, cp), 1)
    for s in range(nseq):
        q = _pad_rows(q_ref[s].astype(F32), cp)
        k = _pad_rows(k_ref[s].astype(F32), cp)
        b = _cumsum_rows(_pad_rows(ld_ref[s], cp))
        b_last = b[cp - 1:cp, :]
        starts = [jnp.zeros((1, GLA_QK), F32)] + [b[i * sub - 1:i * sub, :] for i in range(1, nsub)]
        ends = starts[1:] + [b_last]
        rows_of = lambda vs: jnp.concatenate([jnp.broadcast_to(v, (sub, GLA_QK)) for v in vs], axis=0)
        r, e_own = rows_of(starts), rows_of(ends)
        qh = q * jnp.exp(b - r)
        qt = (qh * jnp.exp(r)).astype(BF16)
        kl32 = k * jnp.exp(b_last - b)
        kl = kl32.astype(BF16)
        k_end = k * jnp.exp(e_own - b)
        k_diag = (k * jnp.exp(r - b)).astype(BF16)
        a_rows = [[] for _ in range(GLA_HEADS)]
        for i in range(nsub):
            pieces = []
            if i:
                hop = jnp.exp(starts[i] - jnp.concatenate(ends[:i], axis=0))
                pieces = [(k_end[j * sub:(j + 1) * sub, :] * hop[j:j + 1, :]).astype(BF16) for j in range(i)]
            pieces.append(k_diag[i * sub:(i + 1) * sub, :])
            if i + 1 < nsub:
                pieces.append(jnp.zeros((cp - (i + 1) * sub, GLA_QK), BF16))
            ki = jnp.concatenate(pieces, axis=0) if len(pieces) > 1 else pieces[0]
            qi = qh[i * sub:(i + 1) * sub, :].astype(BF16)
            for h in range(GLA_HEADS):
                hs = slice(h * GLA_DK, (h + 1) * GLA_DK)
                a_rows[h].append(_dot_nt(qi[:, hs], ki[:, hs]))
        d_last = jnp.exp(b_last)
        for h in range(GLA_HEADS):
            hs = slice(h * GLA_DK, (h + 1) * GLA_DK)
            vs = slice(h * GLA_DV, (h + 1) * GLA_DV)
            a = jnp.where(acol <= arow, jnp.concatenate(a_rows[h], axis=0), 0.0).astype(BF16)
            v = _pad_rows(v_ref[s][:, vs].astype(F32), cp)
            if carry:
                st = st_ref[s * GLA_HEADS + h]
                inter = _dot_nt(qt[:, hs], st.astype(BF16))
            else:
                st = s0_ref[s, h]
                inter = _dot(qt[:, hs], st.astype(BF16))
            o = (inter + _dot(a, v.astype(BF16)))[:cc]
            o = o * lax.rsqrt(jnp.mean(o * o, axis=-1, keepdims=True) + NORM_EPS) * nw_ref[...]
            o_ref[s, :, vs] = (o * g_ref[s][:, vs].astype(F32)).astype(o_ref.dtype)
            v_tile = _pad_rows(v, GLA_CHUNK)
            if carry:
                st_ref[s * GLA_HEADS + h] = d_last[:, hs] * st + _dot(v_tile.T.astype(BF16), _pad_rows(kl[:, hs], GLA_CHUNK))
            else:
                aug = jnp.concatenate([kl32[:, hs], jnp.broadcast_to(d_last[:, hs], (SUBLANES, GLA_DK)),
                                       jnp.zeros((GLA_CHUNK - cp - SUBLANES, GLA_DK), F32)], axis=0).T
                sout_ref[s, h] = aug[:, cp:cp + 1] * st + _dot(aug.astype(BF16), v_tile.astype(BF16))

    if carry:
        @pl.when(chunk == pl.num_programs(1) - 1)
        def _():
            for s in range(nseq):
                for h in range(GLA_HEADS):
                    sout_ref[s, h] = st_ref[s * GLA_HEADS + h].T


def _gla_scan(q, k, v, g, ld, s0, norm_w, *, nseq, cc):
    n_seq, t, _ = q.shape
    assert n_seq % nseq == 0 and t % cc == 0 and cc <= GLA_CHUNK
    blk = lambda sg, c: (sg, c, 0)
    sblk = lambda sg, c: (sg, 0, 0, 0)
    carry = not (t == cc and cc + GLA_SUB + SUBLANES <= GLA_CHUNK)
    body = functools.partial(_gla_scan_body, nseq=nseq, cc=cc, carry=carry)
    est = (2 * nseq * cc * (2 * GLA_QK + 2 * GLA_V) * jnp.dtype(q.dtype).itemsize + 2 * nseq * cc * GLA_QK * 4
           + 2 * nseq * cc * GLA_V * jnp.dtype(q.dtype).itemsize + 5 * nseq * GLA_HEADS * GLA_DK * GLA_DV * 4)
    return pl.pallas_call(
        body,
        grid=(n_seq // nseq, t // cc),
        in_specs=[pl.BlockSpec((nseq, cc, GLA_QK), blk), pl.BlockSpec((nseq, cc, GLA_QK), blk),
                  pl.BlockSpec((nseq, cc, GLA_V), blk), pl.BlockSpec((nseq, cc, GLA_V), blk),
                  pl.BlockSpec((nseq, cc, GLA_QK), blk),
                  pl.BlockSpec((nseq, GLA_HEADS, GLA_DK, GLA_DV), sblk),
                  pl.BlockSpec(norm_w.shape, lambda sg, c: (0, 0))],
        out_specs=[pl.BlockSpec((nseq, cc, GLA_V), blk), pl.BlockSpec((nseq, GLA_HEADS, GLA_DK, GLA_DV), sblk)],
        out_shape=[jax.ShapeDtypeStruct((n_seq, t, GLA_V), q.dtype),
                   jax.ShapeDtypeStruct((n_seq, GLA_HEADS, GLA_DK, GLA_DV), F32)],
        scratch_shapes=[pltpu.VMEM((nseq * GLA_HEADS, GLA_DV, GLA_DK), F32)] if carry else [],
        compiler_params=pltpu.CompilerParams(dimension_semantics=("arbitrary", "arbitrary"),
                                             vmem_limit_bytes=_vmem_limit(est)),
        name="gla_scan",
    )(q, k, v, g, ld, s0, norm_w)


def _dil_attn_block(ins, outs, prev_ref, carry_ref, n, blocks_per_residue):
    row = lax.broadcasted_iota(jnp.int32, (ATT_BLK, 2 * ATT_BLK), 0)
    col = lax.broadcasted_iota(jnp.int32, (ATT_BLK, 2 * ATT_BLK), 1)
    dist = row + ATT_BLK - col
    band = (dist >= 0) & (dist <= ATT_BLK)
    lane = lax.broadcasted_iota(jnp.int32, (ATT_BLK, LANES), 1)
    for g in range(N_GROUPS):
        q_ref, kc_ref, vc_ref = ins[3 * g:3 * g + 3]
        o_ref, lse_ref = outs[2 * g:2 * g + 2]
        jb = _mod(n, blocks_per_residue[g])
        valid = band & ((col >= ATT_BLK) | (jb > 0))
        q, k_cur, v_cur = q_ref[0, 0], kc_ref[0, 0], vc_ref[0, 0]
        kk = jnp.concatenate([prev_ref[g, 0], k_cur], axis=0)
        vv = jnp.concatenate([prev_ref[g, 1], v_cur], axis=0)
        carry_ref[g, 0] = k_cur
        carry_ref[g, 1] = v_cur
        lse_tile = jnp.zeros((ATT_BLK, LANES), F32)
        for p in range(DIL_WIDTH // LANES):
            ls = slice(p * LANES, (p + 1) * LANES)
            q2, k2, v2 = q[:, ls], kk[:, ls], vv[:, ls]
            o_pair = jnp.zeros((ATT_BLK, LANES), F32)
            for e in range(LANES // DIL_HD):
                sel = (lane >= DIL_HD) if e else (lane < DIL_HD)
                s = _dot_nt(jnp.where(sel, q2, jnp.zeros_like(q2)), k2)
                s = jnp.where(valid, s, NEG_FILL)
                m = jnp.max(s, axis=-1, keepdims=True)
                pe = jnp.exp(s - m)
                den = jnp.sum(pe, axis=-1, keepdims=True)
                pv = _dot(pe.astype(BF16), v2)
                o_pair = jnp.where(sel, pv * (1.0 / den), o_pair)
                lse_tile = jnp.where(lane == p * (LANES // DIL_HD) + e, m + jnp.log(den), lse_tile)
            o_ref[0, 0, :, ls] = o_pair.astype(o_ref.dtype)
        lse_ref[0, 0] = lse_tile


def _dec_key_masks(mask_ref, t_new):
    nq = DIL_HEADS * t_new
    off = 0
    for w, r in DIL_GROUPS:
        t_q = _mod(lax.broadcasted_iota(jnp.int32, (nq, w), 0), t_new)
        j = lax.broadcasted_iota(jnp.int32, (nq, w), 1)
        mask_ref[:, off:off + w] = ((j >= t_q) & (_mod(w + t_q - j, r) == 0)).astype(F32)
        t_n = _mod(lax.broadcasted_iota(jnp.int32, (nq, LANES), 0), t_new)
        u = lax.broadcasted_iota(jnp.int32, (nq, LANES), 1)
        ok_n = (u <= t_n) & (_mod(t_n - u, r) == 0)
        mask_ref[:, off + w:off + w + LANES] = ok_n.astype(F32)
        off += w + LANES


def _dec_attn_seq(qkv_refs, c_refs, o_ref, mask_ref, t_new):
    nq = DIL_HEADS * t_new
    off = 0
    lane_w = lax.broadcasted_iota(jnp.int32, (nq, DIL_WIDTH), 1)
    row_w = lax.broadcasted_iota(jnp.int32, (nq, DIL_WIDTH), 0)
    own_head = _div(lane_w, DIL_HD) == _div(row_w, t_new)
    scores, values = [], []
    for g, (w, r) in enumerate(DIL_GROUPS):
        qkv = qkv_refs[g][0, 0].astype(F32)
        q, k_new, v_new = qkv[:, :DIL_WIDTH], qkv[:, DIL_WIDTH:2 * DIL_WIDTH], qkv[:, 2 * DIL_WIDTH:]
        q_rep = jnp.broadcast_to(q[None], (DIL_HEADS, t_new, DIL_WIDTH)).reshape(nq, DIL_WIDTH)
        q_bd = jnp.where(own_head, q_rep, 0.0).astype(BF16)
        kt_c, vt_c = c_refs[g][0].astype(BF16), c_refs[g][1].astype(BF16)
        ok_c = mask_ref[:, off:off + w] > 0.5
        ok_n = mask_ref[:, off + w:off + w + LANES] > 0.5
        off += w + LANES
        scores.append(jnp.where(ok_c, _dot(q_bd, kt_c), NEG_FILL))
        values.append(vt_c)
        scores.append(jnp.where(ok_n, _dot_nt(q_bd, _pad_rows(k_new, LANES).astype(BF16)), NEG_FILL))
        values.append(_pad_rows(v_new, LANES).astype(BF16))
    m = functools.reduce(jnp.maximum, [jnp.max(s, axis=-1, keepdims=True) for s in scores])
    den = jnp.zeros((nq, 1), F32)
    acc = jnp.zeros((nq, DIL_WIDTH), F32)
    for idx, (s, v) in enumerate(zip(scores, values)):
        pe = jnp.exp(s - m)
        den = den + jnp.sum(pe, axis=-1, keepdims=True)
        pe = pe.astype(BF16)
        acc = acc + (_dot(pe, v) if idx % 2 else _dot_nt(pe, v))
    acc = jnp.where(own_head, acc * (1.0 / den), 0.0)
    o_ref[0] = jnp.sum(acc.reshape(DIL_HEADS, t_new, DIL_WIDTH), axis=0)


def _cache_copy(c_hbm, buf, sem, g, seq, slot):
    return pltpu.make_async_copy(c_hbm[g].at[seq], buf[g].at[slot], sem.at[g, slot])


def _attention_body(*refs, blocks_per_residue, n_blocks, t_new):
    n_p = 3 * N_GROUPS
    ins_p, qkv_s, c_hbm = refs[:n_p], refs[n_p:n_p + N_GROUPS], refs[n_p + N_GROUPS:n_p + 2 * N_GROUPS]
    outs_p, o_s = refs[n_p + 2 * N_GROUPS:n_p + 4 * N_GROUPS], refs[n_p + 4 * N_GROUPS]
    bufs, sem = refs[n_p + 4 * N_GROUPS + 1:n_p + 5 * N_GROUPS + 1], refs[n_p + 5 * N_GROUPS + 1]
    prev_ref, mask_ref = refs[n_p + 5 * N_GROUPS + 2], refs[n_p + 5 * N_GROUPS + 3]
    i, n_steps = pl.program_id(0), pl.num_programs(0)

    @pl.when(i == 0)
    def _():
        for s in range(CACHE_RING - 1):
            for g in range(N_GROUPS):
                _cache_copy(c_hbm, bufs, sem, g, s, s).start()
        prev_ref[...] = jnp.zeros(prev_ref.shape, prev_ref.dtype)
        _dec_key_masks(mask_ref, t_new)

    ahead = i + (CACHE_RING - 1)

    @pl.when(ahead < n_steps)
    def _():
        for g in range(N_GROUPS):
            _cache_copy(c_hbm, bufs, sem, g, ahead, ahead % CACHE_RING).start()

    slot = i % CACHE_RING
    for g in range(N_GROUPS):
        _cache_copy(c_hbm, bufs, sem, g, i, slot).wait()
    parity = _mod(i, 2)
    _dil_attn_block(ins_p, outs_p, prev_ref.at[1 - parity], prev_ref.at[parity], _mod(i, n_blocks), blocks_per_residue)
    _dec_attn_seq(qkv_s, [bufs[g].at[slot] for g in range(N_GROUPS)], o_s, mask_ref, t_new)


def _attention(qkv_p, qkv_s, caches, *, n_seq_p, seq_len, n_seq_s, t_new):
    args, in_specs, out_specs, out_shape, bpr, scratch = [], [], [], [], [], []
    n_blocks = seq_len // ATT_BLK
    assert n_seq_s == n_seq_p * n_blocks
    assert n_seq_s >= CACHE_RING
    for g, (w, r) in enumerate(DIL_GROUPS):
        assert w // r == ATT_BLK and seq_len % (r * ATT_BLK) == 0
        nb = n_blocks // r
        bpr.append(nb)
        cur = lambda i, nb, part: (i // n_blocks, (i % n_blocks) // nb, i % nb, part)
        blk = (1, 1, ATT_BLK, DIL_WIDTH)
        for part in range(3):
            args.append(qkv_p[g])
            in_specs.append(pl.BlockSpec(blk, functools.partial(cur, nb=nb, part=part)))
        omap = functools.partial(cur, nb=nb, part=0)
        out_specs += [pl.BlockSpec(blk, omap), pl.BlockSpec((1, 1, ATT_BLK, LANES), omap)]
        out_shape += [jax.ShapeDtypeStruct((n_seq_p, r, seq_len // r, DIL_WIDTH), qkv_p[g].dtype),
                      jax.ShapeDtypeStruct((n_seq_p, r, seq_len // r, LANES), F32)]
    est = 2 * N_GROUPS * (6 * _nbytes((ATT_BLK, DIL_WIDTH), qkv_p[0].dtype) + _nbytes((ATT_BLK, LANES), F32))
    for g in range(N_GROUPS):
        args.append(qkv_s.reshape(N_GROUPS, n_seq_s, t_new, 3 * DIL_WIDTH))
        in_specs.append(pl.BlockSpec((1, 1, t_new, 3 * DIL_WIDTH), functools.partial(lambda i, g: (g, i, 0, 0), g=g)))
    for g, (w, r) in enumerate(DIL_GROUPS):
        args.append(jnp.transpose(caches[g], (0, 2, 3, 4, 1)).reshape(n_seq_s, 2, DIL_WIDTH, w))
        in_specs.append(pl.BlockSpec(memory_space=pl.ANY))
        scratch.append(pltpu.VMEM((CACHE_RING, 2, DIL_WIDTH, w), F32))
        est += CACHE_RING * _nbytes((2, DIL_WIDTH, w), F32) + 3 * _nbytes((DIL_HEADS * t_new + DIL_WIDTH, w), F32)
    scratch.append(pltpu.SemaphoreType.DMA((N_GROUPS, CACHE_RING)))
    scratch.append(pltpu.VMEM((2, N_GROUPS, 2, ATT_BLK, DIL_WIDTH), qkv_p[0].dtype))
    scratch.append(pltpu.VMEM((DIL_HEADS * t_new, sum(w + LANES for w, _ in DIL_GROUPS)), F32))
    out_specs.append(pl.BlockSpec((1, t_new, DIL_WIDTH), lambda i: (i, 0, 0)))
    out_shape.append(jax.ShapeDtypeStruct((n_seq_s, t_new, DIL_WIDTH), F32))
    outs = pl.pallas_call(
        functools.partial(_attention_body, blocks_per_residue=tuple(bpr), n_blocks=n_blocks, t_new=t_new),
        grid=(n_seq_s,),
        in_specs=in_specs, out_specs=out_specs, out_shape=out_shape, scratch_shapes=scratch,
        compiler_params=pltpu.CompilerParams(dimension_semantics=("arbitrary",), vmem_limit_bytes=_vmem_limit(est)),
        name="attention",
    )(*args)
    return ([outs[2 * g] for g in range(N_GROUPS)], [outs[2 * g + 1] for g in range(N_GROUPS)],
            outs[2 * N_GROUPS].reshape(n_seq_s * t_new, DIL_WIDTH))


def _row_order(src_ref, stage_ref, hop_ref, r):
    if r == 1:
        return src_ref[0, 0].astype(F32)
    n = src_ref.shape[2]
    rows = n * r
    n_slab = src_ref.shape[3] // LANES
    two_hops = r > REGROUP_STRIDE
    dst_ref, r2 = (hop_ref, r // REGROUP_STRIDE) if two_hops else (stage_ref, r)
    for c in range(r):
        c1, c2 = (c % REGROUP_STRIDE, c // REGROUP_STRIDE) if two_hops else (0, c)
        first = c1 * (rows // REGROUP_STRIDE) + c2
        for j in range(n_slab):
            dst_ref[j, pl.ds(first, n, stride=r2), :] = src_ref[0, c, :, j * LANES:(j + 1) * LANES].astype(F32)
    if two_hops:
        n1 = rows // REGROUP_STRIDE
        for j in range(n_slab):
            for c1 in range(REGROUP_STRIDE):
                stage_ref[j, pl.ds(c1, n1, stride=REGROUP_STRIDE), :] = hop_ref[j, c1 * n1:(c1 + 1) * n1, :]
    slabs = [stage_ref[j] for j in range(n_slab)]
    return slabs[0] if n_slab == 1 else jnp.concatenate(slabs, axis=1)


def _merge_body(*refs, n_dil_inputs):
    x_ref, og_ref = refs[:2]
    dil_refs = refs[2:2 + n_dil_inputs]
    (wg_ref, wa_ref, wb_ref, wo_ref, ex_ref, g_ref, b_ref, y_ref) = refs[2 + n_dil_inputs:10 + n_dil_inputs]
    x = x_ref[...]
    xb = x.astype(BF16)
    if n_dil_inputs == 1:
        yb_in = dil_refs[0][...].astype(F32)
    else:
        ostage_ref, lstage_ref = refs[10 + n_dil_inputs:]
        bufs = lambda ref, g: (None, None) if g == 0 else (ref.at[2 * (g - 1)], ref.at[2 * (g - 1) + 1])
        os_ = [_row_order(dil_refs[g], *bufs(ostage_ref, g), DIL_GROUPS[g][1]) for g in range(N_GROUPS)]
        lses = [_row_order(dil_refs[N_GROUPS + g], *bufs(lstage_ref, g), DIL_GROUPS[g][1]) for g in range(N_GROUPS)]
        top = functools.reduce(jnp.maximum, lses)
        es = [jnp.exp(l - top) for l in lses]
        inv = 1.0 / functools.reduce(jnp.add, es)
        yb_in = jnp.zeros(os_[0].shape, F32)
        for o, e in zip(os_, es):
            wgt = e * inv
            hi = wgt.astype(BF16)
            lo = (wgt - hi.astype(F32)).astype(BF16)
            wide = _dot(hi, ex_ref[...]) + _dot(lo, ex_ref[...])
            yb_in = yb_in + wide * o
    ya = _dot(og_ref[...].astype(BF16), wa_ref[...])
    yb = _dot(yb_in.astype(BF16), wb_ref[...])
    ga = _dot_nt(xb, wg_ref[:D_MODEL, :])
    gb = _dot_nt(xb, wg_ref[D_MODEL:, :])
    mix = _sigmoid(ga) * ya + _sigmoid(gb) * yb
    z = ALPHA * x + _dot(mix.astype(BF16), wo_ref[...])
    y_ref[...] = _layer_norm(z, g_ref[...], b_ref[...])


def _merge(x2d, o_gla, dil_inputs, w_gates, w_br_gla, w_br_dil, w_out, expand, ln_g, ln_b, *, tm, seq_len):
    m = x2d.shape[0]
    row = lambda i: (i, 0)
    tiles_per_seq = max(seq_len // tm, 1)
    acts, act_specs, scratch = [x2d, o_gla], [pl.BlockSpec((tm, D_MODEL), row), pl.BlockSpec((tm, GLA_V), row)], []
    est = 2 * _nbytes((tm, D_MODEL), F32) + 2 * _nbytes((tm, GLA_V), o_gla.dtype)
    for a in dil_inputs:
        acts.append(a)
        if a.ndim == 2:
            act_specs.append(pl.BlockSpec((tm, a.shape[1]), row))
        else:
            r = a.shape[1]
            assert seq_len % tm == 0 and tm % r == 0
            act_specs.append(pl.BlockSpec((1, r, tm // r, a.shape[3]), lambda i: (i // tiles_per_seq, 0, i % tiles_per_seq, 0)))
        est += 2 * _nbytes((tm, a.shape[-1]), a.dtype)
    if len(dil_inputs) > 1:
        scratch = [pltpu.VMEM((2 * (N_GROUPS - 1), DIL_WIDTH // LANES, tm, LANES), F32),
                   pltpu.VMEM((2 * (N_GROUPS - 1), 1, tm, LANES), F32)]
        est += 2 * (N_GROUPS - 1) * _nbytes((tm, DIL_WIDTH + LANES), F32)
    weights = [w_gates, w_br_gla, w_br_dil, w_out, expand, ln_g, ln_b]
    est += sum(_nbytes(w.shape, w.dtype) for w in weights) + 6 * _nbytes((tm, D_MODEL), F32)
    return pl.pallas_call(
        functools.partial(_merge_body, n_dil_inputs=len(dil_inputs)),
        grid=(m // tm,),
        in_specs=act_specs + [_resident(w.shape) for w in weights],
        out_specs=pl.BlockSpec((tm, D_MODEL), row),
        out_shape=jax.ShapeDtypeStruct((m, D_MODEL), F32),
        scratch_shapes=scratch,
        compiler_params=pltpu.CompilerParams(dimension_semantics=("arbitrary",), vmem_limit_bytes=_vmem_limit(est)),
        name="merge",
    )(*acts, *weights)


def _ffn_body(*refs, tm, tiles_per_seq, seq_len, tail_rows):
    long_seq = seq_len >= tm
    if long_seq:
        x_ref, halo_ref, cp_ref, pe_ref = refs[:4]
    else:
        x_ref, cp0_ref, cp1_ref, pe_ref = refs[:4]
    (wup_ref, cw_ref, cb_ref, wdn_ref, g2_ref, b2_ref, wpg_ref, wpp_ref, g3_ref, b3_ref, y_ref, tail_ref) = refs[4:]
    i = pl.program_id(0)
    x = x_ref[...]
    xb = x.astype(BF16)
    row = lax.broadcasted_iota(jnp.int32, (tm, FF_CHUNK), 0)
    acc = jnp.zeros((tm, D_MODEL), F32)
    for c in range(D_FF // FF_CHUNK):
        cs = slice(c * FF_CHUNK, (c + 1) * FF_CHUNK)
        a = _dot(xb, wup_ref[:, cs])
        u = _dot(xb, wup_ref[:, D_FF + c * FF_CHUNK:D_FF + (c + 1) * FF_CHUNK])
        if long_seq:
            a_halo = _dot(halo_ref[...].astype(BF16), wup_ref[:, cs])
            prev = jnp.where(i % tiles_per_seq == 0, cp_ref[0][:, cs], a_halo)
            p1, p2 = prev[SUBLANES - 1:SUBLANES, :], prev[SUBLANES - 2:SUBLANES - 1, :]
            t = row
        else:
            p1, p2 = cp1_ref[:, cs], cp0_ref[:, cs]
            t = _mod(row, seq_len)
        am1 = jnp.where(t == 0, p1, pltpu.roll(a, 1, 0))
        am2 = jnp.where(t == 0, p2, jnp.where(t == 1, p1, pltpu.roll(a, 2, 0)))
        cw = cw_ref[:, cs]
        conv = cb_ref[:, cs] + cw[0:1, :] * am2 + cw[1:2, :] * am1 + cw[2:3, :] * a
        gelu = 0.5 * conv * (1.0 + lax.erf(conv * (2.0 ** -0.5)))
        acc = acc + _dot((gelu * u).astype(BF16), wdn_ref[cs, :])
        tail_ref[:, cs] = a[tm - tail_rows:, :]
    x2 = _layer_norm(ALPHA * x + acc, g2_ref[...], b2_ref[...])
    gate = _sigmoid(_dot(x2.astype(BF16), wpg_ref[...]))
    emb = _dot(pe_ref[...].astype(BF16), wpp_ref[...])
    y_ref[...] = _layer_norm(ALPHA * x2 + gate * emb, g3_ref[...], b3_ref[...])


def _ffn(x2d, conv_prev, pe2d, w_up, conv_w, conv_b, w_down, ln2_g, ln2_b, w_pg, w_pp, ln3_g, ln3_b, *, tm, seq_len):
    m = x2d.shape[0]
    n_seq = m // seq_len
    row = lambda i: (i, 0)
    n_tiles = m // tm
    if seq_len >= tm:
        tiles_per_seq, tail_rows = seq_len // tm, SUBLANES
        cp = jnp.concatenate([jnp.zeros((n_seq, SUBLANES - (CONV_W - 1), D_FF), F32), conv_prev], axis=1)
        acts = [x2d, x2d, cp, pe2d]
        act_specs = [pl.BlockSpec((tm, D_MODEL), row),
                     pl.BlockSpec((SUBLANES, D_MODEL), lambda i: (jnp.maximum(i * (tm // SUBLANES) - 1, 0), 0)),
                     pl.BlockSpec((1, SUBLANES, D_FF), lambda i: (i // tiles_per_seq, 0, 0)),
                     pl.BlockSpec((tm, PLE_DIM), row)]
        act_bytes = 2 * _nbytes((tm, D_MODEL), F32) + _nbytes((SUBLANES, D_FF), F32)
    else:
        assert tm % seq_len == 0 and seq_len >= CONV_W - 1
        tiles_per_seq, tail_rows = 1, tm
        cp0 = jnp.repeat(conv_prev[:, 0, :], seq_len, axis=0)
        cp1 = jnp.repeat(conv_prev[:, 1, :], seq_len, axis=0)
        acts = [x2d, cp0, cp1, pe2d]
        act_specs = [pl.BlockSpec((tm, D_MODEL), row), pl.BlockSpec((tm, D_FF), row), pl.BlockSpec((tm, D_FF), row),
                     pl.BlockSpec((tm, PLE_DIM), row)]
        act_bytes = _nbytes((tm, D_MODEL), F32) + 2 * _nbytes((tm, D_FF), F32)
    weights = [w_up, conv_w, conv_b, w_down, ln2_g, ln2_b, w_pg, w_pp, ln3_g, ln3_b]
    est = (2 * act_bytes + sum(_nbytes(w.shape, w.dtype) for w in weights) + 4 * _nbytes((tm, D_MODEL), F32)
           + 2 * _nbytes((tail_rows, D_FF), F32) + 6 * _nbytes((tm, FF_CHUNK), F32))
    y, tail = pl.pallas_call(
        functools.partial(_ffn_body, tm=tm, tiles_per_seq=tiles_per_seq, seq_len=seq_len, tail_rows=tail_rows),
        grid=(n_tiles,),
        in_specs=act_specs + [_resident(w.shape) for w in weights],
        out_specs=[pl.BlockSpec((tm, D_MODEL), row), pl.BlockSpec((tail_rows, D_FF), row)],
        out_shape=[jax.ShapeDtypeStruct((m, D_MODEL), F32), jax.ShapeDtypeStruct((n_tiles * tail_rows, D_FF), F32)],
        compiler_params=pltpu.CompilerParams(dimension_semantics=("arbitrary",), vmem_limit_bytes=_vmem_limit(est)),
        name="ffn",
    )(*acts, *weights)
    return y, tail


def _tile_rows(m):
    return min(m, 512)


def _mixers(x, s0, pos_offset, wts, *, prompt):
    n_seq, t, _ = x.shape
    m = n_seq * t
    x2d = x.reshape(m, D_MODEL)
    act_dtype = BF16 if prompt else F32
    tm = _tile_rows(m)
    q, k, v, g, ld = _gla_proj(x2d, wts["w_gla"], wts["w_lr"], wts["w_gkb"], wts["b_gk"], tm=min(m, 2 * tm),
                               act_dtype=act_dtype)
    if prompt:
        qkv, kv_new = _dil_proj(x2d, wts["w_dil"], wts["inv_freq"], tm=tm, seq_len=t, pos_offset=pos_offset, act_dtype=act_dtype)
        kv_out = [jnp.transpose(kvn.reshape(n_seq, 2, DIL_HEADS, DIL_HD, -1), (0, 4, 1, 2, 3)) for kvn in kv_new]
    else:
        qkv, kv_new = _dil_proj_short(x2d, wts["w_dil"], wts["inv_freq"], seq_len=t, pos_offset=pos_offset)
        kv_out = [jnp.transpose(kv_new[g].reshape(t, 2, DIL_HEADS, DIL_HD, n_seq), (4, 0, 1, 2, 3)) for g in range(N_GROUPS)]
    r3 = lambda a: a.reshape(n_seq, t, a.shape[-1])
    o_gla, s_new = _gla_scan(r3(q), r3(k), r3(v), r3(g), r3(ld), s0, wts["gla_norm"],
                             nseq=n_seq if prompt else 8, cc=min(t, GLA_CHUNK))
    return dict(x2d=x2d, tm=tm, n_seq=n_seq, t=t, o_gla=o_gla.reshape(m, GLA_V), s_new=s_new, qkv=qkv, kv_out=kv_out)


def _finish(mix, dil_inputs, pe, conv_prev, wts, *, prompt):
    n_seq, t, tm = mix["n_seq"], mix["t"], mix["tm"]
    m = n_seq * t
    x1 = _merge(mix["x2d"], mix["o_gla"], dil_inputs, wts["w_gates"], wts["w_br_gla"], wts["w_br_dil"], wts["w_out"],
                wts["expand"], wts["ln1_g"], wts["ln1_b"], tm=tm, seq_len=t)
    tm_ffn = tm if prompt else min(tm, 256)
    y, tail = _ffn(x1, conv_prev, pe.reshape(m, PLE_DIM), wts["w_up"], wts["conv_w"], wts["conv_b"], wts["w_down"],
                   wts["ln2_g"], wts["ln2_b"], wts["w_ple_gate"], wts["w_ple_proj"], wts["ln3_g"], wts["ln3_b"],
                   tm=tm_ffn, seq_len=t)
    if t >= tm_ffn:
        conv_new = tail.reshape(n_seq, t // tm_ffn, SUBLANES, D_FF)[:, -1, SUBLANES - (CONV_W - 1):, :]
    else:
        conv_new = tail.reshape(n_seq, t, D_FF)[:, t - (CONV_W - 1):, :]
    return y.reshape(n_seq, t, D_MODEL), conv_new


def _forward(x_p, x_s, pe_p, pe_s, s0_s, conv_s, caches, wts):
    n_p = x_p.shape[0]
    mix_p = _mixers(x_p, jnp.zeros((n_p, GLA_HEADS, GLA_DK, GLA_DV), s0_s.dtype), 0, wts, prompt=True)
    mix_s = _mixers(x_s, s0_s, PAST_LEN, wts, prompt=False)
    o_p, lse_p, o_s = _attention(mix_p["qkv"], mix_s["qkv"], caches, n_seq_p=n_p, seq_len=x_p.shape[1],
                                 n_seq_s=x_s.shape[0], t_new=x_s.shape[1])
    y_p, conv_new_p = _finish(mix_p, o_p + lse_p, pe_p, jnp.zeros((n_p, CONV_W - 1, D_FF), x_p.dtype), wts, prompt=True)
    y_s, conv_new_s = _finish(mix_s, [o_s], pe_s, conv_s, wts, prompt=False)
    return (y_p, mix_p["s_new"], conv_new_p, mix_p["kv_out"]), (y_s, mix_s["s_new"], conv_new_s, mix_s["kv_out"])


def _prep_weights(w_in, w_gk_b, b_gk, gla_norm, w_br_gla, w_br_dil, w_out, ln1_g, ln1_b, w_up, conv_w, conv_b, w_down,
                  ln2_g, ln2_b, w_ple_gate, w_ple_proj, ln3_g, ln3_b):
    w_in_t = jnp.transpose(w_in[0])
    rows_bf16 = lambda lo, hi: w_in_t[lo:hi].astype(BF16)
    vec = lambda a: a[0].reshape(1, -1).astype(F32)
    half = DIL_HD // 2
    inv = ROPE_THETA ** (-jnp.arange(half, dtype=F32) / half)
    head_of_lane = jnp.arange(DIL_WIDTH) // DIL_HD
    expand = (jnp.arange(LANES)[:, None] == head_of_lane[None, :]).astype(BF16)
    return {
        "w_gla": rows_bf16(0, COL_GLA),
        "w_lr": jnp.pad(rows_bf16(COL_GLA, COL_LR), ((0, LANES - GLA_RANK), (0, 0))),
        "w_gkb": jnp.pad(w_gk_b[0].astype(BF16), ((0, LANES - GLA_RANK), (0, 0))),
        "b_gk": vec(b_gk),
        "w_dil": rows_bf16(COL_LR, COL_DIL),
        "w_gates": rows_bf16(COL_DIL, w_in_t.shape[0]),
        "inv_freq": jnp.tile(inv, LANES // half).reshape(1, LANES),
        "gla_norm": vec(gla_norm),
        "w_br_gla": w_br_gla[0].astype(BF16), "w_br_dil": w_br_dil[0].astype(BF16), "w_out": w_out[0].astype(BF16),
        "expand": expand,
        "ln1_g": vec(ln1_g), "ln1_b": vec(ln1_b),
        "w_up": w_up[0].astype(BF16),
        "conv_w": jnp.pad(conv_w[0].astype(F32), ((0, SUBLANES - CONV_W), (0, 0))),
        "conv_b": vec(conv_b),
        "w_down": w_down[0].astype(BF16),
        "ln2_g": vec(ln2_g), "ln2_b": vec(ln2_b),
        "w_ple_gate": w_ple_gate[0].astype(BF16), "w_ple_proj": w_ple_proj[0].astype(BF16),
        "ln3_g": vec(ln3_g), "ln3_b": vec(ln3_b),
    }


def kernel(x_prompt, x_sample, p_prompt, p_sample, state_gla, cache_conv, cache_kv_w128, cache_kv_w512, cache_kv_w2048, w_in, w_gk_b, b_gk, gla_norm, w_br_gla, w_br_dil, w_out, ln1_g, ln1_b, w_up, conv_w, conv_b, w_down, ln2_g, ln2_b, w_ple_gate, w_ple_proj, ln3_g, ln3_b):
    assert w_in.shape[0] == DEPTH == 1
    wts = _prep_weights(w_in, w_gk_b, b_gk, gla_norm, w_br_gla, w_br_dil, w_out, ln1_g, ln1_b, w_up, conv_w, conv_b,
                        w_down, ln2_g, ln2_b, w_ple_gate, w_ple_proj, ln3_g, ln3_b)
    caches = (cache_kv_w128[0], cache_kv_w512[0], cache_kv_w2048[0])
    (y_p, s_p, conv_p, kv_p), (y_s, s_s, conv_s, kv_s) = _forward(
        x_prompt, x_sample, p_prompt[0], p_sample[0], state_gla[0], cache_conv[0], caches, wts)
    return (y_p, y_s, s_p[None], s_s[None], conv_p[None], conv_s[None],
            kv_p[0][None], kv_p[1][None], kv_p[2][None], kv_s[0][None], kv_s[1][None], kv_s[2][None])
```

```python
import functools

import jax
import jax.numpy as jnp
from jax import lax
from jax.experimental import pallas as pl
from jax.experimental.pallas import tpu as pltpu

F32 = jnp.float32
BF16 = jnp.bfloat16

D_MODEL = 1024
DEPTH = 1
PAST_LEN = 8192
GLA_HEADS = 4
GLA_DK = 128
GLA_DV = 256
GLA_QK = GLA_HEADS * GLA_DK
GLA_V = GLA_HEADS * GLA_DV
GLA_RANK = 16
GLA_NORMALIZER = 16.0
DIL_GROUPS = ((128, 1), (512, 4), (2048, 16))
N_GROUPS = len(DIL_GROUPS)
DIL_HEADS = 8
DIL_HD = 64
DIL_WIDTH = DIL_HEADS * DIL_HD
DIL_QKV = N_GROUPS * DIL_WIDTH
ROPE_THETA = 10000.0
D_FF = 2816
CONV_W = 3
PLE_DIM = 256
ALPHA = (2.0 * DEPTH) ** 0.25
NORM_EPS = 1e-5
COL_GLA = 2 * GLA_QK + 2 * GLA_V
COL_LR = COL_GLA + GLA_RANK
COL_DIL = COL_LR + 3 * DIL_QKV
NEG_FILL = -1e30

LANES = 128
SUBLANES = 8
VMEM_BYTES_V7X = 64 * 1024 * 1024
VMEM_LIMIT_CAP = 60000 * 1024

ATT_BLK = 128
GLA_CHUNK = 128
GLA_SUB = 16
FF_CHUNK = D_FF // 2
CACHE_RING = 3
REGROUP_STRIDE = 4


def _vmem_limit(nbytes):
    return int(min(max(2 * nbytes, 16 * 1024 * 1024), VMEM_LIMIT_CAP))


def _nbytes(shape, dtype):
    n = 1
    for s in shape:
        n *= s
    return n * jnp.dtype(dtype).itemsize


def _sigmoid(x):
    return 1.0 / (1.0 + jnp.exp(-x))


def _log_sigmoid(x):
    return jnp.minimum(x, 0.0) - jnp.log1p(jnp.exp(-jnp.abs(x)))


def _layer_norm(z, g, b):
    mu = jnp.mean(z, axis=-1, keepdims=True)
    d = z - mu
    var = jnp.mean(d * d, axis=-1, keepdims=True)
    return d * lax.rsqrt(var + NORM_EPS) * g + b


def _log2(n):
    assert n > 0 and n & (n - 1) == 0, n
    return n.bit_length() - 1


def _div(x, n):
    return x >> _log2(n)


def _mod(x, n):
    _log2(n)
    return x & (n - 1)


def _resident(shape):
    return pl.BlockSpec(shape, lambda *_: (0,) * len(shape), pipeline_mode=pl.Buffered(1))


def _dot(a, b):
    return jnp.dot(a, b, preferred_element_type=F32)


def _dot_nt(a, b):
    return lax.dot_general(a, b, (((1,), (1,)), ((), ())), preferred_element_type=F32)


def _gla_proj_body(x_ref, w_ref, wlr_ref, wgkb_ref, bgk_ref, q_ref, k_ref, v_ref, g_ref, ld_ref):
    xb = x_ref[...].astype(BF16)
    cw = GLA_QK

    def proj(lo):
        return _dot_nt(xb, w_ref[lo:lo + cw, :])

    glr = _dot_nt(xb, wlr_ref[...])
    z = _dot(glr.astype(BF16), wgkb_ref[...]) + bgk_ref[...]
    ld_ref[...] = _log_sigmoid(z) * (1.0 / GLA_NORMALIZER)
    q_ref[...] = (proj(0) * GLA_DK ** -0.5).astype(q_ref.dtype)
    k_ref[...] = proj(GLA_QK).astype(k_ref.dtype)
    for c in range(GLA_V // cw):
        v_ref[:, c * cw:(c + 1) * cw] = proj(2 * GLA_QK + c * cw).astype(v_ref.dtype)
        gg = proj(2 * GLA_QK + GLA_V + c * cw)
        g_ref[:, c * cw:(c + 1) * cw] = (gg * _sigmoid(gg)).astype(g_ref.dtype)


def _gla_proj(x2d, w_gla, w_lr, w_gkb, b_gk, *, tm, act_dtype):
    m = x2d.shape[0]
    row = lambda i: (i, 0)
    out_shape = [
        jax.ShapeDtypeStruct((m, GLA_QK), act_dtype), jax.ShapeDtypeStruct((m, GLA_QK), act_dtype),
        jax.ShapeDtypeStruct((m, GLA_V), act_dtype), jax.ShapeDtypeStruct((m, GLA_V), act_dtype),
        jax.ShapeDtypeStruct((m, GLA_QK), F32),
    ]
    weights = [w_gla, w_lr, w_gkb, b_gk]
    est = (2 * _nbytes((tm, D_MODEL), F32) + sum(_nbytes(w.shape, w.dtype) for w in weights)
           + 2 * sum(_nbytes((tm, s.shape[1]), s.dtype) for s in out_shape) + 2 * _nbytes((tm, GLA_QK), F32))
    return pl.pallas_call(
        _gla_proj_body,
        grid=(m // tm,),
        in_specs=[pl.BlockSpec((tm, D_MODEL), row)] + [_resident(w.shape) for w in weights],
        out_specs=[pl.BlockSpec((tm, s.shape[1]), row) for s in out_shape],
        out_shape=out_shape,
        compiler_params=pltpu.CompilerParams(dimension_semantics=("arbitrary",), vmem_limit_bytes=_vmem_limit(est)),
        name="gla_proj",
    )(x2d, w_gla, w_lr, w_gkb, b_gk)


def _dil_proj_body(x_ref, w_ref, inv_ref, *refs, tm, seq_len, pos_offset):
    qkv_refs, kv_refs = refs[:N_GROUPS], refs[N_GROUPS:2 * N_GROUPS]
    trig_ref, stage_ref, hop_ref = refs[2 * N_GROUPS:2 * N_GROUPS + 3]
    i = pl.program_id(0)
    xb = x_ref[...].astype(BF16)
    half = DIL_HD // 2
    tiles_per_seq = seq_len // tm
    it = _mod(i, tiles_per_seq)
    lane = lax.broadcasted_iota(jnp.int32, (tm, LANES), 1)

    @pl.when(i == 0)
    def _():
        rel = lax.broadcasted_iota(jnp.int32, (tm, LANES), 0).astype(F32) * inv_ref[...]
        trig_ref[0] = jnp.cos(rel)
        trig_ref[1] = jnp.sin(rel)

    base = (pos_offset + it * tm).astype(F32) * inv_ref[...]
    cos_b, sin_b = jnp.cos(base), jnp.sin(base)
    cos = cos_b * trig_ref[0] - sin_b * trig_ref[1]
    sin = sin_b * trig_ref[0] + cos_b * trig_ref[1]
    first_half = _mod(lane, DIL_HD) < half
    sin_signed = jnp.where(first_half, -sin, sin)

    def rope(xc):
        partner = jnp.where(first_half, pltpu.roll(xc, LANES - half, 1), pltpu.roll(xc, half, 1))
        return xc * cos + partner * sin_signed

    n_slab = DIL_WIDTH // LANES
    for g, (w, r) in enumerate(DIL_GROUPS):
        hq = _dot_nt(xb, w_ref[g * DIL_WIDTH:(g + 1) * DIL_WIDTH, :])
        hk = _dot_nt(xb, w_ref[DIL_QKV + g * DIL_WIDTH:DIL_QKV + (g + 1) * DIL_WIDTH, :])
        hv = _dot_nt(xb, w_ref[2 * DIL_QKV + g * DIL_WIDTH:2 * DIL_QKV + (g + 1) * DIL_WIDTH, :])
        slabs = []
        for part, h in enumerate((hq, hk, hv)):
            for c in range(n_slab):
                xc = h[:, c * LANES:(c + 1) * LANES]
                slabs.append(xc if part == 2 else rope(xc) * (DIL_HD ** -0.5 if part == 0 else 1.0))
        for j, xc in enumerate(slabs):
            if r == 1:
                qkv_refs[g][0, 0, :, j * LANES:(j + 1) * LANES] = xc.astype(qkv_refs[g].dtype)
            else:
                stage_ref[g - 1, j] = xc
        src_ref, r2 = stage_ref.at[g - 1], r
        if r > REGROUP_STRIDE:
            r2, n1 = r // REGROUP_STRIDE, tm // REGROUP_STRIDE
            assert r2 <= REGROUP_STRIDE
            for j in range(3 * n_slab):
                for c1 in range(REGROUP_STRIDE):
                    hop_ref[j, c1 * n1:(c1 + 1) * n1, :] = src_ref[j, pl.ds(c1, n1, stride=REGROUP_STRIDE), :]
            src_ref = hop_ref
        for c in range(r if r > 1 else 0):
            c1, c2 = (c % REGROUP_STRIDE, c // REGROUP_STRIDE) if r > REGROUP_STRIDE else (0, c)
            first = c1 * (tm // REGROUP_STRIDE) + c2
            for j in range(3 * n_slab):
                qkv_refs[g][0, c, :, j * LANES:(j + 1) * LANES] = (
                    src_ref[j, pl.ds(first, tm // r, stride=r2), :].astype(qkv_refs[g].dtype))
        keep = min(w, tm)

        @pl.when(it >= tiles_per_seq - w // keep)
        def _(g=g, keep=keep, slabs=slabs):
            for j in range(2 * n_slab):
                kv_refs[g][0, j // n_slab, (j % n_slab) * LANES:(j % n_slab + 1) * LANES, :] = (
                    slabs[n_slab + j][tm - keep:, :].T)


def _dil_proj(x2d, w_dil, inv_freq, *, tm, seq_len, pos_offset, act_dtype):
    m = x2d.shape[0]
    n_seq = m // seq_len
    assert seq_len % tm == 0 and DIL_GROUPS[0][1] == 1 and all(r > 1 for _, r in DIL_GROUPS[1:])
    row = lambda i: (i, 0)
    tiles_per_seq = seq_len // tm
    qkv_shapes, qkv_specs, kv_shapes, kv_specs = [], [], [], []
    for w, r in DIL_GROUPS:
        assert tm % r == 0
        keep = min(w, tm)
        nblk = w // keep
        qkv_shapes.append(jax.ShapeDtypeStruct((n_seq, r, seq_len // r, 3 * DIL_WIDTH), act_dtype))
        qkv_specs.append(pl.BlockSpec((1, r, tm // r, 3 * DIL_WIDTH),
                                      lambda i: (i // tiles_per_seq, 0, i % tiles_per_seq, 0)))
        kv_shapes.append(jax.ShapeDtypeStruct((n_seq, 2, DIL_WIDTH, w), F32))
        kv_specs.append(pl.BlockSpec(
            (1, 2, DIL_WIDTH, keep),
            functools.partial(lambda i, nblk: (i // tiles_per_seq, 0, 0,
                                               jnp.maximum(i % tiles_per_seq - (tiles_per_seq - nblk), 0)), nblk=nblk)))
    n_slab = 3 * DIL_WIDTH // LANES
    scratch = [pltpu.VMEM((2, tm, LANES), F32),
               pltpu.VMEM((N_GROUPS - 1, n_slab, tm, LANES), F32),
               pltpu.VMEM((n_slab, tm, LANES), F32)]
    est = (2 * _nbytes((tm, D_MODEL), F32) + _nbytes(w_dil.shape, BF16)
           + 2 * N_GROUPS * _nbytes((tm, 3 * DIL_WIDTH), act_dtype)
           + 2 * sum(_nbytes(s.block_shape, F32) for s in kv_specs) + 3 * _nbytes((tm, DIL_WIDTH), F32)
           + _nbytes((N_GROUPS, tm, 3 * DIL_WIDTH), F32))
    body = functools.partial(_dil_proj_body, tm=tm, seq_len=seq_len, pos_offset=pos_offset)
    outs = pl.pallas_call(
        body,
        grid=(m // tm,),
        in_specs=[pl.BlockSpec((tm, D_MODEL), row), _resident(w_dil.shape), _resident(inv_freq.shape)],
        out_specs=qkv_specs + kv_specs,
        out_shape=qkv_shapes + kv_shapes,
        scratch_shapes=scratch,
        compiler_params=pltpu.CompilerParams(dimension_semantics=("arbitrary",), vmem_limit_bytes=_vmem_limit(est)),
        name="dil_proj",
    )(x2d, w_dil, inv_freq)
    return outs[:N_GROUPS], outs[N_GROUPS:]


def _dil_proj_short_body(x_ref, wq_ref, wk_ref, wv_ref, inv_ref, qkv_ref, kv_ref, stage_ref, *, seq_len, pos_offset):
    m = x_ref.shape[0]
    n_seq = m // seq_len
    xb = x_ref[...].astype(BF16)
    half = DIL_HD // 2
    row = lax.broadcasted_iota(jnp.int32, (m, LANES), 0)
    lane = lax.broadcasted_iota(jnp.int32, (m, LANES), 1)
    ang = (pos_offset + _mod(row, seq_len)).astype(F32) * inv_ref[...]
    cos, sin = jnp.cos(ang), jnp.sin(ang)
    first_half = _mod(lane, DIL_HD) < half
    sin_signed = jnp.where(first_half, -sin, sin)
    n_slab = DIL_WIDTH // LANES
    for part, w_ref in enumerate((wq_ref, wk_ref, wv_ref)):
        h = _dot_nt(xb, w_ref[...])
        for c in range(n_slab):
            xc = h[:, c * LANES:(c + 1) * LANES]
            if part < 2:
                partner = jnp.where(first_half, pltpu.roll(xc, LANES - half, 1), pltpu.roll(xc, half, 1))
                xc = (xc * cos + partner * sin_signed) * (DIL_HD ** -0.5 if part == 0 else 1.0)
            j = part * n_slab + c
            qkv_ref[0, :, j * LANES:(j + 1) * LANES] = xc
            if part:
                stage_ref[j - n_slab] = xc
    for t in range(seq_len):
        for j in range(2 * n_slab):
            kv_ref[0, t, j * LANES:(j + 1) * LANES, :] = stage_ref[j, pl.ds(t, n_seq, stride=seq_len), :].T


def _dil_proj_short(x2d, w_dil, inv_freq, *, seq_len, pos_offset):
    m = x2d.shape[0]
    n_seq = m // seq_len
    assert n_seq % SUBLANES == 0 and all(w >= seq_len for w, _ in DIL_GROUPS)
    w_spec = lambda part: pl.BlockSpec((DIL_WIDTH, D_MODEL), lambda g: (part * N_GROUPS + g, 0))
    est = (_nbytes((m, D_MODEL), F32) + 6 * _nbytes((DIL_WIDTH, D_MODEL), BF16) + 2 * _nbytes((m, 3 * DIL_WIDTH), F32)
           + 2 * _nbytes((seq_len, 2 * DIL_WIDTH, n_seq), F32) + _nbytes((m, 2 * DIL_WIDTH), F32) + 4 * _nbytes((m, DIL_WIDTH), F32))
    return pl.pallas_call(
        functools.partial(_dil_proj_short_body, seq_len=seq_len, pos_offset=pos_offset),
        grid=(N_GROUPS,),
        in_specs=[_resident((m, D_MODEL)), w_spec(0), w_spec(1), w_spec(2), _resident(inv_freq.shape)],
        out_specs=[pl.BlockSpec((1, m, 3 * DIL_WIDTH), lambda g: (g, 0, 0)),
                   pl.BlockSpec((1, seq_len, 2 * DIL_WIDTH, n_seq), lambda g: (g, 0, 0, 0))],
        out_shape=[jax.ShapeDtypeStruct((N_GROUPS, m, 3 * DIL_WIDTH), F32),
                   jax.ShapeDtypeStruct((N_GROUPS, seq_len, 2 * DIL_WIDTH, n_seq), F32)],
        scratch_shapes=[pltpu.VMEM((2 * DIL_WIDTH // LANES, m, LANES), F32)],
        compiler_params=pltpu.CompilerParams(dimension_semantics=("arbitrary",), vmem_limit_bytes=_vmem_limit(est)),
        name="dil_proj_short",
    )(x2d, w_dil, w_dil, w_dil, inv_freq)


def _cumsum_rows(x):
    n = x.shape[0]
    row = lax.broadcasted_iota(jnp.int32, x.shape, 0)
    s = 1
    while s < n:
        x = x + jnp.where(row >= s, pltpu.roll(x, s, 0), 0.0)
        s *= 2
    return x


def _pad_rows(x, n):
    if x.shape[0] == n:
        return x
    return jnp.concatenate([x, jnp.zeros((n - x.shape[0], x.shape[1]), x.dtype)], axis=0)


def _gla_scan_body(q_ref, k_ref, v_ref, g_ref, ld_ref, s0_ref, nw_ref, o_ref, sout_ref, *scratch, nseq, cc, carry):
    chunk = pl.program_id(1)
    sub = GLA_SUB
    cp = cc if cc % sub == 0 else sub * (cc // sub + 1)
    nsub = cp // sub
    st_ref = scratch[0] if carry else None

    if carry:
        @pl.when(chunk == 0)
        def _():
            for s in range(nseq):
                for h in range(GLA_HEADS):
                    st_ref[s * GLA_HEADS + h] = s0_ref[s, h].T

    arow = lax.broadcasted_iota(jnp.int32, (cp, cp), 0)
    acol = lax.broadcasted_iota(jnp.int32, (cp, cp), 1)
    for s in range(nseq):
        q = _pad_rows(q_ref[s].astype(F32), cp)
        k = _pad_rows(k_ref[s].astype(F32), cp)
        b = _cumsum_rows(_pad_rows(ld_ref[s], cp))
        b_last = b[cp - 1:cp, :]
        starts = [jnp.zeros((1, GLA_QK), F32)] + [b[i * sub - 1:i * sub, :] for i in range(1, nsub)]
        ends = starts[1:] + [b_last]
        rows_of = lambda vs: jnp.concatenate([jnp.broadcast_to(v, (sub, GLA_QK)) for v in vs], axis=0)
        r, e_own = rows_of(starts), rows_of(ends)
        qh = q * jnp.exp(b - r)
        qt = (qh * jnp.exp(r)).astype(BF16)
        kl32 = k * jnp.exp(b_last - b)
        kl = kl32.astype(BF16)
        k_end = k * jnp.exp(e_own - b)
        k_diag = (k * jnp.exp(r - b)).astype(BF16)
        a_rows = [[] for _ in range(GLA_HEADS)]
        for i in range(nsub):
            pieces = []
            if i:
                hop = jnp.exp(starts[i] - jnp.concatenate(ends[:i], axis=0))
                pieces = [(k_end[j * sub:(j + 1) * sub, :] * hop[j:j + 1, :]).astype(BF16) for j in range(i)]
            pieces.append(k_diag[i * sub:(i + 1) * sub, :])
            if i + 1 < nsub:
                pieces.append(jnp.zeros((cp - (i + 1) * sub, GLA_QK), BF16))
            ki = jnp.concatenate(pieces, axis=0) if len(pieces) > 1 else pieces[0]
            qi = qh[i * sub:(i + 1) * sub, :].astype(BF16)
            for h in range(GLA_HEADS):
                hs = slice(h * GLA_DK, (h + 1) * GLA_DK)
                a_rows[h].append(_dot_nt(qi[:, hs], ki[:, hs]))
        d_last = jnp.exp(b_last)
        for h in range(GLA_HEADS):
            hs = slice(h * GLA_DK, (h + 1) * GLA_DK)
            vs = slice(h * GLA_DV, (h + 1) * GLA_DV)
            a = jnp.where(acol <= arow, jnp.concatenate(a_rows[h], axis=0), 0.0).astype(BF16)
            v = _pad_rows(v_ref[s][:, vs].astype(F32), cp)
            if carry:
                st = st_ref[s * GLA_HEADS + h]
                inter = _dot_nt(qt[:, hs], st.astype(BF16))
            else:
                st = s0_ref[s, h]
                inter = _dot(qt[:, hs], st.astype(BF16))
            o = (inter + _dot(a, v.astype(BF16)))[:cc]
            o = o * lax.rsqrt(jnp.mean(o * o, axis=-1, keepdims=True) + NORM_EPS) * nw_ref[...]
            o_ref[s, :, vs] = (o * g_ref[s][:, vs].astype(F32)).astype(o_ref.dtype)
            v_tile = _pad_rows(v, GLA_CHUNK)
            if carry:
                st_ref[s * GLA_HEADS + h] = d_last[:, hs] * st + _dot(v_tile.T.astype(BF16), _pad_rows(kl[:, hs], GLA_CHUNK))
            else:
                aug = jnp.concatenate([kl32[:, hs], jnp.broadcast_to(d_last[:, hs], (SUBLANES, GLA_DK)),
                                       jnp.zeros((GLA_CHUNK - cp - SUBLANES, GLA_DK), F32)], axis=0).T
                sout_ref[s, h] = aug[:, cp:cp + 1] * st + _dot(aug.astype(BF16), v_tile.astype(BF16))

    if carry:
        @pl.when(chunk == pl.num_programs(1) - 1)
        def _():
            for s in range(nseq):
                for h in range(GLA_HEADS):
                    sout_ref[s, h] = st_ref[s * GLA_HEADS + h].T


def _gla_scan(q, k, v, g, ld, s0, norm_w, *, nseq, cc):
    n_seq, t, _ = q.shape
    assert n_seq % nseq == 0 and t % cc == 0 and cc <= GLA_CHUNK
    blk = lambda sg, c: (sg, c, 0)
    sblk = lambda sg, c: (sg, 0, 0, 0)
    carry = not (t == cc and cc + GLA_SUB + SUBLANES <= GLA_CHUNK)
    body = functools.partial(_gla_scan_body, nseq=nseq, cc=cc, carry=carry)
    est = (2 * nseq * cc * (2 * GLA_QK + 2 * GLA_V) * jnp.dtype(q.dtype).itemsize + 2 * nseq * cc * GLA_QK * 4
           + 2 * nseq * cc * GLA_V * jnp.dtype(q.dtype).itemsize + 5 * nseq * GLA_HEADS * GLA_DK * GLA_DV * 4)
    return pl.pallas_call(
        body,
        grid=(n_seq // nseq, t // cc),
        in_specs=[pl.BlockSpec((nseq, cc, GLA_QK), blk), pl.BlockSpec((nseq, cc, GLA_QK), blk),
                  pl.BlockSpec((nseq, cc, GLA_V), blk), pl.BlockSpec((nseq, cc, GLA_V), blk),
                  pl.BlockSpec((nseq, cc, GLA_QK), blk),
                  pl.BlockSpec((nseq, GLA_HEADS, GLA_DK, GLA_DV), sblk),
                  pl.BlockSpec(norm_w.shape, lambda sg, c: (0, 0))],
        out_specs=[pl.BlockSpec((nseq, cc, GLA_V), blk), pl.BlockSpec((nseq, GLA_HEADS, GLA_DK, GLA_DV), sblk)],
        out_shape=[jax.ShapeDtypeStruct((n_seq, t, GLA_V), q.dtype),
                   jax.ShapeDtypeStruct((n_seq, GLA_HEADS, GLA_DK, GLA_DV), F32)],
        scratch_shapes=[pltpu.VMEM((nseq * GLA_HEADS, GLA_DV, GLA_DK), F32)] if carry else [],
        compiler_params=pltpu.CompilerParams(dimension_semantics=("arbitrary", "arbitrary"),
                                             vmem_limit_bytes=_vmem_limit(est)),
        name="gla_scan",
    )(q, k, v, g, ld, s0, norm_w)


def _dil_attn_block(ins, outs, prev_ref, carry_ref, n, blocks_per_residue):
    row = lax.broadcasted_iota(jnp.int32, (ATT_BLK, 2 * ATT_BLK), 0)
    col = lax.broadcasted_iota(jnp.int32, (ATT_BLK, 2 * ATT_BLK), 1)
    dist = row + ATT_BLK - col
    band = (dist >= 0) & (dist <= ATT_BLK)
    lane = lax.broadcasted_iota(jnp.int32, (ATT_BLK, LANES), 1)
    for g in range(N_GROUPS):
        q_ref, kc_ref, vc_ref = ins[3 * g:3 * g + 3]
        o_ref, lse_ref = outs[2 * g:2 * g + 2]
        jb = _mod(n, blocks_per_residue[g])
        valid = band & ((col >= ATT_BLK) | (jb > 0))
        q, k_cur, v_cur = q_ref[0, 0], kc_ref[0, 0], vc_ref[0, 0]
        kk = jnp.concatenate([prev_ref[g, 0], k_cur], axis=0)
        vv = jnp.concatenate([prev_ref[g, 1], v_cur], axis=0)
        carry_ref[g, 0] = k_cur
        carry_ref[g, 1] = v_cur
        lse_tile = jnp.zeros((ATT_BLK, LANES), F32)
        for p in range(DIL_WIDTH // LANES):
            ls = slice(p * LANES, (p + 1) * LANES)
            q2, k2, v2 = q[:, ls], kk[:, ls], vv[:, ls]
            o_pair = jnp.zeros((ATT_BLK, LANES), F32)
            for e in range(LANES // DIL_HD):
                sel = (lane >= DIL_HD) if e else (lane < DIL_HD)
                s = _dot_nt(jnp.where(sel, q2, jnp.zeros_like(q2)), k2)
                s = jnp.where(valid, s, NEG_FILL)
                m = jnp.max(s, axis=-1, keepdims=True)
                pe = jnp.exp(s - m)
                den = jnp.sum(pe, axis=-1, keepdims=True)
                pv = _dot(pe.astype(BF16), v2)
                o_pair = jnp.where(sel, pv * (1.0 / den), o_pair)
                lse_tile = jnp.where(lane == p * (LANES // DIL_HD) + e, m + jnp.log(den), lse_tile)
            o_ref[0, 0, :, ls] = o_pair.astype(o_ref.dtype)
        lse_ref[0, 0] = lse_tile


def _dec_key_masks(mask_ref, t_new):
    nq = DIL_HEADS * t_new
    off = 0
    for w, r in DIL_GROUPS:
        t_q = _mod(lax.broadcasted_iota(jnp.int32, (nq, w), 0), t_new)
        j = lax.broadcasted_iota(jnp.int32, (nq, w), 1)
        mask_ref[:, off:off + w] = ((j >= t_q) & (_mod(w + t_q - j, r) == 0)).astype(F32)
        t_n = _mod(lax.broadcasted_iota(jnp.int32, (nq, LANES), 0), t_new)
        u = lax.broadcasted_iota(jnp.int32, (nq, LANES), 1)
        ok_n = (u <= t_n) & (_mod(t_n - u, r) == 0)
        mask_ref[:, off + w:off + w + LANES] = ok_n.astype(F32)
        off += w + LANES


def _dec_attn_seq(qkv_refs, c_refs, o_ref, mask_ref, t_new):
    nq = DIL_HEADS * t_new
    off = 0
    lane_w = lax.broadcasted_iota(jnp.int32, (nq, DIL_WIDTH), 1)
    row_w = lax.broadcasted_iota(jnp.int32, (nq, DIL_WIDTH), 0)
    own_head = _div(lane_w, DIL_HD) == _div(row_w, t_new)
    scores, values = [], []
    for g, (w, r) in enumerate(DIL_GROUPS):
        qkv = qkv_refs[g][0, 0].astype(F32)
        q, k_new, v_new = qkv[:, :DIL_WIDTH], qkv[:, DIL_WIDTH:2 * DIL_WIDTH], qkv[:, 2 * DIL_WIDTH:]
        q_rep = jnp.broadcast_to(q[None], (DIL_HEADS, t_new, DIL_WIDTH)).reshape(nq, DIL_WIDTH)
        q_bd = jnp.where(own_head, q_rep, 0.0).astype(BF16)
        kt_c, vt_c = c_refs[g][0].astype(BF16), c_refs[g][1].astype(BF16)
        ok_c = mask_ref[:, off:off + w] > 0.5
        ok_n = mask_ref[:, off + w:off + w + LANES] > 0.5
        off += w + LANES
        scores.append(jnp.where(ok_c, _dot(q_bd, kt_c), NEG_FILL))
        values.append(vt_c)
        scores.append(jnp.where(ok_n, _dot_nt(q_bd, _pad_rows(k_new, LANES).astype(BF16)), NEG_FILL))
        values.append(_pad_rows(v_new, LANES).astype(BF16))
    m = functools.reduce(jnp.maximum, [jnp.max(s, axis=-1, keepdims=True) for s in scores])
    den = jnp.zeros((nq, 1), F32)
    acc = jnp.zeros((nq, DIL_WIDTH), F32)
    for idx, (s, v) in enumerate(zip(scores, values)):
        pe = jnp.exp(s - m)
        den = den + jnp.sum(pe, axis=-1, keepdims=True)
        pe = pe.astype(BF16)
        acc = acc + (_dot(pe, v) if idx % 2 else _dot_nt(pe, v))
    acc = jnp.where(own_head, acc * (1.0 / den), 0.0)
    o_ref[0] = jnp.sum(acc.reshape(DIL_HEADS, t_new, DIL_WIDTH), axis=0)


def _cache_copy(c_hbm, buf, sem, g, seq, slot):
    return pltpu.make_async_copy(c_hbm[g].at[seq], buf[g].at[slot], sem.at[g, slot])


def _attention_body(*refs, blocks_per_residue, n_blocks, t_new):
    n_p = 3 * N_GROUPS
    ins_p, qkv_s, c_hbm = refs[:n_p], refs[n_p:n_p + N_GROUPS], refs[n_p + N_GROUPS:n_p + 2 * N_GROUPS]
    outs_p, o_s = refs[n_p + 2 * N_GROUPS:n_p + 4 * N_GROUPS], refs[n_p + 4 * N_GROUPS]
    bufs, sem = refs[n_p + 4 * N_GROUPS + 1:n_p + 5 * N_GROUPS + 1], refs[n_p + 5 * N_GROUPS + 1]
    prev_ref, mask_ref = refs[n_p + 5 * N_GROUPS + 2], refs[n_p + 5 * N_GROUPS + 3]
    i, n_steps = pl.program_id(0), pl.num_programs(0)

    @pl.when(i == 0)
    def _():
        for s in range(CACHE_RING - 1):
            for g in range(N_GROUPS):
                _cache_copy(c_hbm, bufs, sem, g, s, s).start()
        prev_ref[...] = jnp.zeros(prev_ref.shape, prev_ref.dtype)
        _dec_key_masks(mask_ref, t_new)

    ahead = i + (CACHE_RING - 1)

    @pl.when(ahead < n_steps)
    def _():
        for g in range(N_GROUPS):
            _cache_copy(c_hbm, bufs, sem, g, ahead, ahead % CACHE_RING).start()

    slot = i % CACHE_RING
    for g in range(N_GROUPS):
        _cache_copy(c_hbm, bufs, sem, g, i, slot).wait()
    parity = _mod(i, 2)
    _dil_attn_block(ins_p, outs_p, prev_ref.at[1 - parity], prev_ref.at[parity], _mod(i, n_blocks), blocks_per_residue)
    _dec_attn_seq(qkv_s, [bufs[g].at[slot] for g in range(N_GROUPS)], o_s, mask_ref, t_new)


def _attention(qkv_p, qkv_s, caches, *, n_seq_p, seq_len, n_seq_s, t_new):
    args, in_specs, out_specs, out_shape, bpr, scratch = [], [], [], [], [], []
    n_blocks = seq_len // ATT_BLK
    assert n_seq_s == n_seq_p * n_blocks
    assert n_seq_s >= CACHE_RING
    for g, (w, r) in enumerate(DIL_GROUPS):
        assert w // r == ATT_BLK and seq_len % (r * ATT_BLK) == 0
        nb = n_blocks // r
        bpr.append(nb)
        cur = lambda i, nb, part: (i // n_blocks, (i % n_blocks) // nb, i % nb, part)
        blk = (1, 1, ATT_BLK, DIL_WIDTH)
        for part in range(3):
            args.append(qkv_p[g])
            in_specs.append(pl.BlockSpec(blk, functools.partial(cur, nb=nb, part=part)))
        omap = functools.partial(cur, nb=nb, part=0)
        out_specs += [pl.BlockSpec(blk, omap), pl.BlockSpec((1, 1, ATT_BLK, LANES), omap)]
        out_shape += [jax.ShapeDtypeStruct((n_seq_p, r, seq_len // r, DIL_WIDTH), qkv_p[g].dtype),
                      jax.ShapeDtypeStruct((n_seq_p, r, seq_len // r, LANES), F32)]
    est = 2 * N_GROUPS * (6 * _nbytes((ATT_BLK, DIL_WIDTH), qkv_p[0].dtype) + _nbytes((ATT_BLK, LANES), F32))
    for g in range(N_GROUPS):
        args.append(qkv_s.reshape(N_GROUPS, n_seq_s, t_new, 3 * DIL_WIDTH))
        in_specs.append(pl.BlockSpec((1, 1, t_new, 3 * DIL_WIDTH), functools.partial(lambda i, g: (g, i, 0, 0), g=g)))
    for g, (w, r) in enumerate(DIL_GROUPS):
        args.append(jnp.transpose(caches[g], (0, 2, 3, 4, 1)).reshape(n_seq_s, 2, DIL_WIDTH, w))
        in_specs.append(pl.BlockSpec(memory_space=pl.ANY))
        scratch.append(pltpu.VMEM((CACHE_RING, 2, DIL_WIDTH, w), F32))
        est += CACHE_RING * _nbytes((2, DIL_WIDTH, w), F32) + 3 * _nbytes((DIL_HEADS * t_new + DIL_WIDTH, w), F32)
    scratch.append(pltpu.SemaphoreType.DMA((N_GROUPS, CACHE_RING)))
    scratch.append(pltpu.VMEM((2, N_GROUPS, 2, ATT_BLK, DIL_WIDTH), qkv_p[0].dtype))
    scratch.append(pltpu.VMEM((DIL_HEADS * t_new, sum(w + LANES for w, _ in DIL_GROUPS)), F32))
    out_specs.append(pl.BlockSpec((1, t_new, DIL_WIDTH), lambda i: (i, 0, 0)))
    out_shape.append(jax.ShapeDtypeStruct((n_seq_s, t_new, DIL_WIDTH), F32))
    outs = pl.pallas_call(
        functools.partial(_attention_body, blocks_per_residue=tuple(bpr), n_blocks=n_blocks, t_new=t_new),
        grid=(n_seq_s,),
        in_specs=in_specs, out_specs=out_specs, out_shape=out_shape, scratch_shapes=scratch,
        compiler_params=pltpu.CompilerParams(dimension_semantics=("arbitrary",), vmem_limit_bytes=_vmem_limit(est)),
        name="attention",
    )(*args)
    return ([outs[2 * g] for g in range(N_GROUPS)], [outs[2 * g + 1] for g in range(N_GROUPS)],
            outs[2 * N_GROUPS].reshape(n_seq_s * t_new, DIL_WIDTH))


def _row_order(src_ref, stage_ref, hop_ref, r):
    if r == 1:
        return src_ref[0, 0].astype(F32)
    n = src_ref.shape[2]
    rows = n * r
    n_slab = src_ref.shape[3] // LANES
    two_hops = r > REGROUP_STRIDE
    dst_ref, r2 = (hop_ref, r // REGROUP_STRIDE) if two_hops else (stage_ref, r)
    for c in range(r):
        c1, c2 = (c % REGROUP_STRIDE, c // REGROUP_STRIDE) if two_hops else (0, c)
        first = c1 * (rows // REGROUP_STRIDE) + c2
        for j in range(n_slab):
            dst_ref[j, pl.ds(first, n, stride=r2), :] = src_ref[0, c, :, j * LANES:(j + 1) * LANES].astype(F32)
    if two_hops:
        n1 = rows // REGROUP_STRIDE
        for j in range(n_slab):
            for c1 in range(REGROUP_STRIDE):
                stage_ref[j, pl.ds(c1, n1, stride=REGROUP_STRIDE), :] = hop_ref[j, c1 * n1:(c1 + 1) * n1, :]
    slabs = [stage_ref[j] for j in range(n_slab)]
    return slabs[0] if n_slab == 1 else jnp.concatenate(slabs, axis=1)


def _merge_body(*refs, n_dil_inputs):
    x_ref, og_ref = refs[:2]
    dil_refs = refs[2:2 + n_dil_inputs]
    (wg_ref, wa_ref, wb_ref, wo_ref, ex_ref, g_ref, b_ref, y_ref) = refs[2 + n_dil_inputs:10 + n_dil_inputs]
    x = x_ref[...]
    xb = x.astype(BF16)
    if n_dil_inputs == 1:
        yb_in = dil_refs[0][...].astype(F32)
    else:
        ostage_ref, lstage_ref = refs[10 + n_dil_inputs:]
        bufs = lambda ref, g: (None, None) if g == 0 else (ref.at[2 * (g - 1)], ref.at[2 * (g - 1) + 1])
        os_ = [_row_order(dil_refs[g], *bufs(ostage_ref, g), DIL_GROUPS[g][1]) for g in range(N_GROUPS)]
        lses = [_row_order(dil_refs[N_GROUPS + g], *bufs(lstage_ref, g), DIL_GROUPS[g][1]) for g in range(N_GROUPS)]
        top = functools.reduce(jnp.maximum, lses)
        es = [jnp.exp(l - top) for l in lses]
        inv = 1.0 / functools.reduce(jnp.add, es)
        yb_in = jnp.zeros(os_[0].shape, F32)
        for o, e in zip(os_, es):
            wgt = e * inv
            hi = wgt.astype(BF16)
            lo = (wgt - hi.astype(F32)).astype(BF16)
            wide = _dot(hi, ex_ref[...]) + _dot(lo, ex_ref[...])
            yb_in = yb_in + wide * o
    ya = _dot(og_ref[...].astype(BF16), wa_ref[...])
    yb = _dot(yb_in.astype(BF16), wb_ref[...])
    ga = _dot_nt(xb, wg_ref[:D_MODEL, :])
    gb = _dot_nt(xb, wg_ref[D_MODEL:, :])
    mix = _sigmoid(ga) * ya + _sigmoid(gb) * yb
    z = ALPHA * x + _dot(mix.astype(BF16), wo_ref[...])
    y_ref[...] = _layer_norm(z, g_ref[...], b_ref[...])


def _merge(x2d, o_gla, dil_inputs, w_gates, w_br_gla, w_br_dil, w_out, expand, ln_g, ln_b, *, tm, seq_len):
    m = x2d.shape[0]
    row = lambda i: (i, 0)
    tiles_per_seq = max(seq_len // tm, 1)
    acts, act_specs, scratch = [x2d, o_gla], [pl.BlockSpec((tm, D_MODEL), row), pl.BlockSpec((tm, GLA_V), row)], []
    est = 2 * _nbytes((tm, D_MODEL), F32) + 2 * _nbytes((tm, GLA_V), o_gla.dtype)
    for a in dil_inputs:
        acts.append(a)
        if a.ndim == 2:
            act_specs.append(pl.BlockSpec((tm, a.shape[1]), row))
        else:
            r = a.shape[1]
            assert seq_len % tm == 0 and tm % r == 0
            act_specs.append(pl.BlockSpec((1, r, tm // r, a.shape[3]), lambda i: (i // tiles_per_seq, 0, i % tiles_per_seq, 0)))
        est += 2 * _nbytes((tm, a.shape[-1]), a.dtype)
    if len(dil_inputs) > 1:
        scratch = [pltpu.VMEM((2 * (N_GROUPS - 1), DIL_WIDTH // LANES, tm, LANES), F32),
                   pltpu.VMEM((2 * (N_GROUPS - 1), 1, tm, LANES), F32)]
        est += 2 * (N_GROUPS - 1) * _nbytes((tm, DIL_WIDTH + LANES), F32)
    weights = [w_gates, w_br_gla, w_br_dil, w_out, expand, ln_g, ln_b]
    est += sum(_nbytes(w.shape, w.dtype) for w in weights) + 6 * _nbytes((tm, D_MODEL), F32)
    return pl.pallas_call(
        functools.partial(_merge_body, n_dil_inputs=len(dil_inputs)),
        grid=(m // tm,),
        in_specs=act_specs + [_resident(w.shape) for w in weights],
        out_specs=pl.BlockSpec((tm, D_MODEL), row),
        out_shape=jax.ShapeDtypeStruct((m, D_MODEL), F32),
        scratch_shapes=scratch,
        compiler_params=pltpu.CompilerParams(dimension_semantics=("arbitrary",), vmem_limit_bytes=_vmem_limit(est)),
        name="merge",
    )(*acts, *weights)


def _ffn_body(*refs, tm, tiles_per_seq, seq_len, tail_rows):
    long_seq = seq_len >= tm
    if long_seq:
        x_ref, halo_ref, cp_ref, pe_ref = refs[:4]
    else:
        x_ref, cp0_ref, cp1_ref, pe_ref = refs[:4]
    (wup_ref, cw_ref, cb_ref, wdn_ref, g2_ref, b2_ref, wpg_ref, wpp_ref, g3_ref, b3_ref, y_ref, tail_ref) = refs[4:]
    i = pl.program_id(0)
    x = x_ref[...]
    xb = x.astype(BF16)
    row = lax.broadcasted_iota(jnp.int32, (tm, FF_CHUNK), 0)
    acc = jnp.zeros((tm, D_MODEL), F32)
    for c in range(D_FF // FF_CHUNK):
        cs = slice(c * FF_CHUNK, (c + 1) * FF_CHUNK)
        a = _dot(xb, wup_ref[:, cs])
        u = _dot(xb, wup_ref[:, D_FF + c * FF_CHUNK:D_FF + (c + 1) * FF_CHUNK])
        if long_seq:
            a_halo = _dot(halo_ref[...].astype(BF16), wup_ref[:, cs])
            prev = jnp.where(i % tiles_per_seq == 0, cp_ref[0][:, cs], a_halo)
            p1, p2 = prev[SUBLANES - 1:SUBLANES, :], prev[SUBLANES - 2:SUBLANES - 1, :]
            t = row
        else:
            p1, p2 = cp1_ref[:, cs], cp0_ref[:, cs]
            t = _mod(row, seq_len)
        am1 = jnp.where(t == 0, p1, pltpu.roll(a, 1, 0))
        am2 = jnp.where(t == 0, p2, jnp.where(t == 1, p1, pltpu.roll(a, 2, 0)))
        cw = cw_ref[:, cs]
        conv = cb_ref[:, cs] + cw[0:1, :] * am2 + cw[1:2, :] * am1 + cw[2:3, :] * a
        gelu = 0.5 * conv * (1.0 + lax.erf(conv * (2.0 ** -0.5)))
        acc = acc + _dot((gelu * u).astype(BF16), wdn_ref[cs, :])
        tail_ref[:, cs] = a[tm - tail_rows:, :]
    x2 = _layer_norm(ALPHA * x + acc, g2_ref[...], b2_ref[...])
    gate = _sigmoid(_dot(x2.astype(BF16), wpg_ref[...]))
    emb = _dot(pe_ref[...].astype(BF16), wpp_ref[...])
    y_ref[...] = _layer_norm(ALPHA * x2 + gate * emb, g3_ref[...], b3_ref[...])


def _ffn(x2d, conv_prev, pe2d, w_up, conv_w, conv_b, w_down, ln2_g, ln2_b, w_pg, w_pp, ln3_g, ln3_b, *, tm, seq_len):
    m = x2d.shape[0]
    n_seq = m // seq_len
    row = lambda i: (i, 0)
    n_tiles = m // tm
    if seq_len >= tm:
        tiles_per_seq, tail_rows = seq_len // tm, SUBLANES
        cp = jnp.concatenate([jnp.zeros((n_seq, SUBLANES - (CONV_W - 1), D_FF), F32), conv_prev], axis=1)
        acts = [x2d, x2d, cp, pe2d]
        act_specs = [pl.BlockSpec((tm, D_MODEL), row),
                     pl.BlockSpec((SUBLANES, D_MODEL), lambda i: (jnp.maximum(i * (tm // SUBLANES) - 1, 0), 0)),
                     pl.BlockSpec((1, SUBLANES, D_FF), lambda i: (i // tiles_per_seq, 0, 0)),
                     pl.BlockSpec((tm, PLE_DIM), row)]
        act_bytes = 2 * _nbytes((tm, D_MODEL), F32) + _nbytes((SUBLANES, D_FF), F32)
    else:
        assert tm % seq_len == 0 and seq_len >= CONV_W - 1
        tiles_per_seq, tail_rows = 1, tm
        cp0 = jnp.repeat(conv_prev[:, 0, :], seq_len, axis=0)
        cp1 = jnp.repeat(conv_prev[:, 1, :], seq_len, axis=0)
        acts = [x2d, cp0, cp1, pe2d]
        act_specs = [pl.BlockSpec((tm, D_MODEL), row), pl.BlockSpec((tm, D_FF), row), pl.BlockSpec((tm, D_FF), row),
                     pl.BlockSpec((tm, PLE_DIM), row)]
        act_bytes = _nbytes((tm, D_MODEL), F32) + 2 * _nbytes((tm, D_FF), F32)
    weights = [w_up, conv_w, conv_b, w_down, ln2_g, ln2_b, w_pg, w_pp, ln3_g, ln3_b]
    est = (2 * act_bytes + sum(_nbytes(w.shape, w.dtype) for w in weights) + 4 * _nbytes((tm, D_MODEL), F32)
           + 2 * _nbytes((tail_rows, D_FF), F32) + 6 * _nbytes((tm, FF_CHUNK), F32))
    y, tail = pl.pallas_call(
        functools.partial(_ffn_body, tm=tm, tiles_per_seq=tiles_per_seq, seq_len=seq_len, tail_rows=tail_rows),
        grid=(n_tiles,),
        in_specs=act_specs + [_resident(w.shape) for w in weights],
        out_specs=[pl.BlockSpec((tm, D_MODEL), row), pl.BlockSpec((tail_rows, D_FF), row)],
        out_shape=[jax.ShapeDtypeStruct((m, D_MODEL), F32), jax.ShapeDtypeStruct((n_tiles * tail_rows, D_FF), F32)],
        compiler_params=pltpu.CompilerParams(dimension_semantics=("arbitrary",), vmem_limit_bytes=_vmem_limit(est)),
        name="ffn",
    )(*acts, *weights)
    return y, tail


def _tile_rows(m):
    return min(m, 512)


def _mixers(x, s0, pos_offset, wts, *, prompt):
    n_seq, t, _ = x.shape
    m = n_seq * t
    x2d = x.reshape(m, D_MODEL)
    act_dtype = BF16 if prompt else F32
    tm = _tile_rows(m)
    q, k, v, g, ld = _gla_proj(x2d, wts["w_gla"], wts["w_lr"], wts["w_gkb"], wts["b_gk"], tm=min(m, 2 * tm),
                               act_dtype=act_dtype)
    if prompt:
        qkv, kv_new = _dil_proj(x2d, wts["w_dil"], wts["inv_freq"], tm=tm, seq_len=t, pos_offset=pos_offset, act_dtype=act_dtype)
        kv_out = [jnp.transpose(kvn.reshape(n_seq, 2, DIL_HEADS, DIL_HD, -1), (0, 4, 1, 2, 3)) for kvn in kv_new]
    else:
        qkv, kv_new = _dil_proj_short(x2d, wts["w_dil"], wts["inv_freq"], seq_len=t, pos_offset=pos_offset)
        kv_out = [jnp.transpose(kv_new[g].reshape(t, 2, DIL_HEADS, DIL_HD, n_seq), (4, 0, 1, 2, 3)) for g in range(N_GROUPS)]
    r3 = lambda a: a.reshape(n_seq, t, a.shape[-1])
    o_gla, s_new = _gla_scan(r3(q), r3(k), r3(v), r3(g), r3(ld), s0, wts["gla_norm"],
                             nseq=n_seq if prompt else 8, cc=min(t, GLA_CHUNK))
    return dict(x2d=x2d, tm=tm, n_seq=n_seq, t=t, o_gla=o_gla.reshape(m, GLA_V), s_new=s_new, qkv=qkv, kv_out=kv_out)


def _finish(mix, dil_inputs, pe, conv_prev, wts, *, prompt):
    n_seq, t, tm = mix["n_seq"], mix["t"], mix["tm"]
    m = n_seq * t
    x1 = _merge(mix["x2d"], mix["o_gla"], dil_inputs, wts["w_gates"], wts["w_br_gla"], wts["w_br_dil"], wts["w_out"],
                wts["expand"], wts["ln1_g"], wts["ln1_b"], tm=tm, seq_len=t)
    tm_ffn = tm if prompt else min(tm, 256)
    y, tail = _ffn(x1, conv_prev, pe.reshape(m, PLE_DIM), wts["w_up"], wts["conv_w"], wts["conv_b"], wts["w_down"],
                   wts["ln2_g"], wts["ln2_b"], wts["w_ple_gate"], wts["w_ple_proj"], wts["ln3_g"], wts["ln3_b"],
                   tm=tm_ffn, seq_len=t)
    if t >= tm_ffn:
        conv_new = tail.reshape(n_seq, t // tm_ffn, SUBLANES, D_FF)[:, -1, SUBLANES - (CONV_W - 1):, :]
    else:
        conv_new = tail.reshape(n_seq, t, D_FF)[:, t - (CONV_W - 1):, :]
    return y.reshape(n_seq, t, D_MODEL), conv_new


def _forward(x_p, x_s, pe_p, pe_s, s0_s, conv_s, caches, wts):
    n_p = x_p.shape[0]
    mix_p = _mixers(x_p, jnp.zeros((n_p, GLA_HEADS, GLA_DK, GLA_DV), s0_s.dtype), 0, wts, prompt=True)
    mix_s = _mixers(x_s, s0_s, PAST_LEN, wts, prompt=False)
    o_p, lse_p, o_s = _attention(mix_p["qkv"], mix_s["qkv"], caches, n_seq_p=n_p, seq_len=x_p.shape[1],
                                 n_seq_s=x_s.shape[0], t_new=x_s.shape[1])
    y_p, conv_new_p = _finish(mix_p, o_p + lse_p, pe_p, jnp.zeros((n_p, CONV_W - 1, D_FF), x_p.dtype), wts, prompt=True)
    y_s, conv_new_s = _finish(mix_s, [o_s], pe_s, conv_s, wts, prompt=False)
    return (y_p, mix_p["s_new"], conv_new_p, mix_p["kv_out"]), (y_s, mix_s["s_new"], conv_new_s, mix_s["kv_out"])


def _prep_weights(w_in, w_gk_b, b_gk, gla_norm, w_br_gla, w_br_dil, w_out, ln1_g, ln1_b, w_up, conv_w, conv_b, w_down,
                  ln2_g, ln2_b, w_ple_gate, w_ple_proj, ln3_g, ln3_b):
    w_in_t = jnp.transpose(w_in[0])
    rows_bf16 = lambda lo, hi: w_in_t[lo:hi].astype(BF16)
    vec = lambda a: a[0].reshape(1, -1).astype(F32)
    half = DIL_HD // 2
    inv = ROPE_THETA ** (-jnp.arange(half, dtype=F32) / half)
    head_of_lane = jnp.arange(DIL_WIDTH) // DIL_HD
    expand = (jnp.arange(LANES)[:, None] == head_of_lane[None, :]).astype(BF16)
    return {
        "w_gla": rows_bf16(0, COL_GLA),
        "w_lr": jnp.pad(rows_bf16(COL_GLA, COL_LR), ((0, LANES - GLA_RANK), (0, 0))),
        "w_gkb": jnp.pad(w_gk_b[0].astype(BF16), ((0, LANES - GLA_RANK), (0, 0))),
        "b_gk": vec(b_gk),
        "w_dil": rows_bf16(COL_LR, COL_DIL),
        "w_gates": rows_bf16(COL_DIL, w_in_t.shape[0]),
        "inv_freq": jnp.tile(inv, LANES // half).reshape(1, LANES),
        "gla_norm": vec(gla_norm),
        "w_br_gla": w_br_gla[0].astype(BF16), "w_br_dil": w_br_dil[0].astype(BF16), "w_out": w_out[0].astype(BF16),
        "expand": expand,
        "ln1_g": vec(ln1_g), "ln1_b": vec(ln1_b),
        "w_up": w_up[0].astype(BF16),
        "conv_w": jnp.pad(conv_w[0].astype(F32), ((0, SUBLANES - CONV_W), (0, 0))),
        "conv_b": vec(conv_b),
        "w_down": w_down[0].astype(BF16),
        "ln2_g": vec(ln2_g), "ln2_b": vec(ln2_b),
        "w_ple_gate": w_ple_gate[0].astype(BF16), "w_ple_proj": w_ple_proj[0].astype(BF16),
        "ln3_g": vec(ln3_g), "ln3_b": vec(ln3_b),
    }


def kernel(x_prompt, x_sample, p_prompt, p_sample, state_gla, cache_conv, cache_kv_w128, cache_kv_w512, cache_kv_w2048, w_in, w_gk_b, b_gk, gla_norm, w_br_gla, w_br_dil, w_out, ln1_g, ln1_b, w_up, conv_w, conv_b, w_down, ln2_g, ln2_b, w_ple_gate, w_ple_proj, ln3_g, ln3_b):
    assert w_in.shape[0] == DEPTH == 1
    wts = _prep_weights(w_in, w_gk_b, b_gk, gla_norm, w_br_gla, w_br_dil, w_out, ln1_g, ln1_b, w_up, conv_w, conv_b,
                        w_down, ln2_g, ln2_b, w_ple_gate, w_ple_proj, ln3_g, ln3_b)
    caches = (cache_kv_w128[0], cache_kv_w512[0], cache_kv_w2048[0])
    (y_p, s_p, conv_p, kv_p), (y_s, s_s, conv_s, kv_s) = _forward(
        x_prompt, x_sample, p_prompt[0], p_sample[0], state_gla[0], cache_conv[0], caches, wts)
    return (y_p, y_s, s_p[None], s_s[None], conv_p[None], conv_s[None],
            kv_p[0][None], kv_p[1][None], kv_p[2][None], kv_s[0][None], kv_s[1][None], kv_s[2][None])
```

```python
import functools

import jax
import jax.numpy as jnp
from jax import lax
from jax.experimental import pallas as pl
from jax.experimental.pallas import tpu as pltpu

F32 = jnp.float32
BF16 = jnp.bfloat16

D_MODEL = 1024
DEPTH = 1
PAST_LEN = 8192
GLA_HEADS = 4
GLA_DK = 128
GLA_DV = 256
GLA_QK = GLA_HEADS * GLA_DK
GLA_V = GLA_HEADS * GLA_DV
GLA_RANK = 16
GLA_NORMALIZER = 16.0
DIL_GROUPS = ((128, 1), (512, 4), (2048, 16))
N_GROUPS = len(DIL_GROUPS)
DIL_HEADS = 8
DIL_HD = 64
DIL_WIDTH = DIL_HEADS * DIL_HD
DIL_QKV = N_GROUPS * DIL_WIDTH
ROPE_THETA = 10000.0
D_FF = 2816
CONV_W = 3
PLE_DIM = 256
ALPHA = (2.0 * DEPTH) ** 0.25
NORM_EPS = 1e-5
COL_GLA = 2 * GLA_QK + 2 * GLA_V
COL_LR = COL_GLA + GLA_RANK
COL_DIL = COL_LR + 3 * DIL_QKV
NEG_FILL = -1e30

LANES = 128
SUBLANES = 8
VMEM_BYTES_V7X = 64 * 1024 * 1024
VMEM_LIMIT_CAP = 60000 * 1024

ATT_BLK = 128
GLA_CHUNK = 128
GLA_SUB = 16
FF_CHUNK = D_FF // 2
CACHE_RING = 3
REGROUP_STRIDE = 4


def _vmem_limit(nbytes):
    return int(min(max(2 * nbytes, 16 * 1024 * 1024), VMEM_LIMIT_CAP))


def _nbytes(shape, dtype):
    n = 1
    for s in shape:
        n *= s
    return n * jnp.dtype(dtype).itemsize


def _sigmoid(x):
    return 1.0 / (1.0 + jnp.exp(-x))


def _log_sigmoid(x):
    return jnp.minimum(x, 0.0) - jnp.log1p(jnp.exp(-jnp.abs(x)))


def _layer_norm(z, g, b):
    mu = jnp.mean(z, axis=-1, keepdims=True)
    d = z - mu
    var = jnp.mean(d * d, axis=-1, keepdims=True)
    return d * lax.rsqrt(var + NORM_EPS) * g + b


def _log2(n):
    assert n > 0 and n & (n - 1) == 0, n
    return n.bit_length() - 1


def _div(x, n):
    return x >> _log2(n)


def _mod(x, n):
    _log2(n)
    return x & (n - 1)


def _resident(shape):
    return pl.BlockSpec(shape, lambda *_: (0,) * len(shape), pipeline_mode=pl.Buffered(1))


def _dot(a, b):
    return jnp.dot(a, b, preferred_element_type=F32)


def _dot_nt(a, b):
    return lax.dot_general(a, b, (((1,), (1,)), ((), ())), preferred_element_type=F32)


def _gla_proj_body(x_ref, w_ref, wlr_ref, wgkb_ref, bgk_ref, q_ref, k_ref, v_ref, g_ref, ld_ref):
    xb = x_ref[...].astype(BF16)
    cw = GLA_QK

    def proj(lo):
        return _dot_nt(xb, w_ref[lo:lo + cw, :])

    glr = _dot_nt(xb, wlr_ref[...])
    z = _dot(glr.astype(BF16), wgkb_ref[...]) + bgk_ref[...]
    ld_ref[...] = _log_sigmoid(z) * (1.0 / GLA_NORMALIZER)
    q_ref[...] = (proj(0) * GLA_DK ** -0.5).astype(q_ref.dtype)
    k_ref[...] = proj(GLA_QK).astype(k_ref.dtype)
    for c in range(GLA_V // cw):
        v_ref[:, c * cw:(c + 1) * cw] = proj(2 * GLA_QK + c * cw).astype(v_ref.dtype)
        gg = proj(2 * GLA_QK + GLA_V + c * cw)
        g_ref[:, c * cw:(c + 1) * cw] = (gg * _sigmoid(gg)).astype(g_ref.dtype)


def _gla_proj(x2d, w_gla, w_lr, w_gkb, b_gk, *, tm, act_dtype):
    m = x2d.shape[0]
    row = lambda i: (i, 0)
    out_shape = [
        jax.ShapeDtypeStruct((m, GLA_QK), act_dtype), jax.ShapeDtypeStruct((m, GLA_QK), act_dtype),
        jax.ShapeDtypeStruct((m, GLA_V), act_dtype), jax.ShapeDtypeStruct((m, GLA_V), act_dtype),
        jax.ShapeDtypeStruct((m, GLA_QK), F32),
    ]
    weights = [w_gla, w_lr, w_gkb, b_gk]
    est = (2 * _nbytes((tm, D_MODEL), F32) + sum(_nbytes(w.shape, w.dtype) for w in weights)
           + 2 * sum(_nbytes((tm, s.shape[1]), s.dtype) for s in out_shape) + 2 * _nbytes((tm, GLA_QK), F32))
    return pl.pallas_call(
        _gla_proj_body,
        grid=(m // tm,),
        in_specs=[pl.BlockSpec((tm, D_MODEL), row)] + [_resident(w.shape) for w in weights],
        out_specs=[pl.BlockSpec((tm, s.shape[1]), row) for s in out_shape],
        out_shape=out_shape,
        compiler_params=pltpu.CompilerParams(dimension_semantics=("arbitrary",), vmem_limit_bytes=_vmem_limit(est)),
        name="gla_proj",
    )(x2d, w_gla, w_lr, w_gkb, b_gk)


def _dil_proj_body(x_ref, w_ref, inv_ref, *refs, tm, seq_len, pos_offset):
    qkv_refs, kv_refs = refs[:N_GROUPS], refs[N_GROUPS:2 * N_GROUPS]
    trig_ref, stage_ref, hop_ref = refs[2 * N_GROUPS:2 * N_GROUPS + 3]
    i = pl.program_id(0)
    xb = x_ref[...].astype(BF16)
    half = DIL_HD // 2
    tiles_per_seq = seq_len // tm
    it = _mod(i, tiles_per_seq)
    lane = lax.broadcasted_iota(jnp.int32, (tm, LANES), 1)

    @pl.when(i == 0)
    def _():
        rel = lax.broadcasted_iota(jnp.int32, (tm, LANES), 0).astype(F32) * inv_ref[...]
        trig_ref[0] = jnp.cos(rel)
        trig_ref[1] = jnp.sin(rel)

    base = (pos_offset + it * tm).astype(F32) * inv_ref[...]
    cos_b, sin_b = jnp.cos(base), jnp.sin(base)
    cos = cos_b * trig_ref[0] - sin_b * trig_ref[1]
    sin = sin_b * trig_ref[0] + cos_b * trig_ref[1]
    first_half = _mod(lane, DIL_HD) < half
    sin_signed = jnp.where(first_half, -sin, sin)

    def rope(xc):
        partner = jnp.where(first_half, pltpu.roll(xc, LANES - half, 1), pltpu.roll(xc, half, 1))
        return xc * cos + partner * sin_signed

    n_slab = DIL_WIDTH // LANES
    for g, (w, r) in enumerate(DIL_GROUPS):
        hq = _dot_nt(xb, w_ref[g * DIL_WIDTH:(g + 1) * DIL_WIDTH, :])
        hk = _dot_nt(xb, w_ref[DIL_QKV + g * DIL_WIDTH:DIL_QKV + (g + 1) * DIL_WIDTH, :])
        hv = _dot_nt(xb, w_ref[2 * DIL_QKV + g * DIL_WIDTH:2 * DIL_QKV + (g + 1) * DIL_WIDTH, :])
        slabs = []
        for part, h in enumerate((hq, hk, hv)):
            for c in range(n_slab):
                xc = h[:, c * LANES:(c + 1) * LANES]
                slabs.append(xc if part == 2 else rope(xc) * (DIL_HD ** -0.5 if part == 0 else 1.0))
        for j, xc in enumerate(slabs):
            if r == 1:
                qkv_refs[g][0, 0, :, j * LANES:(j + 1) * LANES] = xc.astype(qkv_refs[g].dtype)
            else:
                stage_ref[g - 1, j] = xc
        src_ref, r2 = stage_ref.at[g - 1], r
        if r > REGROUP_STRIDE:
            r2, n1 = r // REGROUP_STRIDE, tm // REGROUP_STRIDE
            assert r2 <= REGROUP_STRIDE
            for j in range(3 * n_slab):
                for c1 in range(REGROUP_STRIDE):
                    hop_ref[j, c1 * n1:(c1 + 1) * n1, :] = src_ref[j, pl.ds(c1, n1, stride=REGROUP_STRIDE), :]
            src_ref = hop_ref
        for c in range(r if r > 1 else 0):
            c1, c2 = (c % REGROUP_STRIDE, c // REGROUP_STRIDE) if r > REGROUP_STRIDE else (0, c)
            first = c1 * (tm // REGROUP_STRIDE) + c2
            for j in range(3 * n_slab):
                qkv_refs[g][0, c, :, j * LANES:(j + 1) * LANES] = (
                    src_ref[j, pl.ds(first, tm // r, stride=r2), :].astype(qkv_refs[g].dtype))
        keep = min(w, tm)

        @pl.when(it >= tiles_per_seq - w // keep)
        def _(g=g, keep=keep, slabs=slabs):
            for j in range(2 * n_slab):
                kv_refs[g][0, j // n_slab, (j % n_slab) * LANES:(j % n_slab + 1) * LANES, :] = (
                    slabs[n_slab + j][tm - keep:, :].T)


def _dil_proj(x2d, w_dil, inv_freq, *, tm, seq_len, pos_offset, act_dtype):
    m = x2d.shape[0]
    n_seq = m // seq_len
    assert seq_len % tm == 0 and DIL_GROUPS[0][1] == 1 and all(r > 1 for _, r in DIL_GROUPS[1:])
    row = lambda i: (i, 0)
    tiles_per_seq = seq_len // tm
    qkv_shapes, qkv_specs, kv_shapes, kv_specs = [], [], [], []
    for w, r in DIL_GROUPS:
        assert tm % r == 0
        keep = min(w, tm)
        nblk = w // keep
        qkv_shapes.append(jax.ShapeDtypeStruct((n_seq, r, seq_len // r, 3 * DIL_WIDTH), act_dtype))
        qkv_specs.append(pl.BlockSpec((1, r, tm // r, 3 * DIL_WIDTH),
                                      lambda i: (i // tiles_per_seq, 0, i % tiles_per_seq, 0)))
        kv_shapes.append(jax.ShapeDtypeStruct((n_seq, 2, DIL_WIDTH, w), F32))
        kv_specs.append(pl.BlockSpec(
            (1, 2, DIL_WIDTH, keep),
            functools.partial(lambda i, nblk: (i // tiles_per_seq, 0, 0,
                                               jnp.maximum(i % tiles_per_seq - (tiles_per_seq - nblk), 0)), nblk=nblk)))
    n_slab = 3 * DIL_WIDTH // LANES
    scratch = [pltpu.VMEM((2, tm, LANES), F32),
               pltpu.VMEM((N_GROUPS - 1, n_slab, tm, LANES), F32),
               pltpu.VMEM((n_slab, tm, LANES), F32)]
    est = (2 * _nbytes((tm, D_MODEL), F32) + _nbytes(w_dil.shape, BF16)
           + 2 * N_GROUPS * _nbytes((tm, 3 * DIL_WIDTH), act_dtype)
           + 2 * sum(_nbytes(s.block_shape, F32) for s in kv_specs) + 3 * _nbytes((tm, DIL_WIDTH), F32)
           + _nbytes((N_GROUPS, tm, 3 * DIL_WIDTH), F32))
    body = functools.partial(_dil_proj_body, tm=tm, seq_len=seq_len, pos_offset=pos_offset)
    outs = pl.pallas_call(
        body,
        grid=(m // tm,),
        in_specs=[pl.BlockSpec((tm, D_MODEL), row), _resident(w_dil.shape), _resident(inv_freq.shape)],
        out_specs=qkv_specs + kv_specs,
        out_shape=qkv_shapes + kv_shapes,
        scratch_shapes=scratch,
        compiler_params=pltpu.CompilerParams(dimension_semantics=("arbitrary",), vmem_limit_bytes=_vmem_limit(est)),
        name="dil_proj",
    )(x2d, w_dil, inv_freq)
    return outs[:N_GROUPS], outs[N_GROUPS:]


def _dil_proj_short_body(x_ref, wq_ref, wk_ref, wv_ref, inv_ref, qkv_ref, kv_ref, stage_ref, *, seq_len, pos_offset):
    m = x_ref.shape[0]
    n_seq = m // seq_len
    xb = x_ref[...].astype(BF16)
    half = DIL_HD // 2
    row = lax.broadcasted_iota(jnp.int32, (m, LANES), 0)
    lane = lax.broadcasted_iota(jnp.int32, (m, LANES), 1)
    ang = (pos_offset + _mod(row, seq_len)).astype(F32) * inv_ref[...]
    cos, sin = jnp.cos(ang), jnp.sin(ang)
    first_half = _mod(lane, DIL_HD) < half
    sin_signed = jnp.where(first_half, -sin, sin)
    n_slab = DIL_WIDTH // LANES
    for part, w_ref in enumerate((wq_ref, wk_ref, wv_ref)):
        h = _dot_nt(xb, w_ref[...])
        for c in range(n_slab):
            xc = h[:, c * LANES:(c + 1) * LANES]
            if part < 2:
                partner = jnp.where(first_half, pltpu.roll(xc, LANES - half, 1), pltpu.roll(xc, half, 1))
                xc = (xc * cos + partner * sin_signed) * (DIL_HD ** -0.5 if part == 0 else 1.0)
            j = part * n_slab + c
            qkv_ref[0, :, j * LANES:(j + 1) * LANES] = xc
            if part:
                stage_ref[j - n_slab] = xc
    for t in range(seq_len):
        for j in range(2 * n_slab):
            kv_ref[0, t, j * LANES:(j + 1) * LANES, :] = stage_ref[j, pl.ds(t, n_seq, stride=seq_len), :].T


def _dil_proj_short(x2d, w_dil, inv_freq, *, seq_len, pos_offset):
    m = x2d.shape[0]
    n_seq = m // seq_len
    assert n_seq % SUBLANES == 0 and all(w >= seq_len for w, _ in DIL_GROUPS)
    w_spec = lambda part: pl.BlockSpec((DIL_WIDTH, D_MODEL), lambda g: (part * N_GROUPS + g, 0))
    est = (_nbytes((m, D_MODEL), F32) + 6 * _nbytes((DIL_WIDTH, D_MODEL), BF16) + 2 * _nbytes((m, 3 * DIL_WIDTH), F32)
           + 2 * _nbytes((seq_len, 2 * DIL_WIDTH, n_seq), F32) + _nbytes((m, 2 * DIL_WIDTH), F32) + 4 * _nbytes((m, DIL_WIDTH), F32))
    return pl.pallas_call(
        functools.partial(_dil_proj_short_body, seq_len=seq_len, pos_offset=pos_offset),
        grid=(N_GROUPS,),
        in_specs=[_resident((m, D_MODEL)), w_spec(0), w_spec(1), w_spec(2), _resident(inv_freq.shape)],
        out_specs=[pl.BlockSpec((1, m, 3 * DIL_WIDTH), lambda g: (g, 0, 0)),
                   pl.BlockSpec((1, seq_len, 2 * DIL_WIDTH, n_seq), lambda g: (g, 0, 0, 0))],
        out_shape=[jax.ShapeDtypeStruct((N_GROUPS, m, 3 * DIL_WIDTH), F32),
                   jax.ShapeDtypeStruct((N_GROUPS, seq_len, 2 * DIL_WIDTH, n_seq), F32)],
        scratch_shapes=[pltpu.VMEM((2 * DIL_WIDTH // LANES, m, LANES), F32)],
        compiler_params=pltpu.CompilerParams(dimension_semantics=("arbitrary",), vmem_limit_bytes=_vmem_limit(est)),
        name="dil_proj_short",
    )(x2d, w_dil, w_dil, w_dil, inv_freq)


def _cumsum_rows(x):
    n = x.shape[0]
    row = lax.broadcasted_iota(jnp.int32, x.shape, 0)
    s = 1
    while s < n:
        x = x + jnp.where(row >= s, pltpu.roll(x, s, 0), 0.0)
        s *= 2
    return x


def _pad_rows(x, n):
    if x.shape[0] == n:
        return x
    return jnp.concatenate([x, jnp.zeros((n - x.shape[0], x.shape[1]), x.dtype)], axis=0)


def _gla_scan_body(q_ref, k_ref, v_ref, g_ref, ld_ref, s0_ref, nw_ref, o_ref, sout_ref, *scratch, nseq, cc, carry):
    chunk = pl.program_id(1)
    sub = GLA_SUB
    cp = cc if cc % sub == 0 else sub * (cc // sub + 1)
    nsub = cp // sub
    st_ref = scratch[0] if carry else None

    if carry:
        @pl.when(chunk == 0)
        def _():
            for s in range(nseq):
                for h in range(GLA_HEADS):
                    st_ref[s * GLA_HEADS + h] = s0_ref[s, h].T

    arow = lax.broadcasted_iota(jnp.int32, (cp, cp), 0)
    acol = lax.broadcasted_iota(jnp.int32, (cp, cp), 1)
    for s in range(nseq):
        q = _pad_rows(q_ref[s].astype(F32), cp)
        k = _pad_rows(k_ref[s].astype(F32), cp)
        b = _cumsum_rows(_pad_rows(ld_ref[s], cp))
        b_last = b[cp - 1:cp, :]
        starts = [jnp.zeros((1, GLA_QK), F32)] + [b[i * sub - 1:i * sub, :] for i in range(1, nsub)]
        ends = starts[1:] + [b_last]
        rows_of = lambda vs: jnp.concatenate([jnp.broadcast_to(v, (sub, GLA_QK)) for v in vs], axis=0)
        r, e_own = rows_of(starts), rows_of(ends)
        qh = q * jnp.exp(b - r)
        qt = (qh * jnp.exp(r)).astype(BF16)
        kl32 = k * jnp.exp(b_last - b)
        kl = kl32.astype(BF16)
        k_end = k * jnp.exp(e_own - b)
        k_diag = (k * jnp.exp(r - b)).astype(BF16)
        a_rows = [[] for _ in range(GLA_HEADS)]
        for i in range(nsub):
            pieces = []
            if i:
                hop = jnp.exp(starts[i] - jnp.concatenate(ends[:i], axis=0))
                pieces = [(k_end[j * sub:(j + 1) * sub, :] * hop[j:j + 1, :]).astype(BF16) for j in range(i)]
            pieces.append(k_diag[i * sub:(i + 1) * sub, :])
            if i + 1 < nsub:
                pieces.append(jnp.zeros((cp - (i + 1) * sub, GLA_QK), BF16))
            ki = jnp.concatenate(pieces, axis=0) if len(pieces) > 1 else pieces[0]
            qi = qh[i * sub:(i + 1) * sub, :].astype(BF16)
            for h in range(GLA_HEADS):
                hs = slice(h * GLA_DK, (h + 1) * GLA_DK)
                a_rows[h].append(_dot_nt(qi[:, hs], ki[:, hs]))
        d_last = jnp.exp(b_last)
        for h in range(GLA_HEADS):
            hs = slice(h * GLA_DK, (h + 1) * GLA_DK)
            vs = slice(h * GLA_DV, (h + 1) * GLA_DV)
            a = jnp.where(acol <= arow, jnp.concatenate(a_rows[h], axis=0), 0.0).astype(BF16)
            v = _pad_rows(v_ref[s][:, vs].astype(F32), cp)
            if carry:
                st = st_ref[s * GLA_HEADS + h]
                inter = _dot_nt(qt[:, hs], st.astype(BF16))
            else:
                st = s0_ref[s, h]
                inter = _dot(qt[:, hs], st.astype(BF16))
            o = (inter + _dot(a, v.astype(BF16)))[:cc]
            o = o * lax.rsqrt(jnp.mean(o * o, axis=-1, keepdims=True) + NORM_EPS) * nw_ref[...]
            o_ref[s, :, vs] = (o * g_ref[s][:, vs].astype(F32)).astype(o_ref.dtype)
            v_tile = _pad_rows(v, GLA_CHUNK)
            if carry:
                st_ref[s * GLA_HEADS + h] = d_last[:, hs] * st + _dot(v_tile.T.astype(BF16), _pad_rows(kl[:, hs], GLA_CHUNK))
            else:
                aug = jnp.concatenate([kl32[:, hs], jnp.broadcast_to(d_last[:, hs], (SUBLANES, GLA_DK)),
                                       jnp.zeros((GLA_CHUNK - cp - SUBLANES, GLA_DK), F32)], axis=0).T
                sout_ref[s, h] = aug[:, cp:cp + 1] * st + _dot(aug.astype(BF16), v_tile.astype(BF16))

    if carry:
        @pl.when(chunk == pl.num_programs(1) - 1)
        def _():
            for s in range(nseq):
                for h in range(GLA_HEADS):
                    sout_ref[s, h] = st_ref[s * GLA_HEADS + h].T


def _gla_scan(q, k, v, g, ld, s0, norm_w, *, nseq, cc):
    n_seq, t, _ = q.shape
    assert n_seq % nseq == 0 and t % cc == 0 and cc <= GLA_CHUNK
    blk = lambda sg, c: (sg, c, 0)
    sblk = lambda sg, c: (sg, 0, 0, 0)
    carry = not (t == cc and cc + GLA_SUB + SUBLANES <= GLA_CHUNK)
    body = functools.partial(_gla_scan_body, nseq=nseq, cc=cc, carry=carry)
    est = (2 * nseq * cc * (2 * GLA_QK + 2 * GLA_V) * jnp.dtype(q.dtype).itemsize + 2 * nseq * cc * GLA_QK * 4
           + 2 * nseq * cc * GLA_V * jnp.dtype(q.dtype).itemsize + 5 * nseq * GLA_HEADS * GLA_DK * GLA_DV * 4)
    return pl.pallas_call(
        body,
        grid=(n_seq // nseq, t // cc),
        in_specs=[pl.BlockSpec((nseq, cc, GLA_QK), blk), pl.BlockSpec((nseq, cc, GLA_QK), blk),
                  pl.BlockSpec((nseq, cc, GLA_V), blk), pl.BlockSpec((nseq, cc, GLA_V), blk),
                  pl.BlockSpec((nseq, cc, GLA_QK), blk),
                  pl.BlockSpec((nseq, GLA_HEADS, GLA_DK, GLA_DV), sblk),
                  pl.BlockSpec(norm_w.shape, lambda sg, c: (0, 0))],
        out_specs=[pl.BlockSpec((nseq, cc, GLA_V), blk), pl.BlockSpec((nseq, GLA_HEADS, GLA_DK, GLA_DV), sblk)],
        out_shape=[jax.ShapeDtypeStruct((n_seq, t, GLA_V), q.dtype),
                   jax.ShapeDtypeStruct((n_seq, GLA_HEADS, GLA_DK, GLA_DV), F32)],
        scratch_shapes=[pltpu.VMEM((nseq * GLA_HEADS, GLA_DV, GLA_DK), F32)] if carry else [],
        compiler_params=pltpu.CompilerParams(dimension_semantics=("arbitrary", "arbitrary"),
                                             vmem_limit_bytes=_vmem_limit(est)),
        name="gla_scan",
    )(q, k, v, g, ld, s0, norm_w)


def _dil_attn_block(ins, outs, prev_ref, carry_ref, n, blocks_per_residue):
    row = lax.broadcasted_iota(jnp.int32, (ATT_BLK, 2 * ATT_BLK), 0)
    col = lax.broadcasted_iota(jnp.int32, (ATT_BLK, 2 * ATT_BLK), 1)
    dist = row + ATT_BLK - col
    band = (dist >= 0) & (dist <= ATT_BLK)
    lane = lax.broadcasted_iota(jnp.int32, (ATT_BLK, LANES), 1)
    for g in range(N_GROUPS):
        q_ref, kc_ref, vc_ref = ins[3 * g:3 * g + 3]
        o_ref, lse_ref = outs[2 * g:2 * g + 2]
        jb = _mod(n, blocks_per_residue[g])
        valid = band & ((col >= ATT_BLK) | (jb > 0))
        q, k_cur, v_cur = q_ref[0, 0], kc_ref[0, 0], vc_ref[0, 0]
        kk = jnp.concatenate([prev_ref[g, 0], k_cur], axis=0)
        vv = jnp.concatenate([prev_ref[g, 1], v_cur], axis=0)
        carry_ref[g, 0] = k_cur
        carry_ref[g, 1] = v_cur
        lse_tile = jnp.zeros((ATT_BLK, LANES), F32)
        for p in range(DIL_WIDTH // LANES):
            ls = slice(p * LANES, (p + 1) * LANES)
            q2, k2, v2 = q[:, ls], kk[:, ls], vv[:, ls]
            o_pair = jnp.zeros((ATT_BLK, LANES), F32)
            for e in range(LANES // DIL_HD):
                sel = (lane >= DIL_HD) if e else (lane < DIL_HD)
                s = _dot_nt(jnp.where(sel, q2, jnp.zeros_like(q2)), k2)
                s = jnp.where(valid, s, NEG_FILL)
                m = jnp.max(s, axis=-1, keepdims=True)
                pe = jnp.exp(s - m)
                den = jnp.sum(pe, axis=-1, keepdims=True)
                pv = _dot(pe.astype(BF16), v2)
                o_pair = jnp.where(sel, pv * (1.0 / den), o_pair)
                lse_tile = jnp.where(lane == p * (LANES // DIL_HD) + e, m + jnp.log(den), lse_tile)
            o_ref[0, 0, :, ls] = o_pair.astype(o_ref.dtype)
        lse_ref[0, 0] = lse_tile


def _dec_key_masks(mask_ref, t_new):
    nq = DIL_HEADS * t_new
    off = 0
    for w, r in DIL_GROUPS:
        t_q = _mod(lax.broadcasted_iota(jnp.int32, (nq, w), 0), t_new)
        j = lax.broadcasted_iota(jnp.int32, (nq, w), 1)
        mask_ref[:, off:off + w] = ((j >= t_q) & (_mod(w + t_q - j, r) == 0)).astype(F32)
        t_n = _mod(lax.broadcasted_iota(jnp.int32, (nq, LANES), 0), t_new)
        u = lax.broadcasted_iota(jnp.int32, (nq, LANES), 1)
        ok_n = (u <= t_n) & (_mod(t_n - u, r) == 0)
        mask_ref[:, off + w:off + w + LANES] = ok_n.astype(F32)
        off += w + LANES


def _dec_attn_seq(qkv_refs, c_refs, o_ref, mask_ref, t_new):
    nq = DIL_HEADS * t_new
    off = 0
    lane_w = lax.broadcasted_iota(jnp.int32, (nq, DIL_WIDTH), 1)
    row_w = lax.broadcasted_iota(jnp.int32, (nq, DIL_WIDTH), 0)
    own_head = _div(lane_w, DIL_HD) == _div(row_w, t_new)
    scores, values = [], []
    for g, (w, r) in enumerate(DIL_GROUPS):
        qkv = qkv_refs[g][0, 0].astype(F32)
        q, k_new, v_new = qkv[:, :DIL_WIDTH], qkv[:, DIL_WIDTH:2 * DIL_WIDTH], qkv[:, 2 * DIL_WIDTH:]
        q_rep = jnp.broadcast_to(q[None], (DIL_HEADS, t_new, DIL_WIDTH)).reshape(nq, DIL_WIDTH)
        q_bd = jnp.where(own_head, q_rep, 0.0).astype(BF16)
        kt_c, vt_c = c_refs[g][0].astype(BF16), c_refs[g][1].astype(BF16)
        ok_c = mask_ref[:, off:off + w] > 0.5
        ok_n = mask_ref[:, off + w:off + w + LANES] > 0.5
        off += w + LANES
        scores.append(jnp.where(ok_c, _dot(q_bd, kt_c), NEG_FILL))
        values.append(vt_c)
        scores.append(jnp.where(ok_n, _dot_nt(q_bd, _pad_rows(k_new, LANES).astype(BF16)), NEG_FILL))
        values.append(_pad_rows(v_new, LANES).astype(BF16))
    m = functools.reduce(jnp.maximum, [jnp.max(s, axis=-1, keepdims=True) for s in scores])
    den = jnp.zeros((nq, 1), F32)
    acc = jnp.zeros((nq, DIL_WIDTH), F32)
    for idx, (s, v) in enumerate(zip(scores, values)):
        pe = jnp.exp(s - m)
        den = den + jnp.sum(pe, axis=-1, keepdims=True)
        pe = pe.astype(BF16)
        acc = acc + (_dot(pe, v) if idx % 2 else _dot_nt(pe, v))
    acc = jnp.where(own_head, acc * (1.0 / den), 0.0)
    o_ref[0] = jnp.sum(acc.reshape(DIL_HEADS, t_new, DIL_WIDTH), axis=0)


def _cache_copy(c_hbm, buf, sem, g, seq, slot):
    return pltpu.make_async_copy(c_hbm[g].at[seq], buf[g].at[slot], sem.at[g, slot])


def _attention_body(*refs, blocks_per_residue, n_blocks, t_new):
    n_p = 3 * N_GROUPS
    ins_p, qkv_s, c_hbm = refs[:n_p], refs[n_p:n_p + N_GROUPS], refs[n_p + N_GROUPS:n_p + 2 * N_GROUPS]
    outs_p, o_s = refs[n_p + 2 * N_GROUPS:n_p + 4 * N_GROUPS], refs[n_p + 4 * N_GROUPS]
    bufs, sem = refs[n_p + 4 * N_GROUPS + 1:n_p + 5 * N_GROUPS + 1], refs[n_p + 5 * N_GROUPS + 1]
    prev_ref, mask_ref = refs[n_p + 5 * N_GROUPS + 2], refs[n_p + 5 * N_GROUPS + 3]
    i, n_steps = pl.program_id(0), pl.num_programs(0)

    @pl.when(i == 0)
    def _():
        for s in range(CACHE_RING - 1):
            for g in range(N_GROUPS):
                _cache_copy(c_hbm, bufs, sem, g, s, s).start()
        prev_ref[...] = jnp.zeros(prev_ref.shape, prev_ref.dtype)
        _dec_key_masks(mask_ref, t_new)

    ahead = i + (CACHE_RING - 1)

    @pl.when(ahead < n_steps)
    def _():
        for g in range(N_GROUPS):
            _cache_copy(c_hbm, bufs, sem, g, ahead, ahead % CACHE_RING).start()

    slot = i % CACHE_RING
    for g in range(N_GROUPS):
        _cache_copy(c_hbm, bufs, sem, g, i, slot).wait()
    parity = _mod(i, 2)
    _dil_attn_block(ins_p, outs_p, prev_ref.at[1 - parity], prev_ref.at[parity], _mod(i, n_blocks), blocks_per_residue)
    _dec_attn_seq(qkv_s, [bufs[g].at[slot] for g in range(N_GROUPS)], o_s, mask_ref, t_new)


def _attention(qkv_p, qkv_s, caches, *, n_seq_p, seq_len, n_seq_s, t_new):
    args, in_specs, out_specs, out_shape, bpr, scratch = [], [], [], [], [], []
    n_blocks = seq_len // ATT_BLK
    assert n_seq_s == n_seq_p * n_blocks
    assert n_seq_s >= CACHE_RING
    for g, (w, r) in enumerate(DIL_GROUPS):
        assert w // r == ATT_BLK and seq_len % (r * ATT_BLK) == 0
        nb = n_blocks // r
        bpr.append(nb)
        cur = lambda i, nb, part: (i // n_blocks, (i % n_blocks) // nb, i % nb, part)
        blk = (1, 1, ATT_BLK, DIL_WIDTH)
        for part in range(3):
            args.append(qkv_p[g])
            in_specs.append(pl.BlockSpec(blk, functools.partial(cur, nb=nb, part=part)))
        omap = functools.partial(cur, nb=nb, part=0)
        out_specs += [pl.BlockSpec(blk, omap), pl.BlockSpec((1, 1, ATT_BLK, LANES), omap)]
        out_shape += [jax.ShapeDtypeStruct((n_seq_p, r, seq_len // r, DIL_WIDTH), qkv_p[g].dtype),
                      jax.ShapeDtypeStruct((n_seq_p, r, seq_len // r, LANES), F32)]
    est = 2 * N_GROUPS * (6 * _nbytes((ATT_BLK, DIL_WIDTH), qkv_p[0].dtype) + _nbytes((ATT_BLK, LANES), F32))
    for g in range(N_GROUPS):
        args.append(qkv_s.reshape(N_GROUPS, n_seq_s, t_new, 3 * DIL_WIDTH))
        in_specs.append(pl.BlockSpec((1, 1, t_new, 3 * DIL_WIDTH), functools.partial(lambda i, g: (g, i, 0, 0), g=g)))
    for g, (w, r) in enumerate(DIL_GROUPS):
        args.append(jnp.transpose(caches[g], (0, 2, 3, 4, 1)).reshape(n_seq_s, 2, DIL_WIDTH, w))
        in_specs.append(pl.BlockSpec(memory_space=pl.ANY))
        scratch.append(pltpu.VMEM((CACHE_RING, 2, DIL_WIDTH, w), F32))
        est += CACHE_RING * _nbytes((2, DIL_WIDTH, w), F32) + 3 * _nbytes((DIL_HEADS * t_new + DIL_WIDTH, w), F32)
    scratch.append(pltpu.SemaphoreType.DMA((N_GROUPS, CACHE_RING)))
    scratch.append(pltpu.VMEM((2, N_GROUPS, 2, ATT_BLK, DIL_WIDTH), qkv_p[0].dtype))
    scratch.append(pltpu.VMEM((DIL_HEADS * t_new, sum(w + LANES for w, _ in DIL_GROUPS)), F32))
    out_specs.append(pl.BlockSpec((1, t_new, DIL_WIDTH), lambda i: (i, 0, 0)))
    out_shape.append(jax.ShapeDtypeStruct((n_seq_s, t_new, DIL_WIDTH), F32))
    outs = pl.pallas_call(
        functools.partial(_attention_body, blocks_per_residue=tuple(bpr), n_blocks=n_blocks, t_new=t_new),
        grid=(n_seq_s,),
        in_specs=in_specs, out_specs=out_specs, out_shape=out_shape, scratch_shapes=scratch,
        compiler_params=pltpu.CompilerParams(dimension_semantics=("arbitrary",), vmem_limit_bytes=_vmem_limit(est)),
        name="attention",
    )(*args)
    return ([outs[2 * g] for g in range(N_GROUPS)], [outs[2 * g + 1] for g in range(N_GROUPS)],
            outs[2 * N_GROUPS].reshape(n_seq_s * t_new, DIL_WIDTH))


def _row_order(src_ref, stage_ref, hop_ref, r):
    if r == 1:
        return src_ref[0, 0].astype(F32)
    n = src_ref.shape[2]
    rows = n * r
    n_slab = src_ref.shape[3] // LANES
    two_hops = r > REGROUP_STRIDE
    dst_ref, r2 = (hop_ref, r // REGROUP_STRIDE) if two_hops else (stage_ref, r)
    for c in range(r):
        c1, c2 = (c % REGROUP_STRIDE, c // REGROUP_STRIDE) if two_hops else (0, c)
        first = c1 * (rows // REGROUP_STRIDE) + c2
        for j in range(n_slab):
            dst_ref[j, pl.ds(first, n, stride=r2), :] = src_ref[0, c, :, j * LANES:(j + 1) * LANES].astype(F32)
    if two_hops:
        n1 = rows // REGROUP_STRIDE
        for j in range(n_slab):
            for c1 in range(REGROUP_STRIDE):
                stage_ref[j, pl.ds(c1, n1, stride=REGROUP_STRIDE), :] = hop_ref[j, c1 * n1:(c1 + 1) * n1, :]
    slabs = [stage_ref[j] for j in range(n_slab)]
    return slabs[0] if n_slab == 1 else jnp.concatenate(slabs, axis=1)


def _merge_body(*refs, n_dil_inputs):
    x_ref, og_ref = refs[:2]
    dil_refs = refs[2:2 + n_dil_inputs]
    (wg_ref, wa_ref, wb_ref, wo_ref, ex_ref, g_ref, b_ref, y_ref) = refs[2 + n_dil_inputs:10 + n_dil_inputs]
    x = x_ref[...]
    xb = x.astype(BF16)
    if n_dil_inputs == 1:
        yb_in = dil_refs[0][...].astype(F32)
    else:
        ostage_ref, lstage_ref = refs[10 + n_dil_inputs:]
        bufs = lambda ref, g: (None, None) if g == 0 else (ref.at[2 * (g - 1)], ref.at[2 * (g - 1) + 1])
        os_ = [_row_order(dil_refs[g], *bufs(ostage_ref, g), DIL_GROUPS[g][1]) for g in range(N_GROUPS)]
        lses = [_row_order(dil_refs[N_GROUPS + g], *bufs(lstage_ref, g), DIL_GROUPS[g][1]) for g in range(N_GROUPS)]
        top = functools.reduce(jnp.maximum, lses)
        es = [jnp.exp(l - top) for l in lses]
        inv = 1.0 / functools.reduce(jnp.add, es)
        yb_in = os_[0]
        for o, e in zip(os_[1:], es[1:]):
            wgt = e * inv
            hi = wgt.astype(BF16)
            lo = (wgt - hi.astype(F32)).astype(BF16)
            wide = _dot(hi, ex_ref[...]) + _dot(lo, ex_ref[...])
            yb_in = yb_in + wide * (o - os_[0])
    ya = _dot(og_ref[...].astype(BF16), wa_ref[...])
    yb = _dot(yb_in.astype(BF16), wb_ref[...])
    ga = _dot_nt(xb, wg_ref[:D_MODEL, :])
    gb = _dot_nt(xb, wg_ref[D_MODEL:, :])
    mix = _sigmoid(ga) * ya + _sigmoid(gb) * yb
    z = ALPHA * x + _dot(mix.astype(BF16), wo_ref[...])
    y_ref[...] = _layer_norm(z, g_ref[...], b_ref[...])


def _merge(x2d, o_gla, dil_inputs, w_gates, w_br_gla, w_br_dil, w_out, expand, ln_g, ln_b, *, tm, seq_len):
    m = x2d.shape[0]
    row = lambda i: (i, 0)
    tiles_per_seq = max(seq_len // tm, 1)
    acts, act_specs, scratch = [x2d, o_gla], [pl.BlockSpec((tm, D_MODEL), row), pl.BlockSpec((tm, GLA_V), row)], []
    est = 2 * _nbytes((tm, D_MODEL), F32) + 2 * _nbytes((tm, GLA_V), o_gla.dtype)
    for a in dil_inputs:
        acts.append(a)
        if a.ndim == 2:
            act_specs.append(pl.BlockSpec((tm, a.shape[1]), row))
        else:
            r = a.shape[1]
            assert seq_len % tm == 0 and tm % r == 0
            act_specs.append(pl.BlockSpec((1, r, tm // r, a.shape[3]), lambda i: (i // tiles_per_seq, 0, i % tiles_per_seq, 0)))
        est += 2 * _nbytes((tm, a.shape[-1]), a.dtype)
    if len(dil_inputs) > 1:
        scratch = [pltpu.VMEM((2 * (N_GROUPS - 1), DIL_WIDTH // LANES, tm, LANES), F32),
                   pltpu.VMEM((2 * (N_GROUPS - 1), 1, tm, LANES), F32)]
        est += 2 * (N_GROUPS - 1) * _nbytes((tm, DIL_WIDTH + LANES), F32)
    weights = [w_gates, w_br_gla, w_br_dil, w_out, expand, ln_g, ln_b]
    est += sum(_nbytes(w.shape, w.dtype) for w in weights) + 6 * _nbytes((tm, D_MODEL), F32)
    return pl.pallas_call(
        functools.partial(_merge_body, n_dil_inputs=len(dil_inputs)),
        grid=(m // tm,),
        in_specs=act_specs + [_resident(w.shape) for w in weights],
        out_specs=pl.BlockSpec((tm, D_MODEL), row),
        out_shape=jax.ShapeDtypeStruct((m, D_MODEL), F32),
        scratch_shapes=scratch,
        compiler_params=pltpu.CompilerParams(dimension_semantics=("arbitrary",), vmem_limit_bytes=_vmem_limit(est)),
        name="merge",
    )(*acts, *weights)


def _ffn_body(*refs, tm, tiles_per_seq, seq_len, tail_rows):
    long_seq = seq_len >= tm
    if long_seq:
        x_ref, halo_ref, cp_ref, pe_ref = refs[:4]
    else:
        x_ref, cp0_ref, cp1_ref, pe_ref = refs[:4]
    (wup_ref, cw_ref, cb_ref, wdn_ref, g2_ref, b2_ref, wpg_ref, wpp_ref, g3_ref, b3_ref, y_ref, tail_ref) = refs[4:]
    i = pl.program_id(0)
    x = x_ref[...]
    xb = x.astype(BF16)
    row = lax.broadcasted_iota(jnp.int32, (tm, FF_CHUNK), 0)
    acc = jnp.zeros((tm, D_MODEL), F32)
    for c in range(D_FF // FF_CHUNK):
        cs = slice(c * FF_CHUNK, (c + 1) * FF_CHUNK)
        a = _dot(xb, wup_ref[:, cs])
        u = _dot(xb, wup_ref[:, D_FF + c * FF_CHUNK:D_FF + (c + 1) * FF_CHUNK])
        if long_seq:
            a_halo = _dot(halo_ref[...].astype(BF16), wup_ref[:, cs])
            prev = jnp.where(i % tiles_per_seq == 0, cp_ref[0][:, cs], a_halo)
            p1, p2 = prev[SUBLANES - 1:SUBLANES, :], prev[SUBLANES - 2:SUBLANES - 1, :]
            t = row
        else:
            p1, p2 = cp1_ref[:, cs], cp0_ref[:, cs]
            t = _mod(row, seq_len)
        am1 = jnp.where(t == 0, p1, pltpu.roll(a, 1, 0))
        am2 = jnp.where(t == 0, p2, jnp.where(t == 1, p1, pltpu.roll(a, 2, 0)))
        cw = cw_ref[:, cs]
        conv = cb_ref[:, cs] + cw[0:1, :] * am2 + cw[1:2, :] * am1 + cw[2:3, :] * a
        gelu = 0.5 * conv * (1.0 + lax.erf(conv * (2.0 ** -0.5)))
        acc = acc + _dot((gelu * u).astype(BF16), wdn_ref[cs, :])
        tail_ref[:, cs] = a[tm - tail_rows:, :]
    x2 = _layer_norm(ALPHA * x + acc, g2_ref[...], b2_ref[...])
    gate = _sigmoid(_dot(x2.astype(BF16), wpg_ref[...]))
    emb = _dot(pe_ref[...].astype(BF16), wpp_ref[...])
    y_ref[...] = _layer_norm(ALPHA * x2 + gate * emb, g3_ref[...], b3_ref[...])


def _ffn(x2d, conv_prev, pe2d, w_up, conv_w, conv_b, w_down, ln2_g, ln2_b, w_pg, w_pp, ln3_g, ln3_b, *, tm, seq_len):
    m = x2d.shape[0]
    n_seq = m // seq_len
    row = lambda i: (i, 0)
    n_tiles = m // tm
    if seq_len >= tm:
        tiles_per_seq, tail_rows = seq_len // tm, SUBLANES
        cp = jnp.concatenate([jnp.zeros((n_seq, SUBLANES - (CONV_W - 1), D_FF), F32), conv_prev], axis=1)
        acts = [x2d, x2d, cp, pe2d]
        act_specs = [pl.BlockSpec((tm, D_MODEL), row),
                     pl.BlockSpec((SUBLANES, D_MODEL), lambda i: (jnp.maximum(i * (tm // SUBLANES) - 1, 0), 0)),
                     pl.BlockSpec((1, SUBLANES, D_FF), lambda i: (i // tiles_per_seq, 0, 0)),
                     pl.BlockSpec((tm, PLE_DIM), row)]
        act_bytes = 2 * _nbytes((tm, D_MODEL), F32) + _nbytes((SUBLANES, D_FF), F32)
    else:
        assert tm % seq_len == 0 and seq_len >= CONV_W - 1
        tiles_per_seq, tail_rows = 1, tm
        cp0 = jnp.repeat(conv_prev[:, 0, :], seq_len, axis=0)
        cp1 = jnp.repeat(conv_prev[:, 1, :], seq_len, axis=0)
        acts = [x2d, cp0, cp1, pe2d]
        act_specs = [pl.BlockSpec((tm, D_MODEL), row), pl.BlockSpec((tm, D_FF), row), pl.BlockSpec((tm, D_FF), row),
                     pl.BlockSpec((tm, PLE_DIM), row)]
        act_bytes = _nbytes((tm, D_MODEL), F32) + 2 * _nbytes((tm, D_FF), F32)
    weights = [w_up, conv_w, conv_b, w_down, ln2_g, ln2_b, w_pg, w_pp, ln3_g, ln3_b]
    est = (2 * act_bytes + sum(_nbytes(w.shape, w.dtype) for w in weights) + 4 * _nbytes((tm, D_MODEL), F32)
           + 2 * _nbytes((tail_rows, D_FF), F32) + 6 * _nbytes((tm, FF_CHUNK), F32))
    y, tail = pl.pallas_call(
        functools.partial(_ffn_body, tm=tm, tiles_per_seq=tiles_per_seq, seq_len=seq_len, tail_rows=tail_rows),
        grid=(n_tiles,),
        in_specs=act_specs + [_resident(w.shape) for w in weights],
        out_specs=[pl.BlockSpec((tm, D_MODEL), row), pl.BlockSpec((tail_rows, D_FF), row)],
        out_shape=[jax.ShapeDtypeStruct((m, D_MODEL), F32), jax.ShapeDtypeStruct((n_tiles * tail_rows, D_FF), F32)],
        compiler_params=pltpu.CompilerParams(dimension_semantics=("arbitrary",), vmem_limit_bytes=_vmem_limit(est)),
        name="ffn",
    )(*acts, *weights)
    return y, tail


def _tile_rows(m):
    return min(m, 512)


def _mixers(x, s0, pos_offset, wts, *, prompt):
    n_seq, t, _ = x.shape
    m = n_seq * t
    x2d = x.reshape(m, D_MODEL)
    act_dtype = BF16 if prompt else F32
    tm = _tile_rows(m)
    q, k, v, g, ld = _gla_proj(x2d, wts["w_gla"], wts["w_lr"], wts["w_gkb"], wts["b_gk"], tm=min(m, 2 * tm),
                               act_dtype=act_dtype)
    if prompt:
        qkv, kv_new = _dil_proj(x2d, wts["w_dil"], wts["inv_freq"], tm=tm, seq_len=t, pos_offset=pos_offset, act_dtype=act_dtype)
        kv_out = [jnp.transpose(kvn.reshape(n_seq, 2, DIL_HEADS, DIL_HD, -1), (0, 4, 1, 2, 3)) for kvn in kv_new]
    else:
        qkv, kv_new = _dil_proj_short(x2d, wts["w_dil"], wts["inv_freq"], seq_len=t, pos_offset=pos_offset)
        kv_out = [jnp.transpose(kv_new[g].reshape(t, 2, DIL_HEADS, DIL_HD, n_seq), (4, 0, 1, 2, 3)) for g in range(N_GROUPS)]
    r3 = lambda a: a.reshape(n_seq, t, a.shape[-1])
    o_gla, s_new = _gla_scan(r3(q), r3(k), r3(v), r3(g), r3(ld), s0, wts["gla_norm"],
                             nseq=n_seq if prompt else 16, cc=min(t, GLA_CHUNK))
    return dict(x2d=x2d, tm=tm, n_seq=n_seq, t=t, o_gla=o_gla.reshape(m, GLA_V), s_new=s_new, qkv=qkv, kv_out=kv_out)


def _finish(mix, dil_inputs, pe, conv_prev, wts, *, prompt):
    n_seq, t, tm = mix["n_seq"], mix["t"], mix["tm"]
    m = n_seq * t
    x1 = _merge(mix["x2d"], mix["o_gla"], dil_inputs, wts["w_gates"], wts["w_br_gla"], wts["w_br_dil"], wts["w_out"],
                wts["expand"], wts["ln1_g"], wts["ln1_b"], tm=tm, seq_len=t)
    tm_ffn = tm if prompt else min(tm, 256)
    y, tail = _ffn(x1, conv_prev, pe.reshape(m, PLE_DIM), wts["w_up"], wts["conv_w"], wts["conv_b"], wts["w_down"],
                   wts["ln2_g"], wts["ln2_b"], wts["w_ple_gate"], wts["w_ple_proj"], wts["ln3_g"], wts["ln3_b"],
                   tm=tm_ffn, seq_len=t)
    if t >= tm_ffn:
        conv_new = tail.reshape(n_seq, t // tm_ffn, SUBLANES, D_FF)[:, -1, SUBLANES - (CONV_W - 1):, :]
    else:
        conv_new = tail.reshape(n_seq, t, D_FF)[:, t - (CONV_W - 1):, :]
    return y.reshape(n_seq, t, D_MODEL), conv_new


def _forward(x_p, x_s, pe_p, pe_s, s0_s, conv_s, caches, wts):
    n_p = x_p.shape[0]
    mix_p = _mixers(x_p, jnp.zeros((n_p, GLA_HEADS, GLA_DK, GLA_DV), s0_s.dtype), 0, wts, prompt=True)
    mix_s = _mixers(x_s, s0_s, PAST_LEN, wts, prompt=False)
    o_p, lse_p, o_s = _attention(mix_p["qkv"], mix_s["qkv"], caches, n_seq_p=n_p, seq_len=x_p.shape[1],
                                 n_seq_s=x_s.shape[0], t_new=x_s.shape[1])
    y_p, conv_new_p = _finish(mix_p, o_p + lse_p, pe_p, jnp.zeros((n_p, CONV_W - 1, D_FF), x_p.dtype), wts, prompt=True)
    y_s, conv_new_s = _finish(mix_s, [o_s], pe_s, conv_s, wts, prompt=False)
    return (y_p, mix_p["s_new"], conv_new_p, mix_p["kv_out"]), (y_s, mix_s["s_new"], conv_new_s, mix_s["kv_out"])


def _prep_weights(w_in, w_gk_b, b_gk, gla_norm, w_br_gla, w_br_dil, w_out, ln1_g, ln1_b, w_up, conv_w, conv_b, w_down,
                  ln2_g, ln2_b, w_ple_gate, w_ple_proj, ln3_g, ln3_b):
    w_in_t = jnp.transpose(w_in[0])
    rows_bf16 = lambda lo, hi: w_in_t[lo:hi].astype(BF16)
    vec = lambda a: a[0].reshape(1, -1).astype(F32)
    half = DIL_HD // 2
    inv = ROPE_THETA ** (-jnp.arange(half, dtype=F32) / half)
    head_of_lane = jnp.arange(DIL_WIDTH) // DIL_HD
    expand = (jnp.arange(LANES)[:, None] == head_of_lane[None, :]).astype(BF16)
    return {
        "w_gla": rows_bf16(0, COL_GLA),
        "w_lr": jnp.pad(rows_bf16(COL_GLA, COL_LR), ((0, LANES - GLA_RANK), (0, 0))),
        "w_gkb": jnp.pad(w_gk_b[0].astype(BF16), ((0, LANES - GLA_RANK), (0, 0))),
        "b_gk": vec(b_gk),
        "w_dil": rows_bf16(COL_LR, COL_DIL),
        "w_gates": rows_bf16(COL_DIL, w_in_t.shape[0]),
        "inv_freq": jnp.tile(inv, LANES // half).reshape(1, LANES),
        "gla_norm": vec(gla_norm),
        "w_br_gla": w_br_gla[0].astype(BF16), "w_br_dil": w_br_dil[0].astype(BF16), "w_out": w_out[0].astype(BF16),
        "expand": expand,
        "ln1_g": vec(ln1_g), "ln1_b": vec(ln1_b),
        "w_up": w_up[0].astype(BF16),
        "conv_w": jnp.pad(conv_w[0].astype(F32), ((0, SUBLANES - CONV_W), (0, 0))),
        "conv_b": vec(conv_b),
        "w_down": w_down[0].astype(BF16),
        "ln2_g": vec(ln2_g), "ln2_b": vec(ln2_b),
        "w_ple_gate": w_ple_gate[0].astype(BF16), "w_ple_proj": w_ple_proj[0].astype(BF16),
        "ln3_g": vec(ln3_g), "ln3_b": vec(ln3_b),
    }


def kernel(x_prompt, x_sample, p_prompt, p_sample, state_gla, cache_conv, cache_kv_w128, cache_kv_w512, cache_kv_w2048, w_in, w_gk_b, b_gk, gla_norm, w_br_gla, w_br_dil, w_out, ln1_g, ln1_b, w_up, conv_w, conv_b, w_down, ln2_g, ln2_b, w_ple_gate, w_ple_proj, ln3_g, ln3_b):
    assert w_in.shape[0] == DEPTH == 1
    wts = _prep_weights(w_in, w_gk_b, b_gk, gla_norm, w_br_gla, w_br_dil, w_out, ln1_g, ln1_b, w_up, conv_w, conv_b,
                        w_down, ln2_g, ln2_b, w_ple_gate, w_ple_proj, ln3_g, ln3_b)
    caches = (cache_kv_w128[0], cache_kv_w512[0], cache_kv_w2048[0])
    (y_p, s_p, conv_p, kv_p), (y_s, s_s, conv_s, kv_s) = _forward(
        x_prompt, x_sample, p_prompt[0], p_sample[0], state_gla[0], cache_conv[0], caches, wts)
    return (y_p, y_s, s_p[None], s_s[None], conv_p[None], conv_s[None],
            kv_p[0][None], kv_p[1][None], kv_p[2][None], kv_s[0][None], kv_s[1][None], kv_s[2][None])
```
